```python
import math
import jax, jax.numpy as jnp
from jax import lax
import numpy as np

D_MODEL = 2048
BATCH = 4
SEQ = 2048
DEPTH = 1
DEC_BATCH = 128
DEC_SEQ = 1
PAST_LEN = 16384
PAGE_SIZE = 128

N_META = 16
D_CONV_A = D_MODEL
CONV_A_WIDTH = 31
D_CONV_B = D_MODEL
CONV_B_WIDTH = 3
N_PROJ = 2 * D_CONV_A + 3 * D_CONV_B + 2 * D_MODEL
PROJ_SPLITS = [D_CONV_A, 2 * D_CONV_A, 2 * D_CONV_A + D_CONV_B, 2 * D_CONV_A + 2 * D_CONV_B, 2 * D_CONV_A + 3 * D_CONV_B, 2 * D_CONV_A + 3 * D_CONV_B + D_MODEL]
N_GROUPS = 4
EXPERTS_PER_GROUP = 8
N_EXPERTS = N_GROUPS * EXPERTS_PER_GROUP
TOP_K_IN_GROUP = 2
D_EXPERT = D_MODEL // 4
MOE_BLOCK = 128
LN_EPS = 1e-5
DEEPNORM_ALPHA = (2 * DEPTH) ** 0.25
DEEPNORM_BETA = (8 * DEPTH) ** -0.25

kernel_name = 'hybrid_conformer_shortconv_hmoe_step'


def layer_norm(x, g, b):
    xf = x.astype(jnp.float32)
    mu = jnp.mean(xf, axis=-1, keepdims=True)
    var = jnp.mean(jnp.square(xf - mu), axis=-1, keepdims=True)
    y = (xf - mu) * lax.rsqrt(var + LN_EPS) * g.astype(jnp.float32) + b.astype(jnp.float32)
    return y.astype(x.dtype)


def causal_depthwise_conv(u, past, w):
    width = w.shape[0]
    full = jnp.concatenate([past.astype(u.dtype), u], axis=1)
    y = lax.conv_general_dilated(full, w[:, None, :].astype(u.dtype), window_strides=(1,), padding='VALID',
                                 dimension_numbers=('NWC', 'WIO', 'NWC'), feature_group_count=u.shape[-1])
    return y, full[:, full.shape[1] - (width - 1):]


def hierarchical_moe(x, w_rg, b_rg, w_re, b_re, w_gate, w_up, w_down):
    t, d = x.shape
    g_logits = (x @ w_rg + b_rg).astype(jnp.float32)
    g_prob = jax.nn.softmax(g_logits, axis=-1)
    g_sel = jnp.argmax(g_logits, axis=-1)
    g_w = jnp.take_along_axis(g_prob, g_sel[:, None], axis=1)[:, 0]
    e_logits = (jnp.einsum('td,gde->tge', x, w_re) + b_re).astype(jnp.float32)
    e_logits = jnp.take_along_axis(e_logits, g_sel[:, None, None], axis=1)[:, 0]
    e_prob = jax.nn.softmax(e_logits, axis=-1)
    top_p, top_i = lax.top_k(e_prob, TOP_K_IN_GROUP)
    comb = (g_w[:, None] * top_p / jnp.sum(top_p, axis=-1, keepdims=True)).reshape(-1)
    expert_id = (g_sel[:, None] * EXPERTS_PER_GROUP + top_i).reshape(-1).astype(jnp.int32)
    tok_id = jnp.repeat(jnp.arange(t, dtype=jnp.int32), TOP_K_IN_GROUP)
    n_assign = t * TOP_K_IN_GROUP
    n_blocks = -(-n_assign // MOE_BLOCK) + N_EXPERTS
    order = jnp.argsort(expert_id)
    e_sorted = expert_id[order]
    counts = jnp.bincount(expert_id, length=N_EXPERTS).astype(jnp.int32)
    padded = (counts + MOE_BLOCK - 1) // MOE_BLOCK * MOE_BLOCK
    pad_end = jnp.cumsum(padded)
    pad_start = pad_end - padded
    start = jnp.cumsum(counts) - counts
    dest = pad_start[e_sorted] + jnp.arange(n_assign, dtype=jnp.int32) - start[e_sorted]
    slot_tok = jnp.zeros((n_blocks * MOE_BLOCK,), jnp.int32).at[dest].set(tok_id[order])
    slot_w = jnp.zeros((n_blocks * MOE_BLOCK,), jnp.float32).at[dest].set(comb[order])
    block_expert = jnp.minimum(
        jnp.searchsorted(pad_end, jnp.arange(n_blocks, dtype=jnp.int32) * MOE_BLOCK, side='right'),
        N_EXPERTS - 1)

    def block_ffn(args):
        e, tok, w = args
        xb = x[tok]
        h = jax.nn.silu(xb @ w_gate[e]) * (xb @ w_up[e])
        return (h @ w_down[e]) * w[:, None].astype(x.dtype)

    out = lax.map(block_ffn, (block_expert, slot_tok.reshape(n_blocks, MOE_BLOCK), slot_w.reshape(n_blocks, MOE_BLOCK)))
    return jax.ops.segment_sum(out.reshape(-1, d), slot_tok, num_segments=t)


def hybrid_layer(x, past_a, past_b, w_in, b_in, conv_a_w, conv_a_b, ln_a_g, ln_a_b, w_a_out,
                 conv_b_w, w_b_out, w_o, ln1_g, ln1_b, w_rg, b_rg, w_re, b_re,
                 w_gate, w_up, w_down, ln2_g, ln2_b):
    bsz, seq_len, d = x.shape
    z = x @ w_in + b_in
    a_val, a_gate, b_gate, c_gate, h_in, gate_a, gate_b = jnp.split(z, PROJ_SPLITS, axis=-1)
    ua, new_a = causal_depthwise_conv(a_val * jax.nn.sigmoid(a_gate), past_a, conv_a_w)
    ya = jax.nn.silu(layer_norm(ua + conv_a_b, ln_a_g, ln_a_b)) @ w_a_out
    vb, new_b = causal_depthwise_conv(c_gate * h_in, past_b, conv_b_w)
    yb = (b_gate * vb) @ w_b_out
    mixed = (jax.nn.sigmoid(gate_a) * ya + jax.nn.sigmoid(gate_b) * yb) @ w_o
    x1 = layer_norm(DEEPNORM_ALPHA * x + mixed, ln1_g, ln1_b)
    f = hierarchical_moe(x1.reshape(-1, d), w_rg, b_rg, w_re, b_re, w_gate, w_up, w_down).reshape(bsz, seq_len, d)
    x2 = layer_norm(DEEPNORM_ALPHA * x1 + f, ln2_g, ln2_b)
    return x2, new_a, new_b


def _normal(k, shape, scale):
    return jax.random.normal(k, shape, jnp.float32) * scale


def setup_inputs(seed: int = 0) -> dict:
    key = jax.random.key(seed)
    ks = jax.random.split(key, 28)
    L = DEPTH
    return {
        'x_prompt': _normal(ks[0], (BATCH, SEQ, D_MODEL), 1.0),
        'x_sample': _normal(ks[1], (DEC_BATCH, DEC_SEQ, D_MODEL), 1.0),
        'state_conv_a': _normal(ks[2], (L, DEC_BATCH, CONV_A_WIDTH - 1, D_CONV_A), 1.0),
        'state_conv_b': _normal(ks[3], (L, DEC_BATCH, CONV_B_WIDTH - 1, D_CONV_B), 1.0),
        'meta_tokens': _normal(ks[4], (N_META, D_MODEL), 1.0),
        'w_in': _normal(ks[5], (L, D_MODEL, N_PROJ), D_MODEL ** -0.5),
        'b_in': _normal(ks[6], (L, N_PROJ), 0.02),
        'conv_a_w': _normal(ks[7], (L, CONV_A_WIDTH, D_CONV_A), CONV_A_WIDTH ** -0.5),
        'conv_a_b': _normal(ks[8], (L, D_CONV_A), 0.02),
        'ln_a_g': 1.0 + _normal(ks[9], (L, D_CONV_A), 0.02),
        'ln_a_b': _normal(ks[10], (L, D_CONV_A), 0.02),
        'w_a_out': _normal(ks[11], (L, D_CONV_A, D_MODEL), D_CONV_A ** -0.5),
        'conv_b_w': _normal(ks[12], (L, CONV_B_WIDTH, D_CONV_B), CONV_B_WIDTH ** -0.5),
        'w_b_out': _normal(ks[13], (L, D_CONV_B, D_MODEL), D_CONV_B ** -0.5),
        'w_o': _normal(ks[14], (L, D_MODEL, D_MODEL), DEEPNORM_BETA * D_MODEL ** -0.5),
        'ln1_g': 1.0 + _normal(ks[15], (L, D_MODEL), 0.02),
        'ln1_b': _normal(ks[16], (L, D_MODEL), 0.02),
        'w_router_group': _normal(ks[17], (L, D_MODEL, N_GROUPS), D_MODEL ** -0.5),
        'b_router_group': _normal(ks[18], (L, N_GROUPS), 0.01),
        'w_router_expert': _normal(ks[19], (L, N_GROUPS, D_MODEL, EXPERTS_PER_GROUP), D_MODEL ** -0.5),
        'b_router_expert': _normal(ks[20], (L, N_GROUPS, EXPERTS_PER_GROUP), 0.01),
        'w_exp_gate': _normal(ks[21], (L, N_EXPERTS, D_MODEL, D_EXPERT), D_MODEL ** -0.5),
        'w_exp_up': _normal(ks[22], (L, N_EXPERTS, D_MODEL, D_EXPERT), D_MODEL ** -0.5),
        'w_exp_down': _normal(ks[23], (L, N_EXPERTS, D_EXPERT, D_MODEL), DEEPNORM_BETA * D_EXPERT ** -0.5),
        'ln2_g': 1.0 + _normal(ks[24], (L, D_MODEL), 0.02),
        'ln2_b': _normal(ks[25], (L, D_MODEL), 0.02),
    }


def reference(x_prompt, x_sample, state_conv_a, state_conv_b, meta_tokens, w_in, b_in, conv_a_w, conv_a_b,
              ln_a_g, ln_a_b, w_a_out, conv_b_w, w_b_out, w_o, ln1_g, ln1_b, w_router_group, b_router_group,
              w_router_expert, b_router_expert, w_exp_gate, w_exp_up, w_exp_down, ln2_g, ln2_b):
    n_prompt = x_prompt.shape[0]
    meta = jnp.broadcast_to(meta_tokens[None].astype(x_prompt.dtype), (n_prompt, N_META, D_MODEL))
    h_p = jnp.concatenate([meta, x_prompt], axis=1)
    h_s = x_sample
    na_p, nb_p, na_s, nb_s = [], [], [], []
    for l in range(DEPTH):
        params = (w_in[l], b_in[l], conv_a_w[l], conv_a_b[l], ln_a_g[l], ln_a_b[l], w_a_out[l],
                  conv_b_w[l], w_b_out[l], w_o[l], ln1_g[l], ln1_b[l], w_router_group[l], b_router_group[l],
                  w_router_expert[l], b_router_expert[l], w_exp_gate[l], w_exp_up[l], w_exp_down[l],
                  ln2_g[l], ln2_b[l])
        zero_a = jnp.zeros((n_prompt, CONV_A_WIDTH - 1, D_CONV_A), h_p.dtype)
        zero_b = jnp.zeros((n_prompt, CONV_B_WIDTH - 1, D_CONV_B), h_p.dtype)
        h_p, a_p, b_p = hybrid_layer(h_p, zero_a, zero_b, *params)
        h_s, a_s, b_s = hybrid_layer(h_s, state_conv_a[l], state_conv_b[l], *params)
        na_p.append(a_p)
        nb_p.append(b_p)
        na_s.append(a_s)
        nb_s.append(b_s)
    return (h_p[:, N_META:], h_s, jnp.stack(na_p), jnp.stack(nb_p), jnp.stack(na_s), jnp.stack(nb_s))
```

```python
import functools

import jax
import jax.numpy as jnp
from jax import lax
from jax.experimental import pallas as pl
from jax.experimental.pallas import tpu as pltpu

F32 = jnp.float32
BF16 = jnp.bfloat16

LN_EPS = 1e-5
N_GROUPS = 4
EXPERTS_PER_GROUP = 8
N_EXPERTS = N_GROUPS * EXPERTS_PER_GROUP
TOP_K = 2
N_PROJ_BLOCKS = 7

VMEM_LIMIT_BYTES = 56 * 1024 * 1024
LANES = 128

IN_TM = 512
IN_TN = 256
CONV_TS = 512
CONV_TC = 256
CONV_RC = 32
HIST_A = 32
HIST_B = 8
MIX_TM = 256
MOE_BM = 128
CMB_TM = 128


def _cparams(n_axes):
    return pltpu.CompilerParams(dimension_semantics=("arbitrary",) * n_axes,
                                vmem_limit_bytes=VMEM_LIMIT_BYTES)


def _sigmoid(x):
    return 1.0 / (1.0 + jnp.exp(-x))


def _layer_norm(x, g, b):
    mu = jnp.mean(x, axis=-1, keepdims=True)
    xc = x - mu
    var = jnp.mean(xc * xc, axis=-1, keepdims=True)
    return xc * lax.rsqrt(var + LN_EPS) * g + b


def _inproj_kernel(n_prompt_tiles, xp_ref, xs_ref, *refs):
    w_refs = refs[0:7]
    b_refs = refs[7:14]
    ga_ref, cb_ref, bg_ref, sga_ref, sgb_ref = refs[14:19]
    wbf_ref, xbf_ref = refs[19:21]
    i = pl.program_id(1)

    @pl.when(i == 0)
    def _():
        for k in range(N_PROJ_BLOCKS):
            wbf_ref[k] = w_refs[k][...].astype(BF16)

    @pl.when(i < n_prompt_tiles)
    def _():
        xbf_ref[...] = xp_ref[...].astype(BF16)

    @pl.when(i >= n_prompt_tiles)
    def _():
        xbf_ref[...] = xs_ref[...]

    x = xbf_ref[...]

    def proj(k):
        return jnp.dot(x, wbf_ref[k], preferred_element_type=F32) + b_refs[k][...]

    ga_ref[...] = proj(0) * _sigmoid(proj(1))
    bg_ref[...] = proj(2).astype(BF16)
    cb_ref[...] = proj(3) * proj(4)
    sga_ref[...] = _sigmoid(proj(5)).astype(BF16)
    sgb_ref[...] = _sigmoid(proj(6)).astype(BF16)


def _inproj(xp, xs_bf, w, b):
    rp, d = xp.shape
    n_prompt_tiles = rp // IN_TM
    nj = d // IN_TN
    rows = rp + IN_TM
    grid = (nj, n_prompt_tiles + 1)
    w_specs = [pl.BlockSpec((d, IN_TN), functools.partial(lambda j, i, k: (0, k * nj + j), k=k))
               for k in range(N_PROJ_BLOCKS)]
    b_specs = [pl.BlockSpec((1, IN_TN), functools.partial(lambda j, i, k: (0, k * nj + j), k=k))
               for k in range(N_PROJ_BLOCKS)]
    out_spec = pl.BlockSpec((IN_TM, IN_TN), lambda j, i: (i, j))
    return pl.pallas_call(
        functools.partial(_inproj_kernel, n_prompt_tiles),
        grid=grid,
        in_specs=[pl.BlockSpec((IN_TM, d), lambda j, i: (jnp.minimum(i, n_prompt_tiles - 1), 0)),
                  pl.BlockSpec((IN_TM, d), lambda j, i: (0, 0))] + w_specs + b_specs,
        out_specs=[out_spec] * 5,
        out_shape=[jax.ShapeDtypeStruct((rows, d), F32), jax.ShapeDtypeStruct((rows, d), F32),
                   jax.ShapeDtypeStruct((rows, d), BF16), jax.ShapeDtypeStruct((rows, d), BF16),
                   jax.ShapeDtypeStruct((rows, d), BF16)],
        scratch_shapes=[pltpu.VMEM((N_PROJ_BLOCKS, d, IN_TN), BF16), pltpu.VMEM((IN_TM, d), BF16)],
        compiler_params=_cparams(2),
        name="inproj",
    )(xp, xs_bf, *([w] * N_PROJ_BLOCKS), *([b] * N_PROJ_BLOCKS))


def _conv_prompt_kernel(ka, kb, ga_ref, cb_ref, bg_ref, ha0_ref, hb0_ref, wa_ref, ba_ref, wb_ref,
                        ua_ref, pb_ref, sa_ref, sb_ref):
    s = pl.program_id(2)
    ts = ga_ref.shape[0]

    @pl.when(s == 0)
    def _():
        sa_ref[0:HIST_A, :] = ha0_ref[...]
        sb_ref[0:HIST_B, :] = hb0_ref[...]

    sa_ref[HIST_A:HIST_A + ts, :] = ga_ref[...]
    sb_ref[HIST_B:HIST_B + ts, :] = cb_ref[...]

    for r0 in range(0, ts, CONV_RC):
        acc = wa_ref[0:1, :] * sa_ref[pl.ds(HIST_A - (ka - 1) + r0, CONV_RC), :]
        for k in range(1, ka):
            acc = acc + wa_ref[k:k + 1, :] * sa_ref[pl.ds(HIST_A - (ka - 1) + k + r0, CONV_RC), :]
        ua_ref[r0:r0 + CONV_RC, :] = acc + ba_ref[...]
        accb = wb_ref[0:1, :] * sb_ref[pl.ds(HIST_B - (kb - 1) + r0, CONV_RC), :]
        for k in range(1, kb):
            accb = accb + wb_ref[k:k + 1, :] * sb_ref[pl.ds(HIST_B - (kb - 1) + k + r0, CONV_RC), :]
        pb_ref[r0:r0 + CONV_RC, :] = (bg_ref[r0:r0 + CONV_RC, :].astype(F32) * accb).astype(BF16)

    sa_ref[0:HIST_A, :] = sa_ref[ts:ts + HIST_A, :]
    sb_ref[0:HIST_B, :] = sb_ref[ts:ts + HIST_B, :]


def _conv_prompt(ga, cb, bg, hist_a0, hist_b0, wa, ba, wb, n_seq, seq):
    d = ga.shape[1]
    ka, kb = wa.shape[0], wb.shape[0]
    ns = seq // CONV_TS
    row_spec = pl.BlockSpec((CONV_TS, CONV_TC), lambda b, c, s: (b * ns + s, c))
    return pl.pallas_call(
        functools.partial(_conv_prompt_kernel, ka, kb),
        grid=(n_seq, d // CONV_TC, ns),
        in_specs=[row_spec, row_spec, row_spec,
                  pl.BlockSpec((HIST_A, CONV_TC), lambda b, c, s: (0, c)),
                  pl.BlockSpec((HIST_B, CONV_TC), lambda b, c, s: (0, c)),
                  pl.BlockSpec((ka, CONV_TC), lambda b, c, s: (0, c)),
                  pl.BlockSpec((1, CONV_TC), lambda b, c, s: (0, c)),
                  pl.BlockSpec((kb, CONV_TC), lambda b, c, s: (0, c))],
        out_specs=[row_spec, row_spec],
        out_shape=[jax.ShapeDtypeStruct((n_seq * seq, d), F32), jax.ShapeDtypeStruct((n_seq * seq, d), BF16)],
        scratch_shapes=[pltpu.VMEM((HIST_A + CONV_TS, CONV_TC), F32), pltpu.VMEM((HIST_B + CONV_TS, CONV_TC), F32)],
        compiler_params=_cparams(3),
        name="conv_prompt",
    )(ga, cb, bg, hist_a0, hist_b0, wa, ba, wb)


def _conv_sample_kernel(ka, kb, sta_ref, stb_ref, ga_ref, cb_ref, bg_ref, wa_ref, ba_ref, wb_ref,
                        ua_ref, pb_ref):
    acc = wa_ref[ka - 1:ka, :] * ga_ref[...]
    for k in range(ka - 1):
        acc = acc + wa_ref[k:k + 1, :] * sta_ref[:, k, :]
    ua_ref[...] = acc + ba_ref[...]
    accb = wb_ref[kb - 1:kb, :] * cb_ref[...]
    for k in range(kb - 1):
        accb = accb + wb_ref[k:k + 1, :] * stb_ref[:, k, :]
    pb_ref[...] = (bg_ref[...].astype(F32) * accb).astype(BF16)


def _conv_sample(state_a, state_b, ga, cb, bg, row_block, wa, ba, wb):
    n, _, d = state_a.shape
    ka, kb = wa.shape[0], wb.shape[0]
    row_spec = pl.BlockSpec((n, CONV_TC), lambda c: (row_block, c))
    out_spec = pl.BlockSpec((n, CONV_TC), lambda c: (0, c))
    return pl.pallas_call(
        functools.partial(_conv_sample_kernel, ka, kb),
        grid=(d // CONV_TC,),
        in_specs=[pl.BlockSpec((n, ka - 1, CONV_TC), lambda c: (0, 0, c)),
                  pl.BlockSpec((n, kb - 1, CONV_TC), lambda c: (0, 0, c)),
                  row_spec, row_spec, row_spec,
                  pl.BlockSpec((ka, CONV_TC), lambda c: (0, c)),
                  pl.BlockSpec((1, CONV_TC), lambda c: (0, c)),
                  pl.BlockSpec((kb, CONV_TC), lambda c: (0, c))],
        out_specs=[out_spec, out_spec],
        out_shape=[jax.ShapeDtypeStruct((n, d), F32), jax.ShapeDtypeStruct((n, d), BF16)],
        compiler_params=_cparams(1),
        name="conv_sample",
    )(state_a, state_b, ga, cb, bg, wa, ba, wb)


def _mixer_kernel(alpha, n_tiles, ua_ref, pb_ref, sga_ref, sgb_ref, x_ref, wa_ref, wb_ref, wo_ref,
                  lnag_ref, lnab_ref, ln1g_ref, ln1b_ref, wr_ref, br_ref, *refs):
    x1_ref, lg_ref = refs[-2:]
    i = pl.program_id(0)

    @pl.when(i < n_tiles)
    def _():
        un = _layer_norm(ua_ref[...], lnag_ref[...], lnab_ref[...])
        act = (un * _sigmoid(un)).astype(BF16)
        ya = jnp.dot(act, wa_ref[...], preferred_element_type=F32)
        yb = jnp.dot(pb_ref[...], wb_ref[...], preferred_element_type=F32)
        m = (sga_ref[...].astype(F32) * ya + sgb_ref[...].astype(F32) * yb).astype(BF16)
        mixed = jnp.dot(m, wo_ref[...], preferred_element_type=F32)
        x1 = _layer_norm(alpha * x_ref[...] + mixed, ln1g_ref[...], ln1b_ref[...])
        x1_ref[...] = x1
        hi = x1.astype(BF16)
        lo = (x1 - hi.astype(F32)).astype(BF16)
        a = jnp.dot(hi, wr_ref[...], preferred_element_type=F32)
        b = jnp.dot(lo, wr_ref[...], preferred_element_type=F32)
        lg_ref[...] = a[:, :LANES] + a[:, LANES:] + b[:, :LANES] + br_ref[...]

    if len(refs) == 4:
        x1_tail_ref, lg_tail_ref = refs[:2]
        n_tail = x1_tail_ref.shape[0]

        @pl.when(i == n_tiles)
        def _():
            x1_ref[0:n_tail, :] = x1_tail_ref[...]
            lg_ref[0:n_tail, :] = lg_tail_ref[...]


def _mixer(tm, n_tiles, gate_off, alpha, ua, pb, sga, sgb, x, wa, wb, wo,
           lnag, lnab, ln1g, ln1b, wr, br, tail=None):
    d = x.shape[1]
    last = n_tiles - 1
    in_spec = pl.BlockSpec((tm, d), lambda i: (jnp.minimum(i, last), 0))
    gate_spec = pl.BlockSpec((tm, d), lambda i: (jnp.minimum(i, last) + gate_off, 0))
    vec_spec = pl.BlockSpec((1, d), lambda i: (0, 0))
    w_spec = pl.BlockSpec((d, d), lambda i: (0, 0), pipeline_mode=pl.Buffered(1))
    in_specs = [in_spec, in_spec, gate_spec, gate_spec, in_spec, w_spec, w_spec, w_spec,
                vec_spec, vec_spec, vec_spec, vec_spec,
                pl.BlockSpec((d, 2 * LANES), lambda i: (0, 0)),
                pl.BlockSpec((1, LANES), lambda i: (0, 0))]
    args = [ua, pb, sga, sgb, x, wa, wb, wo, lnag, lnab, ln1g, ln1b, wr, br]
    n_rows, n_steps = n_tiles * tm, n_tiles
    if tail is not None:
        n_tail = tail[0].shape[0]
        assert n_tail <= tm
        in_specs += [pl.BlockSpec((n_tail, d), lambda i: (0, 0)), pl.BlockSpec((n_tail, LANES), lambda i: (0, 0))]
        args += list(tail)
        n_rows, n_steps = n_rows + n_tail, n_steps + 1
    return pl.pallas_call(
        functools.partial(_mixer_kernel, alpha, n_tiles),
        grid=(n_steps,),
        in_specs=in_specs,
        out_specs=[pl.BlockSpec((tm, d), lambda i: (i, 0)), pl.BlockSpec((tm, LANES), lambda i: (i, 0))],
        out_shape=[jax.ShapeDtypeStruct((n_rows, d), F32), jax.ShapeDtypeStruct((n_rows, LANES), F32)],
        compiler_params=_cparams(1),
        name="mixer",
    )(*args)


def _moe_kernel(be_ref, tok_ref, nused_ref, x_hbm, sw_ref, wg_ref, wu_ref, wd_ref, ys_ref,
                xbuf_ref, sem_ref, wgu_ref, wdb_ref):
    b = pl.program_id(0)
    n_used = nused_ref[0]
    de = wg_ref.shape[1]

    def gather(blk, slot, start):
        for r in range(MOE_BM):
            cp = pltpu.make_async_copy(x_hbm.at[pl.ds(tok_ref[blk * MOE_BM + r], 1)],
                                       xbuf_ref.at[slot, pl.ds(r, 1)], sem_ref.at[slot])
            if start:
                cp.start()
            else:
                cp.wait()

    @pl.when(jnp.logical_and(b == 0, n_used > 0))
    def _():
        gather(0, 0, True)

    @pl.when(b + 1 < n_used)
    def _():
        gather(b + 1, (b + 1) % 2, True)

    @pl.when(b < n_used)
    def _():
        prev = be_ref[jnp.maximum(b - 1, 0)]

        @pl.when(jnp.logical_or(b == 0, be_ref[b] != prev))
        def _():
            wgu_ref[:, 0:de] = wg_ref[...].astype(BF16)
            wgu_ref[:, de:2 * de] = wu_ref[...].astype(BF16)
            wdb_ref[...] = wd_ref[...].astype(BF16)

        slot = b % 2
        gather(b, slot, False)
        xb = xbuf_ref[slot].astype(BF16)
        gu = jnp.dot(xb, wgu_ref[...], preferred_element_type=F32)
        g = gu[:, 0:de]
        h = (g * _sigmoid(g) * gu[:, de:2 * de]).astype(BF16)
        y = jnp.dot(h, wdb_ref[...], preferred_element_type=F32)
        ys_ref[...] = y * sw_ref[...]

    @pl.when(b >= n_used)
    def _():
        ys_ref[...] = jnp.zeros_like(ys_ref)


def _moe(block_expert, slot_tok, n_used, x1, slot_w, wg, wu, wd):
    n_blocks = block_expert.shape[0]
    n_slots = slot_tok.shape[0]
    _, d, de = wg.shape
    grid_spec = pltpu.PrefetchScalarGridSpec(
        num_scalar_prefetch=3,
        grid=(n_blocks,),
        in_specs=[pl.BlockSpec(memory_space=pl.ANY),
                  pl.BlockSpec((MOE_BM, 1), lambda b, be, tok, nu: (b, 0)),
                  pl.BlockSpec((None, d, de), lambda b, be, tok, nu: (be[b], 0, 0)),
                  pl.BlockSpec((None, d, de), lambda b, be, tok, nu: (be[b], 0, 0)),
                  pl.BlockSpec((None, de, d), lambda b, be, tok, nu: (be[b], 0, 0))],
        out_specs=pl.BlockSpec((MOE_BM, d), lambda b, be, tok, nu: (b, 0)),
        scratch_shapes=[pltpu.VMEM((2, MOE_BM, d), F32), pltpu.SemaphoreType.DMA((2,)),
                        pltpu.VMEM((d, 2 * de), BF16), pltpu.VMEM((de, d), BF16)],
    )
    return pl.pallas_call(
        _moe_kernel,
        grid_spec=grid_spec,
        out_shape=jax.ShapeDtypeStruct((n_slots, d), F32),
        compiler_params=_cparams(1),
        name="moe_ffn",
    )(block_expert, slot_tok, n_used, x1, slot_w, wg, wu, wd)


def _combine_kernel(alpha, tile_off, dst_ref, ys_hbm, x1_ref, g_ref, b_ref, out_ref, buf_ref, sem_ref):
    i = pl.program_id(0)
    n = pl.num_programs(0)
    tm = x1_ref.shape[0]

    def gather(tile, slot, start):
        base = (tile + tile_off) * tm * TOP_K
        for r in range(tm):
            for k in range(TOP_K):
                cp = pltpu.make_async_copy(ys_hbm.at[pl.ds(dst_ref[base + r * TOP_K + k], 1)],
                                           buf_ref.at[slot, k, pl.ds(r, 1)], sem_ref.at[slot])
                if start:
                    cp.start()
                else:
                    cp.wait()

    @pl.when(i == 0)
    def _():
        gather(0, 0, True)

    @pl.when(i + 1 < n)
    def _():
        gather(i + 1, (i + 1) % 2, True)

    slot = i % 2
    gather(i, slot, False)
    f = buf_ref[slot, 0] + buf_ref[slot, 1]
    out_ref[...] = _layer_norm(alpha * x1_ref[...] + f, g_ref[...], b_ref[...])


def _combine(tm, n_tiles, tile_off, alpha, dst, ys, x1, g, b):
    d = x1.shape[1]
    grid_spec = pltpu.PrefetchScalarGridSpec(
        num_scalar_prefetch=1,
        grid=(n_tiles,),
        in_specs=[pl.BlockSpec(memory_space=pl.ANY),
                  pl.BlockSpec((tm, d), lambda i, dst: (i + tile_off, 0)),
                  pl.BlockSpec((1, d), lambda i, dst: (0, 0)),
                  pl.BlockSpec((1, d), lambda i, dst: (0, 0))],
        out_specs=pl.BlockSpec((tm, d), lambda i, dst: (i, 0)),
        scratch_shapes=[pltpu.VMEM((2, TOP_K, tm, d), F32), pltpu.SemaphoreType.DMA((2,))],
    )
    return pl.pallas_call(
        functools.partial(_combine_kernel, alpha, tile_off),
        grid_spec=grid_spec,
        out_shape=jax.ShapeDtypeStruct((n_tiles * tm, d), F32),
        compiler_params=_cparams(1),
        name="combine",
    )(dst, ys, x1, g, b)


def _route(logits):
    t = logits.shape[0]
    g_logits = logits[:, :N_GROUPS]
    g_prob = jax.nn.softmax(g_logits, axis=-1)
    g_sel = jnp.argmax(g_logits, axis=-1)
    g_w = jnp.take_along_axis(g_prob, g_sel[:, None], axis=1)[:, 0]
    e_all = logits[:, N_GROUPS:N_GROUPS + N_EXPERTS].reshape(t, N_GROUPS, EXPERTS_PER_GROUP)
    e_logits = jnp.take_along_axis(e_all, g_sel[:, None, None], axis=1)[:, 0]
    e_prob = jax.nn.softmax(e_logits, axis=-1)
    top_p, top_i = lax.top_k(e_prob, TOP_K)
    comb = (g_w[:, None] * top_p / jnp.sum(top_p, axis=-1, keepdims=True)).reshape(-1)
    expert_id = (g_sel[:, None] * EXPERTS_PER_GROUP + top_i).reshape(-1).astype(jnp.int32)
    n_assign = t * TOP_K
    n_blocks = -(-n_assign // MOE_BM) + N_EXPERTS
    tok_id = jnp.repeat(jnp.arange(t, dtype=jnp.int32), TOP_K)
    order = jnp.argsort(expert_id)
    e_sorted = expert_id[order]
    counts = jnp.bincount(expert_id, length=N_EXPERTS).astype(jnp.int32)
    padded = (counts + MOE_BM - 1) // MOE_BM * MOE_BM
    pad_end = jnp.cumsum(padded)
    pad_start = pad_end - padded
    start = jnp.cumsum(counts) - counts
    dest_sorted = pad_start[e_sorted] + jnp.arange(n_assign, dtype=jnp.int32) - start[e_sorted]
    slot_tok = jnp.zeros((n_blocks * MOE_BM,), jnp.int32).at[dest_sorted].set(tok_id[order])
    slot_w = jnp.zeros((n_blocks * MOE_BM,), F32).at[dest_sorted].set(comb[order])
    dest = jnp.zeros((n_assign,), jnp.int32).at[order].set(dest_sorted)
    block_expert = jnp.minimum(
        jnp.searchsorted(pad_end, jnp.arange(n_blocks, dtype=jnp.int32) * MOE_BM, side='right'),
        N_EXPERTS - 1).astype(jnp.int32)
    n_used = (pad_end[-1] // MOE_BM).astype(jnp.int32).reshape(1)
    return block_expert, slot_tok, slot_w[:, None], dest, n_used


def kernel(x_prompt, x_sample, state_conv_a, state_conv_b, meta_tokens, w_in, b_in, conv_a_w, conv_a_b, ln_a_g, ln_a_b, w_a_out, conv_b_w, w_b_out, w_o, ln1_g, ln1_b, w_router_group, b_router_group, w_router_expert, b_router_expert, w_exp_gate, w_exp_up, w_exp_down, ln2_g, ln2_b):
    depth = w_in.shape[0]
    assert depth == 1, "single-layer step only"
    n_seq, seq, d = x_prompt.shape
    n_s = x_sample.shape[0]
    n_meta = meta_tokens.shape[0]
    ka, kb = conv_a_w.shape[1], conv_b_w.shape[1]
    assert x_sample.shape[1] == 1 and seq % CONV_TS == 0 and seq >= ka - 1
    assert n_meta <= HIST_A and ka - 1 <= HIST_A and kb - 1 <= HIST_B and kb - 1 <= n_meta
    assert n_s + n_meta <= IN_TM and n_s % CMB_TM == 0
    alpha = (2.0 * depth) ** 0.25
    rp = n_seq * seq

    xp = x_prompt.reshape(rp, d)
    xs = jnp.concatenate([x_sample.reshape(n_s, d), meta_tokens,
                          jnp.zeros((IN_TM - n_s - n_meta, d), F32)], axis=0).astype(BF16)
    ga, cb, bg, sga, sgb = _inproj(xp, xs, w_in[0], b_in)

    ga_meta = ga[rp + n_s:rp + n_s + n_meta]
    cb_meta = cb[rp + n_s:rp + n_s + n_meta]
    hist_a0 = jnp.concatenate([jnp.zeros((HIST_A - n_meta, d), F32), ga_meta], axis=0)
    hist_b0 = jnp.concatenate([jnp.zeros((HIST_B - (kb - 1), d), F32), cb_meta[n_meta - (kb - 1):]], axis=0)
    ua_p, pb_p = _conv_prompt(ga, cb, bg, hist_a0, hist_b0, conv_a_w[0], conv_a_b, conv_b_w[0], n_seq, seq)
    ua_s, pb_s = _conv_sample(state_conv_a[0], state_conv_b[0], ga, cb, bg, rp // n_s,
                              conv_a_w[0], conv_a_b, conv_b_w[0])

    t = rp + n_s
    wr_f = jnp.concatenate([w_router_group[0], w_router_expert[0].transpose(1, 0, 2).reshape(d, N_EXPERTS),
                            jnp.zeros((d, LANES - N_GROUPS - N_EXPERTS), F32)], axis=1)
    wr_hi = wr_f.astype(BF16)
    wr_lo = (wr_f - wr_hi.astype(F32)).astype(BF16)
    wr = jnp.concatenate([wr_hi, wr_lo], axis=1)
    br = jnp.concatenate([b_router_group[0], b_router_expert[0].reshape(-1),
                          jnp.zeros((LANES - N_GROUPS - N_EXPERTS,), F32)])[None, :]
    wa_bf, wb_bf, wo_bf = w_a_out[0].astype(BF16), w_b_out[0].astype(BF16), w_o[0].astype(BF16)
    mix = functools.partial(_mixer, alpha=alpha, wa=wa_bf, wb=wb_bf, wo=wo_bf, lnag=ln_a_g, lnab=ln_a_b,
                            ln1g=ln1_g, ln1b=ln1_b, wr=wr, br=br)
    x1_s, lg_s = mix(n_s, 1, rp // n_s, ua=ua_s, pb=pb_s, sga=sga, sgb=sgb, x=x_sample.reshape(n_s, d))
    x1, lg = mix(MIX_TM, rp // MIX_TM, 0, ua=ua_p, pb=pb_p, sga=sga, sgb=sgb, x=xp, tail=(x1_s, lg_s))

    block_expert, slot_tok, slot_w, dest, n_used = _route(lg)
    ys = _moe(block_expert, slot_tok, n_used, x1, slot_w, w_exp_gate[0], w_exp_up[0], w_exp_down[0])
    y_p = _combine(CMB_TM, rp // CMB_TM, 0, alpha, dest, ys, x1, ln2_g, ln2_b)
    y_s = _combine(CMB_TM, n_s // CMB_TM, rp // CMB_TM, alpha, dest, ys, x1, ln2_g, ln2_b)

    ga_p = ga[:rp].reshape(n_seq, seq, d)
    cb_p = cb[:rp].reshape(n_seq, seq, d)
    new_a_p = ga_p[:, seq - (ka - 1):][None]
    new_b_p = cb_p[:, seq - (kb - 1):][None]
    new_a_s = jnp.concatenate([state_conv_a[0][:, 1:], ga[rp:rp + n_s][:, None]], axis=1)[None]
    new_b_s = jnp.concatenate([state_conv_b[0][:, 1:], cb[rp:rp + n_s][:, None]], axis=1)[None]
    return (y_p.reshape(n_seq, seq, d), y_s.reshape(n_s, 1, d), new_a_p, new_b_p, new_a_s, new_b_s)
```

```python
import functools

import jax
import jax.numpy as jnp
from jax import lax
from jax.experimental import pallas as pl
from jax.experimental.pallas import tpu as pltpu

F32 = jnp.float32
BF16 = jnp.bfloat16

LN_EPS = 1e-5
N_GROUPS = 4
EXPERTS_PER_GROUP = 8
N_EXPERTS = N_GROUPS * EXPERTS_PER_GROUP
TOP_K = 2
N_PROJ_BLOCKS = 7

VMEM_LIMIT_BYTES = 56 * 1024 * 1024
LANES = 128

IN_TM = 512
IN_TN = 256
CONV_TS = 512
CONV_TC = 256
CONV_RC = 32
HIST_A = 32
HIST_B = 8
MIX_TM = 256
MOE_BM = 128
CMB_TM = 128


def _cparams(n_axes):
    return pltpu.CompilerParams(dimension_semantics=("arbitrary",) * n_axes,
                                vmem_limit_bytes=VMEM_LIMIT_BYTES)


def _sigmoid(x):
    return 1.0 / (1.0 + jnp.exp(-x))


def _layer_norm(x, g, b):
    mu = jnp.mean(x, axis=-1, keepdims=True)
    xc = x - mu
    var = jnp.mean(xc * xc, axis=-1, keepdims=True)
    return xc * lax.rsqrt(var + LN_EPS) * g + b


def _inproj_kernel(n_prompt_tiles, xp_ref, xs_ref, *refs):
    w_refs = refs[0:7]
    b_refs = refs[7:14]
    ga_ref, cb_ref, bg_ref, sga_ref, sgb_ref = refs[14:19]
    wbf_ref, xbf_ref = refs[19:21]
    i = pl.program_id(1)

    @pl.when(i == 0)
    def _():
        for k in range(N_PROJ_BLOCKS):
            wbf_ref[k] = w_refs[k][...].astype(BF16)

    @pl.when(i < n_prompt_tiles)
    def _():
        xbf_ref[...] = xp_ref[...].astype(BF16)

    @pl.when(i >= n_prompt_tiles)
    def _():
        xbf_ref[...] = xs_ref[...]

    x = xbf_ref[...]

    def proj(k):
        return jnp.dot(x, wbf_ref[k], preferred_element_type=F32) + b_refs[k][...]

    ga_ref[...] = proj(0) * _sigmoid(proj(1))
    bg_ref[...] = proj(2).astype(BF16)
    cb_ref[...] = proj(3) * proj(4)
    sga_ref[...] = _sigmoid(proj(5)).astype(BF16)
    sgb_ref[...] = _sigmoid(proj(6)).astype(BF16)


def _inproj(xp, xs_bf, w, b):
    rp, d = xp.shape
    n_prompt_tiles = rp // IN_TM
    nj = d // IN_TN
    rows = rp + IN_TM
    grid = (nj, n_prompt_tiles + 1)
    w_specs = [pl.BlockSpec((d, IN_TN), functools.partial(lambda j, i, k: (0, k * nj + j), k=k))
               for k in range(N_PROJ_BLOCKS)]
    b_specs = [pl.BlockSpec((1, IN_TN), functools.partial(lambda j, i, k: (0, k * nj + j), k=k))
               for k in range(N_PROJ_BLOCKS)]
    out_spec = pl.BlockSpec((IN_TM, IN_TN), lambda j, i: (i, j))
    return pl.pallas_call(
        functools.partial(_inproj_kernel, n_prompt_tiles),
        grid=grid,
        in_specs=[pl.BlockSpec((IN_TM, d), lambda j, i: (jnp.minimum(i, n_prompt_tiles - 1), 0)),
                  pl.BlockSpec((IN_TM, d), lambda j, i: (0, 0))] + w_specs + b_specs,
        out_specs=[out_spec] * 5,
        out_shape=[jax.ShapeDtypeStruct((rows, d), F32), jax.ShapeDtypeStruct((rows, d), F32),
                   jax.ShapeDtypeStruct((rows, d), BF16), jax.ShapeDtypeStruct((rows, d), BF16),
                   jax.ShapeDtypeStruct((rows, d), BF16)],
        scratch_shapes=[pltpu.VMEM((N_PROJ_BLOCKS, d, IN_TN), BF16), pltpu.VMEM((IN_TM, d), BF16)],
        compiler_params=_cparams(2),
        name="inproj",
    )(xp, xs_bf, *([w] * N_PROJ_BLOCKS), *([b] * N_PROJ_BLOCKS))


def _conv_prompt_kernel(ka, kb, ga_ref, cb_ref, bg_ref, ha0_ref, hb0_ref, wa_ref, ba_ref, wb_ref,
                        ua_ref, pb_ref, sa_ref, sb_ref):
    s = pl.program_id(2)
    ts = ga_ref.shape[0]

    @pl.when(s == 0)
    def _():
        sa_ref[0:HIST_A, :] = ha0_ref[...]
        sb_ref[0:HIST_B, :] = hb0_ref[...]

    sa_ref[HIST_A:HIST_A + ts, :] = ga_ref[...]
    sb_ref[HIST_B:HIST_B + ts, :] = cb_ref[...]

    for r0 in range(0, ts, CONV_RC):
        acc = wa_ref[0:1, :] * sa_ref[pl.ds(HIST_A - (ka - 1) + r0, CONV_RC), :]
        for k in range(1, ka):
            acc = acc + wa_ref[k:k + 1, :] * sa_ref[pl.ds(HIST_A - (ka - 1) + k + r0, CONV_RC), :]
        ua_ref[r0:r0 + CONV_RC, :] = acc + ba_ref[...]
        accb = wb_ref[0:1, :] * sb_ref[pl.ds(HIST_B - (kb - 1) + r0, CONV_RC), :]
        for k in range(1, kb):
            accb = accb + wb_ref[k:k + 1, :] * sb_ref[pl.ds(HIST_B - (kb - 1) + k + r0, CONV_RC), :]
        pb_ref[r0:r0 + CONV_RC, :] = (bg_ref[r0:r0 + CONV_RC, :].astype(F32) * accb).astype(BF16)

    sa_ref[0:HIST_A, :] = sa_ref[ts:ts + HIST_A, :]
    sb_ref[0:HIST_B, :] = sb_ref[ts:ts + HIST_B, :]


def _conv_prompt(ga, cb, bg, hist_a0, hist_b0, wa, ba, wb, n_seq, seq):
    d = ga.shape[1]
    ka, kb = wa.shape[0], wb.shape[0]
    ns = seq // CONV_TS
    row_spec = pl.BlockSpec((CONV_TS, CONV_TC), lambda b, c, s: (b * ns + s, c))
    return pl.pallas_call(
        functools.partial(_conv_prompt_kernel, ka, kb),
        grid=(n_seq, d // CONV_TC, ns),
        in_specs=[row_spec, row_spec, row_spec,
                  pl.BlockSpec((HIST_A, CONV_TC), lambda b, c, s: (0, c)),
                  pl.BlockSpec((HIST_B, CONV_TC), lambda b, c, s: (0, c)),
                  pl.BlockSpec((ka, CONV_TC), lambda b, c, s: (0, c)),
                  pl.BlockSpec((1, CONV_TC), lambda b, c, s: (0, c)),
                  pl.BlockSpec((kb, CONV_TC), lambda b, c, s: (0, c))],
        out_specs=[row_spec, row_spec],
        out_shape=[jax.ShapeDtypeStruct((n_seq * seq, d), F32), jax.ShapeDtypeStruct((n_seq * seq, d), BF16)],
        scratch_shapes=[pltpu.VMEM((HIST_A + CONV_TS, CONV_TC), F32), pltpu.VMEM((HIST_B + CONV_TS, CONV_TC), F32)],
        compiler_params=_cparams(3),
        name="conv_prompt",
    )(ga, cb, bg, hist_a0, hist_b0, wa, ba, wb)


def _conv_sample_kernel(ka, kb, sta_ref, stb_ref, ga_ref, cb_ref, bg_ref, wa_ref, ba_ref, wb_ref,
                        ua_ref, pb_ref):
    acc = wa_ref[ka - 1:ka, :] * ga_ref[...]
    for k in range(ka - 1):
        acc = acc + wa_ref[k:k + 1, :] * sta_ref[:, k, :]
    ua_ref[...] = acc + ba_ref[...]
    accb = wb_ref[kb - 1:kb, :] * cb_ref[...]
    for k in range(kb - 1):
        accb = accb + wb_ref[k:k + 1, :] * stb_ref[:, k, :]
    pb_ref[...] = (bg_ref[...].astype(F32) * accb).astype(BF16)


def _conv_sample(state_a, state_b, ga, cb, bg, row_block, wa, ba, wb):
    n, _, d = state_a.shape
    ka, kb = wa.shape[0], wb.shape[0]
    row_spec = pl.BlockSpec((n, CONV_TC), lambda c: (row_block, c))
    out_spec = pl.BlockSpec((n, CONV_TC), lambda c: (0, c))
    return pl.pallas_call(
        functools.partial(_conv_sample_kernel, ka, kb),
        grid=(d // CONV_TC,),
        in_specs=[pl.BlockSpec((n, ka - 1, CONV_TC), lambda c: (0, 0, c)),
                  pl.BlockSpec((n, kb - 1, CONV_TC), lambda c: (0, 0, c)),
                  row_spec, row_spec, row_spec,
                  pl.BlockSpec((ka, CONV_TC), lambda c: (0, c)),
                  pl.BlockSpec((1, CONV_TC), lambda c: (0, c)),
                  pl.BlockSpec((kb, CONV_TC), lambda c: (0, c))],
        out_specs=[out_spec, out_spec],
        out_shape=[jax.ShapeDtypeStruct((n, d), F32), jax.ShapeDtypeStruct((n, d), BF16)],
        compiler_params=_cparams(1),
        name="conv_sample",
    )(state_a, state_b, ga, cb, bg, wa, ba, wb)


ROUTE_ID, ROUTE_RANK, ROUTE_W = 0, TOP_K, 2 * TOP_K


def _route_tile(lg, carry):
    tm = lg.shape[0]
    lane = lax.broadcasted_iota(jnp.int32, (tm, LANES), 1)
    neg_inf = jnp.float32(-jnp.inf)

    def first_max(v):
        m = jnp.max(v, axis=-1, keepdims=True)
        return m, jnp.min(jnp.where(v == m, lane, LANES), axis=-1, keepdims=True)

    g_mask = lane < N_GROUPS
    g_max, g_sel = first_max(jnp.where(g_mask, lg, neg_inf))
    g_w = 1.0 / jnp.sum(jnp.where(g_mask, jnp.exp(lg - g_max), 0.0), axis=-1, keepdims=True)
    lane0 = N_GROUPS + g_sel * EXPERTS_PER_GROUP
    e_lg = jnp.where(jnp.logical_and(lane >= lane0, lane < lane0 + EXPERTS_PER_GROUP), lg, neg_inf)
    m1, l1 = first_max(e_lg)
    m2, l2 = first_max(jnp.where(lane == l1, neg_inf, e_lg))
    r = jnp.exp(m2 - m1)
    c1 = g_w / (1.0 + r)
    c2 = g_w * r / (1.0 + r)

    a1 = lane == l1
    a2 = lane == l2
    hit = jnp.where(jnp.logical_or(a1, a2), 1.0, 0.0)
    row = lax.broadcasted_iota(jnp.int32, (tm, tm), 0)
    col = lax.broadcasted_iota(jnp.int32, (tm, tm), 1)
    before = jnp.where(col < row, 1.0, 0.0).astype(BF16)
    seen = jnp.dot(before, hit.astype(BF16), preferred_element_type=F32) + carry
    rank1 = jnp.sum(jnp.where(a1, seen, 0.0), axis=-1, keepdims=True)
    rank2 = jnp.sum(jnp.where(a2, seen, 0.0), axis=-1, keepdims=True)
    carry = carry + jnp.sum(hit, axis=0, keepdims=True)

    rec = jnp.zeros((tm, LANES), F32)
    fields = [(l1 - N_GROUPS).astype(F32), (l2 - N_GROUPS).astype(F32), rank1, rank2, c1, c2]
    for n, v in enumerate(fields):
        rec = jnp.where(lane == n, v, rec)
    return rec, carry


def _mixer_kernel(alpha, n_tiles, ua_ref, pb_ref, sga_ref, sgb_ref, x_ref, wa_ref, wb_ref, wo_ref,
                  lnag_ref, lnab_ref, ln1g_ref, ln1b_ref, wr_ref, br_ref, cnt0_ref, *refs):
    x1_ref, rt_ref, ri_ref, cnt_ref, carry_ref = refs[-5:]
    i = pl.program_id(0)

    @pl.when(i == 0)
    def _():
        carry_ref[...] = cnt0_ref[...]

    @pl.when(i < n_tiles)
    def _():
        un = _layer_norm(ua_ref[...], lnag_ref[...], lnab_ref[...])
        act = (un * _sigmoid(un)).astype(BF16)
        ya = jnp.dot(act, wa_ref[...], preferred_element_type=F32)
        yb = jnp.dot(pb_ref[...], wb_ref[...], preferred_element_type=F32)
        m = (sga_ref[...].astype(F32) * ya + sgb_ref[...].astype(F32) * yb).astype(BF16)
        mixed = jnp.dot(m, wo_ref[...], preferred_element_type=F32)
        x1 = _layer_norm(alpha * x_ref[...] + mixed, ln1g_ref[...], ln1b_ref[...])
        x1_ref[...] = x1
        hi = x1.astype(BF16)
        lo = (x1 - hi.astype(F32)).astype(BF16)
        a = jnp.dot(hi, wr_ref[...], preferred_element_type=F32)
        b = jnp.dot(lo, wr_ref[...], preferred_element_type=F32)
        lg = a[:, :LANES] + a[:, LANES:] + b[:, :LANES] + br_ref[...]
        rec, carry = _route_tile(lg, carry_ref[...])
        carry_ref[...] = carry
        rt_ref[...] = rec
        ri_ref[...] = rec.T[0:2 * TOP_K, :].astype(jnp.int32)

    if len(refs) == 8:
        x1_tail_ref, rt_tail_ref, ri_tail_ref = refs[:3]
        n_tail = x1_tail_ref.shape[0]

        @pl.when(i == n_tiles)
        def _():
            x1_ref[0:n_tail, :] = x1_tail_ref[...]
            rt_ref[0:n_tail, :] = rt_tail_ref[...]
            ri_ref[:, 0:n_tail] = ri_tail_ref[...]

    cnt_ref[...] = carry_ref[...]


def _mixer(tm, n_tiles, gate_off, alpha, ua, pb, sga, sgb, x, wa, wb, wo,
           lnag, lnab, ln1g, ln1b, wr, br, cnt0, tail=None):
    d = x.shape[1]
    last = n_tiles - 1
    in_spec = pl.BlockSpec((tm, d), lambda i: (jnp.minimum(i, last), 0))
    gate_spec = pl.BlockSpec((tm, d), lambda i: (jnp.minimum(i, last) + gate_off, 0))
    vec_spec = pl.BlockSpec((1, d), lambda i: (0, 0))
    lane_spec = pl.BlockSpec((1, LANES), lambda i: (0, 0))
    w_spec = pl.BlockSpec((d, d), lambda i: (0, 0), pipeline_mode=pl.Buffered(1))
    in_specs = [in_spec, in_spec, gate_spec, gate_spec, in_spec, w_spec, w_spec, w_spec,
                vec_spec, vec_spec, vec_spec, vec_spec,
                pl.BlockSpec((d, 2 * LANES), lambda i: (0, 0)), lane_spec, lane_spec]
    args = [ua, pb, sga, sgb, x, wa, wb, wo, lnag, lnab, ln1g, ln1b, wr, br, cnt0]
    n_rows, n_steps = n_tiles * tm, n_tiles
    if tail is not None:
        n_tail = tail[0].shape[0]
        assert n_tail <= tm
        in_specs += [pl.BlockSpec((n_tail, d), lambda i: (0, 0)), pl.BlockSpec((n_tail, LANES), lambda i: (0, 0)),
                     pl.BlockSpec((2 * TOP_K, n_tail), lambda i: (0, 0))]
        args += list(tail)
        n_rows, n_steps = n_rows + n_tail, n_steps + 1
    return pl.pallas_call(
        functools.partial(_mixer_kernel, alpha, n_tiles),
        grid=(n_steps,),
        in_specs=in_specs,
        out_specs=[pl.BlockSpec((tm, d), lambda i: (i, 0)), pl.BlockSpec((tm, LANES), lambda i: (i, 0)),
                   pl.BlockSpec((2 * TOP_K, tm), lambda i: (0, i)), lane_spec],
        out_shape=[jax.ShapeDtypeStruct((n_rows, d), F32), jax.ShapeDtypeStruct((n_rows, LANES), F32),
                   jax.ShapeDtypeStruct((2 * TOP_K, n_rows), jnp.int32), jax.ShapeDtypeStruct((1, LANES), F32)],
        scratch_shapes=[pltpu.VMEM((1, LANES), F32)],
        compiler_params=_cparams(1),
        name="mixer",
    )(*args)


def _plan_kernel(n_tok, n_blocks, ri_ref, cnt_ref, be_ref, tok_ref, dst_ref, nu_ref, start_ref):
    shift = MOE_BM.bit_length() - 1

    def per_expert(e, blk0):
        start_ref[e] = blk0 * MOE_BM
        nb = lax.shift_right_logical(cnt_ref[0, N_GROUPS + e] + (MOE_BM - 1), shift)

        def fill(j, carry):
            be_ref[blk0 + j] = e
            return carry

        lax.fori_loop(0, nb, fill, 0)
        return blk0 + nb

    n_used = lax.fori_loop(0, N_EXPERTS, per_expert, 0)
    nu_ref[0] = n_used

    def fill_rest(b, carry):
        be_ref[b] = N_EXPERTS - 1
        return carry

    lax.fori_loop(n_used, n_blocks, fill_rest, 0)

    def clear(s, carry):
        tok_ref[s] = 0
        return carry

    lax.fori_loop(0, n_blocks * MOE_BM, clear, 0, unroll=8)

    def place(t, carry):
        for k in range(TOP_K):
            slot = start_ref[ri_ref[ROUTE_ID + k, t]] + ri_ref[ROUTE_RANK + k, t]
            dst_ref[t * TOP_K + k] = slot
            tok_ref[slot] = t
        return carry

    lax.fori_loop(0, n_tok, place, 0, unroll=4)


def _plan(ri, cnt):
    n_tok = ri.shape[1]
    n_blocks = -(-n_tok * TOP_K // MOE_BM) + N_EXPERTS
    smem = pl.BlockSpec(memory_space=pltpu.SMEM)
    return pl.pallas_call(
        functools.partial(_plan_kernel, n_tok, n_blocks),
        in_specs=[smem, smem],
        out_specs=[smem, smem, smem, smem],
        out_shape=[jax.ShapeDtypeStruct((n_blocks,), jnp.int32), jax.ShapeDtypeStruct((n_blocks * MOE_BM,), jnp.int32),
                   jax.ShapeDtypeStruct((n_tok * TOP_K,), jnp.int32), jax.ShapeDtypeStruct((1,), jnp.int32)],
        scratch_shapes=[pltpu.SMEM((N_EXPERTS,), jnp.int32)],
        name="plan",
    )(ri, cnt)


def _moe_kernel(be_ref, tok_ref, nused_ref, x_hbm, wg_ref, wu_ref, wd_ref, ys_ref,
                xbuf_ref, sem_ref, wgu_ref, wdb_ref):
    b = pl.program_id(0)
    n_used = nused_ref[0]
    de = wg_ref.shape[1]

    def gather(blk, slot, start):
        for r in range(MOE_BM):
            cp = pltpu.make_async_copy(x_hbm.at[pl.ds(tok_ref[blk * MOE_BM + r], 1)],
                                       xbuf_ref.at[slot, pl.ds(r, 1)], sem_ref.at[slot])
            if start:
                cp.start()
            else:
                cp.wait()

    @pl.when(jnp.logical_and(b == 0, n_used > 0))
    def _():
        gather(0, 0, True)

    @pl.when(b + 1 < n_used)
    def _():
        gather(b + 1, (b + 1) % 2, True)

    @pl.when(b < n_used)
    def _():
        prev = be_ref[jnp.maximum(b - 1, 0)]

        @pl.when(jnp.logical_or(b == 0, be_ref[b] != prev))
        def _():
            wgu_ref[:, 0:de] = wg_ref[...].astype(BF16)
            wgu_ref[:, de:2 * de] = wu_ref[...].astype(BF16)
            wdb_ref[...] = wd_ref[...].astype(BF16)

        slot = b % 2
        gather(b, slot, False)
        xb = xbuf_ref[slot].astype(BF16)
        gu = jnp.dot(xb, wgu_ref[...], preferred_element_type=F32)
        g = gu[:, 0:de]
        h = (g * _sigmoid(g) * gu[:, de:2 * de]).astype(BF16)
        y = jnp.dot(h, wdb_ref[...], preferred_element_type=F32)
        ys_ref[...] = y

    @pl.when(b >= n_used)
    def _():
        ys_ref[...] = jnp.zeros_like(ys_ref)


def _moe(block_expert, slot_tok, n_used, x1, wg, wu, wd):
    n_blocks = block_expert.shape[0]
    n_slots = slot_tok.shape[0]
    _, d, de = wg.shape
    grid_spec = pltpu.PrefetchScalarGridSpec(
        num_scalar_prefetch=3,
        grid=(n_blocks,),
        in_specs=[pl.BlockSpec(memory_space=pl.ANY),
                  pl.BlockSpec((None, d, de), lambda b, be, tok, nu: (be[b], 0, 0)),
                  pl.BlockSpec((None, d, de), lambda b, be, tok, nu: (be[b], 0, 0)),
                  pl.BlockSpec((None, de, d), lambda b, be, tok, nu: (be[b], 0, 0))],
        out_specs=pl.BlockSpec((MOE_BM, d), lambda b, be, tok, nu: (b, 0)),
        scratch_shapes=[pltpu.VMEM((2, MOE_BM, d), F32), pltpu.SemaphoreType.DMA((2,)),
                        pltpu.VMEM((d, 2 * de), BF16), pltpu.VMEM((de, d), BF16)],
    )
    return pl.pallas_call(
        _moe_kernel,
        grid_spec=grid_spec,
        out_shape=jax.ShapeDtypeStruct((n_slots, d), F32),
        compiler_params=_cparams(1),
        name="moe_ffn",
    )(block_expert, slot_tok, n_used, x1, wg, wu, wd)


def _combine_kernel(alpha, tile_off, dst_ref, ys_hbm, x1_ref, rt_ref, g_ref, b_ref, out_ref, buf_ref, sem_ref):
    i = pl.program_id(0)
    n = pl.num_programs(0)
    tm = x1_ref.shape[0]

    def gather(tile, slot, start):
        base = (tile + tile_off) * tm * TOP_K
        for r in range(tm):
            for k in range(TOP_K):
                cp = pltpu.make_async_copy(ys_hbm.at[pl.ds(dst_ref[base + r * TOP_K + k], 1)],
                                           buf_ref.at[slot, k, pl.ds(r, 1)], sem_ref.at[slot])
                if start:
                    cp.start()
                else:
                    cp.wait()

    @pl.when(i == 0)
    def _():
        gather(0, 0, True)

    @pl.when(i + 1 < n)
    def _():
        gather(i + 1, (i + 1) % 2, True)

    slot = i % 2
    gather(i, slot, False)
    rt = rt_ref[...]
    f = rt[:, ROUTE_W:ROUTE_W + 1] * buf_ref[slot, 0] + rt[:, ROUTE_W + 1:ROUTE_W + 2] * buf_ref[slot, 1]
    out_ref[...] = _layer_norm(alpha * x1_ref[...] + f, g_ref[...], b_ref[...])


def _combine(tm, n_tiles, tile_off, alpha, dst, ys, x1, rt, g, b):
    d = x1.shape[1]
    grid_spec = pltpu.PrefetchScalarGridSpec(
        num_scalar_prefetch=1,
        grid=(n_tiles,),
        in_specs=[pl.BlockSpec(memory_space=pl.ANY),
                  pl.BlockSpec((tm, d), lambda i, dst: (i + tile_off, 0)),
                  pl.BlockSpec((tm, LANES), lambda i, dst: (i + tile_off, 0)),
                  pl.BlockSpec((1, d), lambda i, dst: (0, 0)),
                  pl.BlockSpec((1, d), lambda i, dst: (0, 0))],
        out_specs=pl.BlockSpec((tm, d), lambda i, dst: (i, 0)),
        scratch_shapes=[pltpu.VMEM((2, TOP_K, tm, d), F32), pltpu.SemaphoreType.DMA((2,))],
    )
    return pl.pallas_call(
        functools.partial(_combine_kernel, alpha, tile_off),
        grid_spec=grid_spec,
        out_shape=jax.ShapeDtypeStruct((n_tiles * tm, d), F32),
        compiler_params=_cparams(1),
        name="combine",
    )(dst, ys, x1, rt, g, b)


def kernel(x_prompt, x_sample, state_conv_a, state_conv_b, meta_tokens, w_in, b_in, conv_a_w, conv_a_b, ln_a_g, ln_a_b, w_a_out, conv_b_w, w_b_out, w_o, ln1_g, ln1_b, w_router_group, b_router_group, w_router_expert, b_router_expert, w_exp_gate, w_exp_up, w_exp_down, ln2_g, ln2_b):
    depth = w_in.shape[0]
    assert depth == 1, "single-layer step only"
    n_seq, seq, d = x_prompt.shape
    n_s = x_sample.shape[0]
    n_meta = meta_tokens.shape[0]
    ka, kb = conv_a_w.shape[1], conv_b_w.shape[1]
    assert x_sample.shape[1] == 1 and seq % CONV_TS == 0 and seq >= ka - 1
    assert n_meta <= HIST_A and ka - 1 <= HIST_A and kb - 1 <= HIST_B and kb - 1 <= n_meta
    assert n_s + n_meta <= IN_TM and n_s % CMB_TM == 0
    alpha = (2.0 * depth) ** 0.25
    rp = n_seq * seq

    xp = x_prompt.reshape(rp, d)
    xs = jnp.concatenate([x_sample.reshape(n_s, d), meta_tokens,
                          jnp.zeros((IN_TM - n_s - n_meta, d), F32)], axis=0).astype(BF16)
    ga, cb, bg, sga, sgb = _inproj(xp, xs, w_in[0], b_in)

    ga_meta = ga[rp + n_s:rp + n_s + n_meta]
    cb_meta = cb[rp + n_s:rp + n_s + n_meta]
    hist_a0 = jnp.concatenate([jnp.zeros((HIST_A - n_meta, d), F32), ga_meta], axis=0)
    hist_b0 = jnp.concatenate([jnp.zeros((HIST_B - (kb - 1), d), F32), cb_meta[n_meta - (kb - 1):]], axis=0)
    ua_p, pb_p = _conv_prompt(ga, cb, bg, hist_a0, hist_b0, conv_a_w[0], conv_a_b, conv_b_w[0], n_seq, seq)
    ua_s, pb_s = _conv_sample(state_conv_a[0], state_conv_b[0], ga, cb, bg, rp // n_s,
                              conv_a_w[0], conv_a_b, conv_b_w[0])

    t = rp + n_s
    wr_f = jnp.concatenate([w_router_group[0], w_router_expert[0].transpose(1, 0, 2).reshape(d, N_EXPERTS),
                            jnp.zeros((d, LANES - N_GROUPS - N_EXPERTS), F32)], axis=1)
    wr_hi = wr_f.astype(BF16)
    wr_lo = (wr_f - wr_hi.astype(F32)).astype(BF16)
    wr = jnp.concatenate([wr_hi, wr_lo], axis=1)
    br = jnp.concatenate([b_router_group[0], b_router_expert[0].reshape(-1),
                          jnp.zeros((LANES - N_GROUPS - N_EXPERTS,), F32)])[None, :]
    wa_bf, wb_bf, wo_bf = w_a_out[0].astype(BF16), w_b_out[0].astype(BF16), w_o[0].astype(BF16)
    mix = functools.partial(_mixer, alpha=alpha, wa=wa_bf, wb=wb_bf, wo=wo_bf, lnag=ln_a_g, lnab=ln_a_b,
                            ln1g=ln1_g, ln1b=ln1_b, wr=wr, br=br)
    cnt0 = jnp.zeros((1, LANES), F32)
    x1_s, rt_s, ri_s, cnt_s = mix(n_s, 1, rp // n_s, ua=ua_s, pb=pb_s, sga=sga, sgb=sgb,
                                  x=x_sample.reshape(n_s, d), cnt0=cnt0)
    x1, rt, ri, cnt = mix(MIX_TM, rp // MIX_TM, 0, ua=ua_p, pb=pb_p, sga=sga, sgb=sgb, x=xp, cnt0=cnt_s,
                          tail=(x1_s, rt_s, ri_s))

    block_expert, slot_tok, dest, n_used = _plan(ri, cnt.astype(jnp.int32))
    ys = _moe(block_expert, slot_tok, n_used, x1, w_exp_gate[0], w_exp_up[0], w_exp_down[0])
    y_p = _combine(CMB_TM, rp // CMB_TM, 0, alpha, dest, ys, x1, rt, ln2_g, ln2_b)
    y_s = _combine(CMB_TM, n_s // CMB_TM, rp // CMB_TM, alpha, dest, ys, x1, rt, ln2_g, ln2_b)

    new_a_p = jnp.stack([ga[(b + 1) * seq - (ka - 1):(b + 1) * seq] for b in range(n_seq)])[None]
    new_b_p = jnp.stack([cb[(b + 1) * seq - (kb - 1):(b + 1) * seq] for b in range(n_seq)])[None]
    new_a_s = jnp.concatenate([state_conv_a[0][:, 1:], ga[rp:rp + n_s][:, None]], axis=1)[None]
    new_b_s = jnp.concatenate([state_conv_b[0][:, 1:], cb[rp:rp + n_s][:, None]], axis=1)[None]
    return (y_p.reshape(n_seq, seq, d), y_s.reshape(n_s, 1, d), new_a_p, new_b_p, new_a_s, new_b_s)
```

```python
import functools

import jax
import jax.numpy as jnp
from jax import lax
from jax.experimental import pallas as pl
from jax.experimental.pallas import tpu as pltpu

F32 = jnp.float32
BF16 = jnp.bfloat16

LN_EPS = 1e-5
N_GROUPS = 4
EXPERTS_PER_GROUP = 8
N_EXPERTS = N_GROUPS * EXPERTS_PER_GROUP
TOP_K = 2
N_PROJ_BLOCKS = 7

VMEM_LIMIT_BYTES = 56 * 1024 * 1024
LANES = 128

IN_TM = 512
IN_TN = 256
IN_HM = 256
W_UNIT_ROWS = 1024
CONV_TC = 256
CONV_RC = 64
HIST_A = 32
HIST_B = 8
MIX_TM = 256
MOE_BM = 128
MOE_NC = 256
CMB_TM = 128
CMB_RC = 32


def _cparams(n_axes):
    return pltpu.CompilerParams(dimension_semantics=("arbitrary",) * n_axes,
                                vmem_limit_bytes=VMEM_LIMIT_BYTES)


def _sigmoid(x):
    return 1.0 / (1.0 + jnp.exp(-x))


def _layer_norm(x, g, b):
    mu = jnp.mean(x, axis=-1, keepdims=True)
    xc = x - mu
    var = jnp.mean(xc * xc, axis=-1, keepdims=True)
    return xc * lax.rsqrt(var + LN_EPS) * g + b


def _conv_taps(src_ref, w_ref, n_taps, hist, r0, rc, c0):
    base = hist - (n_taps - 1)
    acc = None
    for rho in range(8):
        offs = [o for o in range(base, base + n_taps) if o % 8 == rho]
        if not offs:
            continue
        lo = offs[0]
        x = src_ref[pl.ds(r0 + lo, rc + offs[-1] - lo), c0:c0 + LANES]
        for o in offs:
            term = w_ref[o - base:o - base + 1, c0:c0 + LANES] * x[o - lo:o - lo + rc]
            acc = term if acc is None else acc + term
    return acc


def _inproj_conv_kernel(ka, kb, tiles_per_seq, n_s, n_meta, xp_ref, xs_ref, wnext_ref, w_hbm, *refs):
    b_refs = refs[0:7]
    wa_ref, ba_ref, wb_ref = refs[7:10]
    ua_ref, pb_ref, sga_ref, sgb_ref, na_ref, nb_ref = refs[10:16]
    gas_ref, cbs_ref, bgs_ref, sgas_ref, sgbs_ref = refs[16:21]
    wbf_ref, stage_ref, sem_ref, xbf_ref, sa_ref, sb_ref, bg_ref, ha_ref, hb_ref = refs[21:30]
    j = pl.program_id(0)
    i = pl.program_id(1)
    nj = pl.num_programs(0)
    tm, d = xbf_ref.shape
    tn = sa_ref.shape[1]
    unit_rows = stage_ref.shape[1]
    units_per_block = d // unit_rows
    n_units = N_PROJ_BLOCKS * units_per_block

    @pl.when(jnp.logical_and(j == 0, i == 0))
    def _():
        def unit_copy(u):
            k, h = divmod(u, units_per_block)
            return pltpu.make_async_copy(
                w_hbm.at[pl.ds(h * unit_rows, unit_rows), pl.ds(k * d, tn)], stage_ref.at[u % 2], sem_ref.at[u % 2])

        unit_copy(0).start()
        for u in range(n_units):
            if u + 1 < n_units:
                unit_copy(u + 1).start()
            unit_copy(u).wait()
            k, h = divmod(u, units_per_block)
            wbf_ref[0, k, h * unit_rows:(h + 1) * unit_rows, :] = stage_ref[u % 2].astype(BF16)

    @pl.when(jnp.logical_and(i < n_units, j + 1 < nj))
    def _():
        k = i // units_per_block
        h = i % units_per_block
        row0 = pl.multiple_of(h * unit_rows, unit_rows)
        wbf_ref[(j + 1) % 2, k, pl.ds(row0, unit_rows), :] = wnext_ref[...].astype(BF16)

    @pl.when(i == 0)
    def _():
        xbf_ref[...] = xs_ref[...]

    @pl.when(i > 0)
    def _():
        xbf_ref[...] = xp_ref[...].astype(BF16)

    seq_pos = (i - 1) % tiles_per_seq

    @pl.when(jnp.logical_and(i > 0, seq_pos == 0))
    def _():
        sa_ref[0:HIST_A, :] = ha_ref[...]
        sb_ref[0:HIST_B, :] = hb_ref[...]

    slot = j % 2
    halves = list(range(0, tm, IN_HM))

    def proj(k, m0):
        return (jnp.dot(xbf_ref[m0:m0 + IN_HM, :], wbf_ref[slot, k], preferred_element_type=F32)
                + b_refs[k][...])

    chunks = [(c0, r0) for c0 in range(0, tn, LANES) for r0 in range(0, tm, CONV_RC)]

    def conv_a(c0, r0):
        acc = _conv_taps(sa_ref, wa_ref, ka, HIST_A, r0, CONV_RC, c0)
        ua_ref[r0:r0 + CONV_RC, c0:c0 + LANES] = acc + ba_ref[:, c0:c0 + LANES]

    def conv_b(c0, r0):
        accb = _conv_taps(sb_ref, wb_ref, kb, HIST_B, r0, CONV_RC, c0)
        pb_ref[r0:r0 + CONV_RC, c0:c0 + LANES] = (bg_ref[r0:r0 + CONV_RC, c0:c0 + LANES] * accb).astype(BF16)

    todo_a = [functools.partial(conv_a, c0, r0) for c0, r0 in chunks]
    todo_b = [functools.partial(conv_b, c0, r0) for c0, r0 in chunks]

    def run(todo, n):
        for _ in range(min(n, len(todo))):
            todo.pop(0)()

    for m0 in halves:
        sa_ref[HIST_A + m0:HIST_A + m0 + IN_HM, :] = proj(0, m0) * _sigmoid(proj(1, m0))
    for m0 in halves:
        p3 = proj(3, m0)
        run(todo_a, 2)
        p4 = proj(4, m0)
        run(todo_a, 2)
        sb_ref[HIST_B + m0:HIST_B + m0 + IN_HM, :] = p3 * p4
    for m0 in halves:
        bg_ref[m0:m0 + IN_HM, :] = proj(2, m0)
        run(todo_a, 2)
    for m0 in halves:
        sga_ref[m0:m0 + IN_HM, :] = _sigmoid(proj(5, m0)).astype(BF16)
        run(todo_a, 1)
        run(todo_b, 4)
    for m0 in halves:
        sgb_ref[m0:m0 + IN_HM, :] = _sigmoid(proj(6, m0)).astype(BF16)
        run(todo_a, 1)
        run(todo_b, 4)
    run(todo_a, len(todo_a))
    run(todo_b, len(todo_b))

    @pl.when(jnp.logical_and(i > 0, seq_pos == tiles_per_seq - 1))
    def _():
        na_ref[...] = sa_ref[HIST_A + tm - (ka - 1):HIST_A + tm, :]
        nb_ref[...] = sb_ref[HIST_B + tm - (kb - 1):HIST_B + tm, :]

    @pl.when(i == 0)
    def _():
        gas_ref[...] = sa_ref[HIST_A:HIST_A + tm, :]
        cbs_ref[...] = sb_ref[HIST_B:HIST_B + tm, :]
        bgs_ref[...] = bg_ref[...]
        sgas_ref[...] = sga_ref[...]
        sgbs_ref[...] = sgb_ref[...]
        ma, mb = min(n_meta, HIST_A), min(n_meta, HIST_B)
        meta_end = n_s + n_meta
        if ma < HIST_A:
            ha_ref[0:HIST_A - ma, :] = jnp.zeros((HIST_A - ma, tn), F32)
        if mb < HIST_B:
            hb_ref[0:HIST_B - mb, :] = jnp.zeros((HIST_B - mb, tn), F32)
        ha_ref[HIST_A - ma:HIST_A, :] = sa_ref[HIST_A + meta_end - ma:HIST_A + meta_end, :]
        hb_ref[HIST_B - mb:HIST_B, :] = sb_ref[HIST_B + meta_end - mb:HIST_B + meta_end, :]

    sa_ref[0:HIST_A, :] = sa_ref[tm:tm + HIST_A, :]
    sb_ref[0:HIST_B, :] = sb_ref[tm:tm + HIST_B, :]


def _inproj_conv(xp, xs_bf, w, b, wa, ba, wb, n_seq, seq, n_s, n_meta):
    rp, d = xp.shape
    ka, kb = wa.shape[0], wb.shape[0]
    tiles_per_seq = seq // IN_TM
    n_prompt_tiles = rp // IN_TM
    nj = d // IN_TN
    units_per_block = d // W_UNIT_ROWS
    n_units = N_PROJ_BLOCKS * units_per_block

    def wnext_map(j, i):
        u = jnp.where(j == nj - 1, n_units - 1, jnp.minimum(i, n_units - 1))
        col = jnp.minimum(j + 1, nj - 1)
        return (u % units_per_block, (u // units_per_block) * nj + col)

    def prow(j, i):
        return (jnp.maximum(i - 1, 0), j)

    b_specs = [pl.BlockSpec((1, IN_TN), functools.partial(lambda j, i, k: (0, k * nj + j), k=k))
               for k in range(N_PROJ_BLOCKS)]
    ch_spec = lambda rows: pl.BlockSpec((rows, IN_TN), lambda j, i: (0, j))
    prompt_spec = pl.BlockSpec((IN_TM, IN_TN), prow)
    state_spec = lambda rows: pl.BlockSpec((None, rows, IN_TN),
                                           lambda j, i: (jnp.maximum(i - 1, 0) // tiles_per_seq, 0, j))
    small_spec = pl.BlockSpec((IN_TM, IN_TN), lambda j, i: (0, j))
    sds = jax.ShapeDtypeStruct
    return pl.pallas_call(
        functools.partial(_inproj_conv_kernel, ka, kb, tiles_per_seq, n_s, n_meta),
        grid=(nj, n_prompt_tiles + 1),
        in_specs=[pl.BlockSpec((IN_TM, d), lambda j, i: (jnp.maximum(i - 1, 0), 0)),
                  pl.BlockSpec((IN_TM, d), lambda j, i: (0, 0)),
                  pl.BlockSpec((W_UNIT_ROWS, IN_TN), wnext_map),
                  pl.BlockSpec(memory_space=pl.ANY)] + b_specs + [ch_spec(ka), ch_spec(1), ch_spec(kb)],
        out_specs=[prompt_spec, prompt_spec, prompt_spec, prompt_spec, state_spec(ka - 1), state_spec(kb - 1),
                   small_spec, small_spec, small_spec, small_spec, small_spec],
        out_shape=[sds((rp, d), F32), sds((rp, d), BF16), sds((rp, d), BF16), sds((rp, d), BF16),
                   sds((n_seq, ka - 1, d), F32), sds((n_seq, kb - 1, d), F32),
                   sds((IN_TM, d), F32), sds((IN_TM, d), F32), sds((IN_TM, d), F32),
                   sds((IN_TM, d), BF16), sds((IN_TM, d), BF16)],
        scratch_shapes=[pltpu.VMEM((2, N_PROJ_BLOCKS, d, IN_TN), BF16),
                        pltpu.VMEM((2, W_UNIT_ROWS, IN_TN), F32), pltpu.SemaphoreType.DMA((2,)),
                        pltpu.VMEM((IN_TM, d), BF16),
                        pltpu.VMEM((HIST_A + IN_TM, IN_TN), F32), pltpu.VMEM((HIST_B + IN_TM, IN_TN), F32),
                        pltpu.VMEM((IN_TM, IN_TN), F32),
                        pltpu.VMEM((HIST_A, IN_TN), F32), pltpu.VMEM((HIST_B, IN_TN), F32)],
        compiler_params=_cparams(2),
        name="inproj_conv",
    )(xp, xs_bf, w, w, *([b] * N_PROJ_BLOCKS), wa, ba, wb)


def _conv_sample_kernel(ka, kb, sta_ref, stb_ref, ga_ref, cb_ref, bg_ref, wa_ref, ba_ref, wb_ref,
                        ua_ref, pb_ref):
    acc = wa_ref[ka - 1:ka, :] * ga_ref[...]
    for k in range(ka - 1):
        acc = acc + wa_ref[k:k + 1, :] * sta_ref[:, k, :]
    ua_ref[...] = acc + ba_ref[...]
    accb = wb_ref[kb - 1:kb, :] * cb_ref[...]
    for k in range(kb - 1):
        accb = accb + wb_ref[k:k + 1, :] * stb_ref[:, k, :]
    pb_ref[...] = (bg_ref[...] * accb).astype(BF16)


def _conv_sample(state_a, state_b, ga, cb, bg, row_block, wa, ba, wb):
    n, _, d = state_a.shape
    ka, kb = wa.shape[0], wb.shape[0]
    row_spec = pl.BlockSpec((n, CONV_TC), lambda c: (row_block, c))
    out_spec = pl.BlockSpec((n, CONV_TC), lambda c: (0, c))
    return pl.pallas_call(
        functools.partial(_conv_sample_kernel, ka, kb),
        grid=(d // CONV_TC,),
        in_specs=[pl.BlockSpec((n, ka - 1, CONV_TC), lambda c: (0, 0, c)),
                  pl.BlockSpec((n, kb - 1, CONV_TC), lambda c: (0, 0, c)),
                  row_spec, row_spec, row_spec,
                  pl.BlockSpec((ka, CONV_TC), lambda c: (0, c)),
                  pl.BlockSpec((1, CONV_TC), lambda c: (0, c)),
                  pl.BlockSpec((kb, CONV_TC), lambda c: (0, c))],
        out_specs=[out_spec, out_spec],
        out_shape=[jax.ShapeDtypeStruct((n, d), F32), jax.ShapeDtypeStruct((n, d), BF16)],
        compiler_params=_cparams(1),
        name="conv_sample",
    )(state_a, state_b, ga, cb, bg, wa, ba, wb)


ROUTE_ID, ROUTE_RANK, ROUTE_W = 0, TOP_K, 2 * TOP_K


def _route_tile(lg, carry):
    tm = lg.shape[0]
    lane = lax.broadcasted_iota(jnp.int32, (tm, LANES), 1)
    neg_inf = jnp.float32(-jnp.inf)

    def first_max(v):
        m = jnp.max(v, axis=-1, keepdims=True)
        return m, jnp.min(jnp.where(v == m, lane, LANES), axis=-1, keepdims=True)

    g_mask = lane < N_GROUPS
    g_max, g_sel = first_max(jnp.where(g_mask, lg, neg_inf))
    g_w = 1.0 / jnp.sum(jnp.where(g_mask, jnp.exp(lg - g_max), 0.0), axis=-1, keepdims=True)
    lane0 = N_GROUPS + g_sel * EXPERTS_PER_GROUP
    e_lg = jnp.where(jnp.logical_and(lane >= lane0, lane < lane0 + EXPERTS_PER_GROUP), lg, neg_inf)
    m1, l1 = first_max(e_lg)
    m2, l2 = first_max(jnp.where(lane == l1, neg_inf, e_lg))
    r = jnp.exp(m2 - m1)
    c1 = g_w / (1.0 + r)
    c2 = g_w * r / (1.0 + r)

    a1 = lane == l1
    a2 = lane == l2
    hit = jnp.where(jnp.logical_or(a1, a2), 1.0, 0.0)
    row = lax.broadcasted_iota(jnp.int32, (tm, tm), 0)
    col = lax.broadcasted_iota(jnp.int32, (tm, tm), 1)
    before = jnp.where(col < row, 1.0, 0.0).astype(BF16)
    seen = jnp.dot(before, hit.astype(BF16), preferred_element_type=F32) + carry
    rank1 = jnp.sum(jnp.where(a1, seen, 0.0), axis=-1, keepdims=True)
    rank2 = jnp.sum(jnp.where(a2, seen, 0.0), axis=-1, keepdims=True)
    carry = carry + jnp.sum(hit, axis=0, keepdims=True)

    rec = jnp.zeros((tm, LANES), F32)
    fields = [(l1 - N_GROUPS).astype(F32), (l2 - N_GROUPS).astype(F32), rank1, rank2, c1, c2]
    for n, v in enumerate(fields):
        rec = jnp.where(lane == n, v, rec)
    return rec, carry


def _mixer_kernel(alpha, n_tiles, ua_ref, pb_ref, sga_ref, sgb_ref, x_ref, wa_ref, wb_ref, wo_ref,
                  lnag_ref, lnab_ref, ln1g_ref, ln1b_ref, wr_ref, br_ref, cnt0_ref, *refs):
    x1_ref, rt_ref, ri_ref, cnt_ref, carry_ref = refs[-5:]
    i = pl.program_id(0)

    @pl.when(i == 0)
    def _():
        carry_ref[...] = cnt0_ref[...]

    @pl.when(i < n_tiles)
    def _():
        un = _layer_norm(ua_ref[...], lnag_ref[...], lnab_ref[...])
        act = (un * _sigmoid(un)).astype(BF16)
        ya = jnp.dot(act, wa_ref[...], preferred_element_type=F32)
        yb = jnp.dot(pb_ref[...], wb_ref[...], preferred_element_type=F32)
        m = (sga_ref[...].astype(F32) * ya + sgb_ref[...].astype(F32) * yb).astype(BF16)
        mixed = jnp.dot(m, wo_ref[...], preferred_element_type=F32)
        x1 = _layer_norm(alpha * x_ref[...] + mixed, ln1g_ref[...], ln1b_ref[...])
        x1_ref[...] = x1
        hi = x1.astype(BF16)
        lo = (x1 - hi.astype(F32)).astype(BF16)
        a = jnp.dot(hi, wr_ref[...], preferred_element_type=F32)
        b = jnp.dot(lo, wr_ref[...], preferred_element_type=F32)
        lg = a[:, :LANES] + a[:, LANES:] + b[:, :LANES] + br_ref[...]
        rec, carry = _route_tile(lg, carry_ref[...])
        carry_ref[...] = carry
        rt_ref[...] = rec
        ri_ref[...] = rec.T[0:2 * TOP_K, :].astype(jnp.int32)

    if len(refs) == 8:
        x1_tail_ref, rt_tail_ref, ri_tail_ref = refs[:3]
        n_tail = x1_tail_ref.shape[0]

        @pl.when(i == n_tiles)
        def _():
            x1_ref[0:n_tail, :] = x1_tail_ref[...]
            rt_ref[0:n_tail, :] = rt_tail_ref[...]
            ri_ref[:, 0:n_tail] = ri_tail_ref[...]

    cnt_ref[...] = carry_ref[...]


def _mixer(tm, n_tiles, gate_off, alpha, ua, pb, sga, sgb, x, wa, wb, wo,
           lnag, lnab, ln1g, ln1b, wr, br, cnt0, tail=None):
    d = x.shape[1]
    last = n_tiles - 1
    in_spec = pl.BlockSpec((tm, d), lambda i: (jnp.minimum(i, last), 0))
    gate_spec = pl.BlockSpec((tm, d), lambda i: (jnp.minimum(i, last) + gate_off, 0))
    vec_spec = pl.BlockSpec((1, d), lambda i: (0, 0))
    lane_spec = pl.BlockSpec((1, LANES), lambda i: (0, 0))
    w_spec = pl.BlockSpec((d, d), lambda i: (0, 0), pipeline_mode=pl.Buffered(1))
    in_specs = [in_spec, in_spec, gate_spec, gate_spec, in_spec, w_spec, w_spec, w_spec,
                vec_spec, vec_spec, vec_spec, vec_spec,
                pl.BlockSpec((d, 2 * LANES), lambda i: (0, 0)), lane_spec, lane_spec]
    args = [ua, pb, sga, sgb, x, wa, wb, wo, lnag, lnab, ln1g, ln1b, wr, br, cnt0]
    n_rows, n_steps = n_tiles * tm, n_tiles
    if tail is not None:
        n_tail = tail[0].shape[0]
        assert n_tail <= tm
        in_specs += [pl.BlockSpec((n_tail, d), lambda i: (0, 0)), pl.BlockSpec((n_tail, LANES), lambda i: (0, 0)),
                     pl.BlockSpec((2 * TOP_K, n_tail), lambda i: (0, 0))]
        args += list(tail)
        n_rows, n_steps = n_rows + n_tail, n_steps + 1
    return pl.pallas_call(
        functools.partial(_mixer_kernel, alpha, n_tiles),
        grid=(n_steps,),
        in_specs=in_specs,
        out_specs=[pl.BlockSpec((tm, d), lambda i: (i, 0)), pl.BlockSpec((tm, LANES), lambda i: (i, 0)),
                   pl.BlockSpec((2 * TOP_K, tm), lambda i: (0, i)), lane_spec],
        out_shape=[jax.ShapeDtypeStruct((n_rows, d), F32), jax.ShapeDtypeStruct((n_rows, LANES), F32),
                   jax.ShapeDtypeStruct((2 * TOP_K, n_rows), jnp.int32), jax.ShapeDtypeStruct((1, LANES), F32)],
        scratch_shapes=[pltpu.VMEM((1, LANES), F32)],
        compiler_params=_cparams(1),
        name="mixer",
    )(*args)


def _plan_kernel(n_tok, n_blocks, *refs):
    id_refs = refs[0:TOP_K]
    rank_refs = refs[TOP_K:2 * TOP_K]
    cnt_ref, be_ref, tok_ref, dst_ref, nu_ref, start_ref = refs[2 * TOP_K:]
    shift = MOE_BM.bit_length() - 1

    def per_expert(e, blk0):
        cnt = cnt_ref[0, N_GROUPS + e]
        nb = lax.shift_right_logical(cnt + (MOE_BM - 1), shift)
        start_ref[e] = blk0 * MOE_BM

        def fill(j, carry):
            be_ref[blk0 + j] = e
            return carry

        lax.fori_loop(0, nb, fill, 0)

        def pad(s, carry):
            tok_ref[s] = 0
            return carry

        lax.fori_loop(blk0 * MOE_BM + cnt, (blk0 + nb) * MOE_BM, pad, 0)
        return blk0 + nb

    n_used = lax.fori_loop(0, N_EXPERTS, per_expert, 0)
    nu_ref[0] = n_used

    def rest(b, carry):
        be_ref[b] = N_EXPERTS - 1

        def pad(s, c):
            tok_ref[b * MOE_BM + s] = 0
            return c

        lax.fori_loop(0, MOE_BM, pad, 0, unroll=8)
        return carry

    lax.fori_loop(n_used, n_blocks, rest, 0)

    def place(t, carry):
        for k in range(TOP_K):
            slot = start_ref[id_refs[k][t]] + rank_refs[k][t]
            dst_ref[k * n_tok + t] = slot
            tok_ref[slot] = t
        return carry

    lax.fori_loop(0, n_tok, place, 0, unroll=8)


def _plan(ids, ranks, cnt):
    n_tok = ids[0].shape[0]
    n_blocks = -(-n_tok * TOP_K // MOE_BM) + N_EXPERTS
    smem = pl.BlockSpec(memory_space=pltpu.SMEM)
    return pl.pallas_call(
        functools.partial(_plan_kernel, n_tok, n_blocks),
        in_specs=[smem] * (2 * TOP_K + 1),
        out_specs=[smem, smem, smem, smem],
        out_shape=[jax.ShapeDtypeStruct((n_blocks,), jnp.int32), jax.ShapeDtypeStruct((n_blocks * MOE_BM,), jnp.int32),
                   jax.ShapeDtypeStruct((TOP_K * n_tok,), jnp.int32), jax.ShapeDtypeStruct((1,), jnp.int32)],
        scratch_shapes=[pltpu.SMEM((N_EXPERTS,), jnp.int32)],
        name="plan",
    )(*ids, *ranks, cnt)


def _moe_kernel(be_ref, tok_ref, nused_ref, cnt_ref, x_hbm, wg_hbm, wu_hbm, wd_hbm, ys_ref,
                xbuf_ref, gsem_ref, wgf_ref, wuf_ref, wdf_ref, wsem_ref, wgu_ref, wdb_ref, ord_ref):
    b = pl.program_id(0)
    n_used = nused_ref[0]
    de2 = wgu_ref.shape[1]
    de = de2 // 2
    shift = MOE_BM.bit_length() - 1

    def weight_copies(e, slot):
        return [pltpu.make_async_copy(wg_hbm.at[e], wgf_ref.at[slot], wsem_ref.at[slot]),
                pltpu.make_async_copy(wu_hbm.at[e], wuf_ref.at[slot], wsem_ref.at[slot]),
                pltpu.make_async_copy(wd_hbm.at[e], wdf_ref.at[slot], wsem_ref.at[slot])]

    def row_copy(blk, slot, r):
        return pltpu.make_async_copy(x_hbm.at[pl.ds(tok_ref[blk * MOE_BM + r], 1)],
                                     xbuf_ref.at[slot, pl.ds(r, 1)], gsem_ref.at[slot])

    @pl.when(jnp.logical_and(b == 0, n_used > 0))
    def _():
        ord_ref[0] = 0
        for cp in weight_copies(be_ref[0], 0):
            cp.start()
        for r in range(MOE_BM):
            row_copy(0, 0, r).start()

    @pl.when(b < n_used)
    def _():
        e = be_ref[b]

        @pl.when(jnp.logical_or(b == 0, e != be_ref[jnp.maximum(b - 1, 0)]))
        def _():
            order = ord_ref[0]
            wslot = order % 2
            for cp in weight_copies(e, wslot):
                cp.wait()
            nxt = b + lax.shift_right_logical(cnt_ref[0, N_GROUPS + e] + (MOE_BM - 1), shift)

            @pl.when(nxt < n_used)
            def _():
                for cp in weight_copies(be_ref[jnp.minimum(nxt, be_ref.shape[0] - 1)], 1 - wslot):
                    cp.start()

            wgu_ref[:, 0:de] = wgf_ref[wslot].astype(BF16)
            wgu_ref[:, de:de2] = wuf_ref[wslot].astype(BF16)
            wdb_ref[...] = wdf_ref[wslot].astype(BF16)
            ord_ref[0] = order + 1

        slot = b % 2
        for r in range(MOE_BM):
            row_copy(b, slot, r).wait()
        nxt_blk = jnp.minimum(b + 1, n_used - 1)
        xb = xbuf_ref[slot].astype(BF16)
        n_chunks = de2 // MOE_NC
        per = MOE_BM // n_chunks
        gu = []
        for c in range(n_chunks):
            for r in range(c * per, (c + 1) * per):
                row_copy(nxt_blk, 1 - slot, r).start()
            gu.append(jnp.dot(xb, wgu_ref[:, c * MOE_NC:(c + 1) * MOE_NC], preferred_element_type=F32))
        half = n_chunks // 2
        y = None
        for c in range(half):
            g = gu[c]
            h = (g * _sigmoid(g) * gu[half + c]).astype(BF16)
            part = jnp.dot(h, wdb_ref[c * MOE_NC:(c + 1) * MOE_NC, :], preferred_element_type=F32)
            y = part if y is None else y + part
        ys_ref[...] = y

    @pl.when(b == n_used - 1)
    def _():
        for r in range(MOE_BM):
            row_copy(b, 1 - b % 2, r).wait()

    @pl.when(b >= n_used)
    def _():
        ys_ref[...] = jnp.zeros_like(ys_ref)


def _moe(block_expert, slot_tok, n_used, cnt, x1, wg, wu, wd):
    n_blocks = block_expert.shape[0]
    n_slots = slot_tok.shape[0]
    _, d, de = wg.shape
    any_spec = pl.BlockSpec(memory_space=pl.ANY)
    grid_spec = pltpu.PrefetchScalarGridSpec(
        num_scalar_prefetch=4,
        grid=(n_blocks,),
        in_specs=[any_spec, any_spec, any_spec, any_spec],
        out_specs=pl.BlockSpec((MOE_BM, d), lambda b, *_: (b, 0)),
        scratch_shapes=[pltpu.VMEM((2, MOE_BM, d), F32), pltpu.SemaphoreType.DMA((2,)),
                        pltpu.VMEM((2, d, de), F32), pltpu.VMEM((2, d, de), F32), pltpu.VMEM((2, de, d), F32),
                        pltpu.SemaphoreType.DMA((2,)),
                        pltpu.VMEM((d, 2 * de), BF16), pltpu.VMEM((de, d), BF16), pltpu.SMEM((1,), jnp.int32)],
    )
    return pl.pallas_call(
        _moe_kernel,
        grid_spec=grid_spec,
        out_shape=jax.ShapeDtypeStruct((n_slots, d), F32),
        compiler_params=_cparams(1),
        name="moe_ffn",
    )(block_expert, slot_tok, n_used, cnt, x1, wg, wu, wd)


def _combine_kernel(alpha, tile_off, n_tok, dst_ref, ys_hbm, x1_ref, rt_ref, g_ref, b_ref, out_ref,
                    buf_ref, sem_ref):
    i = pl.program_id(0)
    n = pl.num_programs(0)
    tm = x1_ref.shape[0]

    def row_copy(tile, slot, r, k):
        src = dst_ref[k * n_tok + (tile + tile_off) * tm + r]
        return pltpu.make_async_copy(ys_hbm.at[pl.ds(src, 1)], buf_ref.at[slot, k, pl.ds(r, 1)], sem_ref.at[slot])

    @pl.when(i == 0)
    def _():
        for r in range(tm):
            for k in range(TOP_K):
                row_copy(0, 0, r, k).start()

    slot = i % 2
    for r in range(tm):
        for k in range(TOP_K):
            row_copy(i, slot, r, k).wait()
    nxt = jnp.minimum(i + 1, n - 1)
    for r0 in range(0, tm, CMB_RC):
        for r in range(r0, r0 + CMB_RC):
            for k in range(TOP_K):
                row_copy(nxt, 1 - slot, r, k).start()
        rt = rt_ref[r0:r0 + CMB_RC, :]
        f = (rt[:, ROUTE_W:ROUTE_W + 1] * buf_ref[slot, 0, r0:r0 + CMB_RC, :]
             + rt[:, ROUTE_W + 1:ROUTE_W + 2] * buf_ref[slot, 1, r0:r0 + CMB_RC, :])
        out_ref[r0:r0 + CMB_RC, :] = _layer_norm(alpha * x1_ref[r0:r0 + CMB_RC, :] + f, g_ref[...], b_ref[...])

    @pl.when(i == n - 1)
    def _():
        for r in range(tm):
            for k in range(TOP_K):
                row_copy(i, 1 - slot, r, k).wait()


def _combine(tm, n_tiles, tile_off, alpha, dst, ys, x1, rt, g, b):
    n_tok, d = x1.shape
    grid_spec = pltpu.PrefetchScalarGridSpec(
        num_scalar_prefetch=1,
        grid=(n_tiles,),
        in_specs=[pl.BlockSpec(memory_space=pl.ANY),
                  pl.BlockSpec((tm, d), lambda i, *_: (i + tile_off, 0)),
                  pl.BlockSpec((tm, LANES), lambda i, *_: (i + tile_off, 0)),
                  pl.BlockSpec((1, d), lambda i, *_: (0, 0)),
                  pl.BlockSpec((1, d), lambda i, *_: (0, 0))],
        out_specs=pl.BlockSpec((tm, d), lambda i, *_: (i, 0)),
        scratch_shapes=[pltpu.VMEM((2, TOP_K, tm, d), F32), pltpu.SemaphoreType.DMA((2,))],
    )
    return pl.pallas_call(
        functools.partial(_combine_kernel, alpha, tile_off, n_tok),
        grid_spec=grid_spec,
        out_shape=jax.ShapeDtypeStruct((n_tiles * tm, d), F32),
        compiler_params=_cparams(1),
        name="combine",
    )(dst, ys, x1, rt, g, b)


def kernel(x_prompt, x_sample, state_conv_a, state_conv_b, meta_tokens, w_in, b_in, conv_a_w, conv_a_b, ln_a_g, ln_a_b, w_a_out, conv_b_w, w_b_out, w_o, ln1_g, ln1_b, w_router_group, b_router_group, w_router_expert, b_router_expert, w_exp_gate, w_exp_up, w_exp_down, ln2_g, ln2_b):
    depth = w_in.shape[0]
    assert depth == 1, "single-layer step only"
    n_seq, seq, d = x_prompt.shape
    n_s = x_sample.shape[0]
    n_meta = meta_tokens.shape[0]
    ka, kb = conv_a_w.shape[1], conv_b_w.shape[1]
    assert x_sample.shape[1] == 1 and seq % IN_TM == 0 and IN_TM >= ka - 1
    assert n_meta <= HIST_A and ka - 1 <= HIST_A and kb - 1 <= HIST_B and kb - 1 <= n_meta
    assert n_s + n_meta <= IN_TM and n_s % CMB_TM == 0
    alpha = (2.0 * depth) ** 0.25
    rp = n_seq * seq

    xp = x_prompt.reshape(rp, d)
    xs = jnp.concatenate([x_sample.reshape(n_s, d), meta_tokens,
                          jnp.zeros((IN_TM - n_s - n_meta, d), F32)], axis=0).astype(BF16)
    (ua_p, pb_p, sga_p, sgb_p, new_a_p, new_b_p, ga_x, cb_x, bg_x, sga_x, sgb_x) = _inproj_conv(
        xp, xs, w_in[0], b_in, conv_a_w[0], conv_a_b, conv_b_w[0], n_seq, seq, n_s, n_meta)
    ua_s, pb_s = _conv_sample(state_conv_a[0], state_conv_b[0], ga_x, cb_x, bg_x, 0,
                              conv_a_w[0], conv_a_b, conv_b_w[0])

    t = rp + n_s
    wr_f = jnp.concatenate([w_router_group[0], w_router_expert[0].transpose(1, 0, 2).reshape(d, N_EXPERTS),
                            jnp.zeros((d, LANES - N_GROUPS - N_EXPERTS), F32)], axis=1)
    wr_hi = wr_f.astype(BF16)
    wr_lo = (wr_f - wr_hi.astype(F32)).astype(BF16)
    wr = jnp.concatenate([wr_hi, wr_lo], axis=1)
    br = jnp.concatenate([b_router_group[0], b_router_expert[0].reshape(-1),
                          jnp.zeros((LANES - N_GROUPS - N_EXPERTS,), F32)])[None, :]
    wa_bf, wb_bf, wo_bf = w_a_out[0].astype(BF16), w_b_out[0].astype(BF16), w_o[0].astype(BF16)
    mix = functools.partial(_mixer, alpha=alpha, wa=wa_bf, wb=wb_bf, wo=wo_bf, lnag=ln_a_g, lnab=ln_a_b,
                            ln1g=ln1_g, ln1b=ln1_b, wr=wr, br=br)
    cnt0 = jnp.zeros((1, LANES), F32)
    x1_s, rt_s, ri_s, cnt_s = mix(n_s, 1, 0, ua=ua_s, pb=pb_s, sga=sga_x, sgb=sgb_x,
                                  x=x_sample.reshape(n_s, d), cnt0=cnt0)
    x1, rt, ri, cnt = mix(MIX_TM, rp // MIX_TM, 0, ua=ua_p, pb=pb_p, sga=sga_p, sgb=sgb_p, x=xp, cnt0=cnt_s,
                          tail=(x1_s, rt_s, ri_s))

    cnt_i = cnt.astype(jnp.int32)
    ids = [ri[ROUTE_ID + k] for k in range(TOP_K)]
    ranks = [ri[ROUTE_RANK + k] for k in range(TOP_K)]
    block_expert, slot_tok, dest, n_used = _plan(ids, ranks, cnt_i)
    ys = _moe(block_expert, slot_tok, n_used, cnt_i, x1, w_exp_gate[0], w_exp_up[0], w_exp_down[0])
    y_p = _combine(CMB_TM, rp // CMB_TM, 0, alpha, dest, ys, x1, rt, ln2_g, ln2_b)
    y_s = _combine(CMB_TM, n_s // CMB_TM, rp // CMB_TM, alpha, dest, ys, x1, rt, ln2_g, ln2_b)

    new_a_s = jnp.concatenate([state_conv_a[0][:, 1:], ga_x[:n_s][:, None]], axis=1)
    new_b_s = jnp.concatenate([state_conv_b[0][:, 1:], cb_x[:n_s][:, None]], axis=1)
    return (y_p.reshape(n_seq, seq, d), y_s.reshape(n_s, 1, d), new_a_p[None], new_b_p[None], new_a_s[None], new_b_s[None])
```

```python
import functools

import jax
import jax.numpy as jnp
from jax import lax
from jax.experimental import pallas as pl
from jax.experimental.pallas import tpu as pltpu

F32 = jnp.float32
BF16 = jnp.bfloat16

LN_EPS = 1e-5
N_GROUPS = 4
EXPERTS_PER_GROUP = 8
N_EXPERTS = N_GROUPS * EXPERTS_PER_GROUP
TOP_K = 2
N_PROJ_BLOCKS = 7

VMEM_LIMIT_BYTES = 56 * 1024 * 1024
LANES = 128

IN_TM = 512
IN_TN = 256
IN_HM = 256
W_UNIT_ROWS = 1024
CONV_TC = 256
CONV_RC = 64
HIST_A = 32
HIST_B = 8
MIX_TM = 256
MOE_BM = 128
MOE_NC = 256
MOE_AHEAD = 4
WEIGHT_DMA_PRIORITY = 1
CMB_TM = 128
CMB_RC = 32


def _cparams(n_axes):
    return pltpu.CompilerParams(dimension_semantics=("arbitrary",) * n_axes,
                                vmem_limit_bytes=VMEM_LIMIT_BYTES)


def _sigmoid(x):
    return 1.0 / (1.0 + jnp.exp(-x))


def _layer_norm(x, g, b):
    mu = jnp.mean(x, axis=-1, keepdims=True)
    xc = x - mu
    var = jnp.mean(xc * xc, axis=-1, keepdims=True)
    return xc * lax.rsqrt(var + LN_EPS) * g + b


def _conv_taps(src_ref, w_ref, n_taps, hist, r0, rc, c0):
    base = hist - (n_taps - 1)
    acc = None
    for rho in range(8):
        offs = [o for o in range(base, base + n_taps) if o % 8 == rho]
        if not offs:
            continue
        lo = offs[0]
        x = src_ref[pl.ds(r0 + lo, rc + offs[-1] - lo), c0:c0 + LANES]
        for o in offs:
            term = w_ref[o - base:o - base + 1, c0:c0 + LANES] * x[o - lo:o - lo + rc]
            acc = term if acc is None else acc + term
    return acc


def _inproj_conv_kernel(ka, kb, tiles_per_seq, n_s, n_meta, xp_ref, xs_ref, wnext_ref, w_hbm, *refs):
    b_refs = refs[0:7]
    wa_ref, ba_ref, wb_ref = refs[7:10]
    ua_ref, pb_ref, sga_ref, sgb_ref, na_ref, nb_ref = refs[10:16]
    gas_ref, cbs_ref, bgs_ref, sgas_ref, sgbs_ref = refs[16:21]
    wbf_ref, stage_ref, sem_ref, xbf_ref, sa_ref, sb_ref, bg_ref, ha_ref, hb_ref = refs[21:30]
    j = pl.program_id(0)
    i = pl.program_id(1)
    nj = pl.num_programs(0)
    tm, d = xbf_ref.shape
    tn = sa_ref.shape[1]
    unit_rows = stage_ref.shape[1]
    units_per_block = d // unit_rows
    n_units = N_PROJ_BLOCKS * units_per_block

    @pl.when(jnp.logical_and(j == 0, i == 0))
    def _():
        def unit_copy(u):
            k, h = divmod(u, units_per_block)
            return pltpu.make_async_copy(
                w_hbm.at[pl.ds(h * unit_rows, unit_rows), pl.ds(k * d, tn)], stage_ref.at[u % 2], sem_ref.at[u % 2])

        unit_copy(0).start()
        for u in range(n_units):
            if u + 1 < n_units:
                unit_copy(u + 1).start()
            unit_copy(u).wait()
            k, h = divmod(u, units_per_block)
            wbf_ref[0, k, h * unit_rows:(h + 1) * unit_rows, :] = stage_ref[u % 2].astype(BF16)

    @pl.when(jnp.logical_and(i < n_units, j + 1 < nj))
    def _():
        k = i // units_per_block
        h = i % units_per_block
        row0 = pl.multiple_of(h * unit_rows, unit_rows)
        wbf_ref[(j + 1) % 2, k, pl.ds(row0, unit_rows), :] = wnext_ref[...].astype(BF16)

    @pl.when(i == 0)
    def _():
        xbf_ref[...] = xs_ref[...]

    @pl.when(i > 0)
    def _():
        xbf_ref[...] = xp_ref[...].astype(BF16)

    seq_pos = (i - 1) % tiles_per_seq

    @pl.when(jnp.logical_and(i > 0, seq_pos == 0))
    def _():
        sa_ref[0:HIST_A, :] = ha_ref[...]
        sb_ref[0:HIST_B, :] = hb_ref[...]

    slot = j % 2
    halves = list(range(0, tm, IN_HM))

    def proj(k, m0):
        return (jnp.dot(xbf_ref[m0:m0 + IN_HM, :], wbf_ref[slot, k], preferred_element_type=F32)
                + b_refs[k][...])

    chunks = [(c0, r0) for c0 in range(0, tn, LANES) for r0 in range(0, tm, CONV_RC)]

    def conv_a(c0, r0):
        acc = _conv_taps(sa_ref, wa_ref, ka, HIST_A, r0, CONV_RC, c0)
        ua_ref[r0:r0 + CONV_RC, c0:c0 + LANES] = acc + ba_ref[:, c0:c0 + LANES]

    def conv_b(c0, r0):
        accb = _conv_taps(sb_ref, wb_ref, kb, HIST_B, r0, CONV_RC, c0)
        pb_ref[r0:r0 + CONV_RC, c0:c0 + LANES] = (bg_ref[r0:r0 + CONV_RC, c0:c0 + LANES] * accb).astype(BF16)

    todo_a = [functools.partial(conv_a, c0, r0) for c0, r0 in chunks]
    todo_b = [functools.partial(conv_b, c0, r0) for c0, r0 in chunks]

    def run(todo, n):
        for _ in range(min(n, len(todo))):
            todo.pop(0)()

    for m0 in halves:
        sa_ref[HIST_A + m0:HIST_A + m0 + IN_HM, :] = proj(0, m0) * _sigmoid(proj(1, m0))
    for m0 in halves:
        p3 = proj(3, m0)
        run(todo_a, 2)
        p4 = proj(4, m0)
        run(todo_a, 2)
        sb_ref[HIST_B + m0:HIST_B + m0 + IN_HM, :] = p3 * p4
    for m0 in halves:
        bg_ref[m0:m0 + IN_HM, :] = proj(2, m0)
        run(todo_a, 2)
    for m0 in halves:
        sga_ref[m0:m0 + IN_HM, :] = _sigmoid(proj(5, m0)).astype(BF16)
        run(todo_a, 1)
        run(todo_b, 4)
    for m0 in halves:
        sgb_ref[m0:m0 + IN_HM, :] = _sigmoid(proj(6, m0)).astype(BF16)
        run(todo_a, 1)
        run(todo_b, 4)
    run(todo_a, len(todo_a))
    run(todo_b, len(todo_b))

    @pl.when(jnp.logical_and(i > 0, seq_pos == tiles_per_seq - 1))
    def _():
        na_ref[...] = sa_ref[HIST_A + tm - (ka - 1):HIST_A + tm, :]
        nb_ref[...] = sb_ref[HIST_B + tm - (kb - 1):HIST_B + tm, :]

    @pl.when(i == 0)
    def _():
        gas_ref[...] = sa_ref[HIST_A:HIST_A + tm, :]
        cbs_ref[...] = sb_ref[HIST_B:HIST_B + tm, :]
        bgs_ref[...] = bg_ref[...]
        sgas_ref[...] = sga_ref[...]
        sgbs_ref[...] = sgb_ref[...]
        ma, mb = min(n_meta, HIST_A), min(n_meta, HIST_B)
        meta_end = n_s + n_meta
        if ma < HIST_A:
            ha_ref[0:HIST_A - ma, :] = jnp.zeros((HIST_A - ma, tn), F32)
        if mb < HIST_B:
            hb_ref[0:HIST_B - mb, :] = jnp.zeros((HIST_B - mb, tn), F32)
        ha_ref[HIST_A - ma:HIST_A, :] = sa_ref[HIST_A + meta_end - ma:HIST_A + meta_end, :]
        hb_ref[HIST_B - mb:HIST_B, :] = sb_ref[HIST_B + meta_end - mb:HIST_B + meta_end, :]

    sa_ref[0:HIST_A, :] = sa_ref[tm:tm + HIST_A, :]
    sb_ref[0:HIST_B, :] = sb_ref[tm:tm + HIST_B, :]


def _inproj_conv(xp, xs_bf, w, b, wa, ba, wb, n_seq, seq, n_s, n_meta):
    rp, d = xp.shape
    ka, kb = wa.shape[0], wb.shape[0]
    tiles_per_seq = seq // IN_TM
    n_prompt_tiles = rp // IN_TM
    nj = d // IN_TN
    units_per_block = d // W_UNIT_ROWS
    n_units = N_PROJ_BLOCKS * units_per_block

    def wnext_map(j, i):
        u = jnp.where(j == nj - 1, n_units - 1, jnp.minimum(i, n_units - 1))
        col = jnp.minimum(j + 1, nj - 1)
        return (u % units_per_block, (u // units_per_block) * nj + col)

    def prow(j, i):
        return (jnp.maximum(i - 1, 0), j)

    b_specs = [pl.BlockSpec((1, IN_TN), functools.partial(lambda j, i, k: (0, k * nj + j), k=k))
               for k in range(N_PROJ_BLOCKS)]
    ch_spec = lambda rows: pl.BlockSpec((rows, IN_TN), lambda j, i: (0, j))
    prompt_spec = pl.BlockSpec((IN_TM, IN_TN), prow)
    state_spec = lambda rows: pl.BlockSpec((None, rows, IN_TN),
                                           lambda j, i: (jnp.maximum(i - 1, 0) // tiles_per_seq, 0, j))
    small_spec = pl.BlockSpec((IN_TM, IN_TN), lambda j, i: (0, j))
    sds = jax.ShapeDtypeStruct
    return pl.pallas_call(
        functools.partial(_inproj_conv_kernel, ka, kb, tiles_per_seq, n_s, n_meta),
        grid=(nj, n_prompt_tiles + 1),
        in_specs=[pl.BlockSpec((IN_TM, d), lambda j, i: (jnp.maximum(i - 1, 0), 0)),
                  pl.BlockSpec((IN_TM, d), lambda j, i: (0, 0)),
                  pl.BlockSpec((W_UNIT_ROWS, IN_TN), wnext_map),
                  pl.BlockSpec(memory_space=pl.ANY)] + b_specs + [ch_spec(ka), ch_spec(1), ch_spec(kb)],
        out_specs=[prompt_spec, prompt_spec, prompt_spec, prompt_spec, state_spec(ka - 1), state_spec(kb - 1),
                   small_spec, small_spec, small_spec, small_spec, small_spec],
        out_shape=[sds((rp, d), F32), sds((rp, d), BF16), sds((rp, d), BF16), sds((rp, d), BF16),
                   sds((n_seq, ka - 1, d), F32), sds((n_seq, kb - 1, d), F32),
                   sds((IN_TM, d), F32), sds((IN_TM, d), F32), sds((IN_TM, d), F32),
                   sds((IN_TM, d), BF16), sds((IN_TM, d), BF16)],
        scratch_shapes=[pltpu.VMEM((2, N_PROJ_BLOCKS, d, IN_TN), BF16),
                        pltpu.VMEM((2, W_UNIT_ROWS, IN_TN), F32), pltpu.SemaphoreType.DMA((2,)),
                        pltpu.VMEM((IN_TM, d), BF16),
                        pltpu.VMEM((HIST_A + IN_TM, IN_TN), F32), pltpu.VMEM((HIST_B + IN_TM, IN_TN), F32),
                        pltpu.VMEM((IN_TM, IN_TN), F32),
                        pltpu.VMEM((HIST_A, IN_TN), F32), pltpu.VMEM((HIST_B, IN_TN), F32)],
        compiler_params=_cparams(2),
        name="inproj_conv",
    )(xp, xs_bf, w, w, *([b] * N_PROJ_BLOCKS), wa, ba, wb)


def _conv_sample_kernel(ka, kb, sta_ref, stb_ref, ga_ref, cb_ref, bg_ref, wa_ref, ba_ref, wb_ref,
                        ua_ref, pb_ref):
    acc = wa_ref[ka - 1:ka, :] * ga_ref[...]
    for k in range(ka - 1):
        acc = acc + wa_ref[k:k + 1, :] * sta_ref[:, k, :]
    ua_ref[...] = acc + ba_ref[...]
    accb = wb_ref[kb - 1:kb, :] * cb_ref[...]
    for k in range(kb - 1):
        accb = accb + wb_ref[k:k + 1, :] * stb_ref[:, k, :]
    pb_ref[...] = (bg_ref[...] * accb).astype(BF16)


def _conv_sample(state_a, state_b, ga, cb, bg, row_block, wa, ba, wb):
    n, _, d = state_a.shape
    ka, kb = wa.shape[0], wb.shape[0]
    row_spec = pl.BlockSpec((n, CONV_TC), lambda c: (row_block, c))
    out_spec = pl.BlockSpec((n, CONV_TC), lambda c: (0, c))
    return pl.pallas_call(
        functools.partial(_conv_sample_kernel, ka, kb),
        grid=(d // CONV_TC,),
        in_specs=[pl.BlockSpec((n, ka - 1, CONV_TC), lambda c: (0, 0, c)),
                  pl.BlockSpec((n, kb - 1, CONV_TC), lambda c: (0, 0, c)),
                  row_spec, row_spec, row_spec,
                  pl.BlockSpec((ka, CONV_TC), lambda c: (0, c)),
                  pl.BlockSpec((1, CONV_TC), lambda c: (0, c)),
                  pl.BlockSpec((kb, CONV_TC), lambda c: (0, c))],
        out_specs=[out_spec, out_spec],
        out_shape=[jax.ShapeDtypeStruct((n, d), F32), jax.ShapeDtypeStruct((n, d), BF16)],
        compiler_params=_cparams(1),
        name="conv_sample",
    )(state_a, state_b, ga, cb, bg, wa, ba, wb)


ROUTE_ID, ROUTE_RANK, ROUTE_W = 0, TOP_K, 2 * TOP_K


def _route_tile(lg, carry):
    tm = lg.shape[0]
    lane = lax.broadcasted_iota(jnp.int32, (tm, LANES), 1)
    neg_inf = jnp.float32(-jnp.inf)

    def first_max(v):
        m = jnp.max(v, axis=-1, keepdims=True)
        return m, jnp.min(jnp.where(v == m, lane, LANES), axis=-1, keepdims=True)

    g_mask = lane < N_GROUPS
    g_max, g_sel = first_max(jnp.where(g_mask, lg, neg_inf))
    g_w = 1.0 / jnp.sum(jnp.where(g_mask, jnp.exp(lg - g_max), 0.0), axis=-1, keepdims=True)
    lane0 = N_GROUPS + g_sel * EXPERTS_PER_GROUP
    e_lg = jnp.where(jnp.logical_and(lane >= lane0, lane < lane0 + EXPERTS_PER_GROUP), lg, neg_inf)
    m1, l1 = first_max(e_lg)
    m2, l2 = first_max(jnp.where(lane == l1, neg_inf, e_lg))
    r = jnp.exp(m2 - m1)
    c1 = g_w / (1.0 + r)
    c2 = g_w * r / (1.0 + r)

    a1 = lane == l1
    a2 = lane == l2
    hit = jnp.where(jnp.logical_or(a1, a2), 1.0, 0.0)
    row = lax.broadcasted_iota(jnp.int32, (tm, tm), 0)
    col = lax.broadcasted_iota(jnp.int32, (tm, tm), 1)
    before = jnp.where(col < row, 1.0, 0.0).astype(BF16)
    seen = jnp.dot(before, hit.astype(BF16), preferred_element_type=F32) + carry
    rank1 = jnp.sum(jnp.where(a1, seen, 0.0), axis=-1, keepdims=True)
    rank2 = jnp.sum(jnp.where(a2, seen, 0.0), axis=-1, keepdims=True)
    carry = carry + jnp.sum(hit, axis=0, keepdims=True)

    rec = jnp.zeros((tm, LANES), F32)
    fields = [(l1 - N_GROUPS).astype(F32), (l2 - N_GROUPS).astype(F32), rank1, rank2, c1, c2]
    for n, v in enumerate(fields):
        rec = jnp.where(lane == n, v, rec)
    return rec, carry


def _mixer_kernel(alpha, n_tiles, ua_ref, pb_ref, sga_ref, sgb_ref, x_ref, wa_ref, wb_ref, wo_ref,
                  lnag_ref, lnab_ref, ln1g_ref, ln1b_ref, wr_ref, br_ref, cnt0_ref, *refs):
    x1_ref, rt_ref, ri_ref, cnt_ref, carry_ref = refs[-5:]
    i = pl.program_id(0)

    @pl.when(i == 0)
    def _():
        carry_ref[...] = cnt0_ref[...]

    @pl.when(i < n_tiles)
    def _():
        un = _layer_norm(ua_ref[...], lnag_ref[...], lnab_ref[...])
        act = (un * _sigmoid(un)).astype(BF16)
        ya = jnp.dot(act, wa_ref[...], preferred_element_type=F32)
        yb = jnp.dot(pb_ref[...], wb_ref[...], preferred_element_type=F32)
        m = (sga_ref[...].astype(F32) * ya + sgb_ref[...].astype(F32) * yb).astype(BF16)
        mixed = jnp.dot(m, wo_ref[...], preferred_element_type=F32)
        x1 = _layer_norm(alpha * x_ref[...] + mixed, ln1g_ref[...], ln1b_ref[...])
        x1_ref[...] = x1
        hi = x1.astype(BF16)
        lo = (x1 - hi.astype(F32)).astype(BF16)
        a = jnp.dot(hi, wr_ref[...], preferred_element_type=F32)
        b = jnp.dot(lo, wr_ref[...], preferred_element_type=F32)
        lg = a[:, :LANES] + a[:, LANES:] + b[:, :LANES] + br_ref[...]
        rec, carry = _route_tile(lg, carry_ref[...])
        carry_ref[...] = carry
        rt_ref[...] = rec
        ri_ref[...] = rec.T[0:2 * TOP_K, :].astype(jnp.int32)

    if len(refs) == 8:
        x1_tail_ref, rt_tail_ref, ri_tail_ref = refs[:3]
        n_tail = x1_tail_ref.shape[0]

        @pl.when(i == n_tiles)
        def _():
            x1_ref[0:n_tail, :] = x1_tail_ref[...]
            rt_ref[0:n_tail, :] = rt_tail_ref[...]
            ri_ref[:, 0:n_tail] = ri_tail_ref[...]

    cnt_ref[...] = carry_ref[...]


def _mixer(tm, n_tiles, gate_off, alpha, ua, pb, sga, sgb, x, wa, wb, wo,
           lnag, lnab, ln1g, ln1b, wr, br, cnt0, tail=None):
    d = x.shape[1]
    last = n_tiles - 1
    in_spec = pl.BlockSpec((tm, d), lambda i: (jnp.minimum(i, last), 0))
    gate_spec = pl.BlockSpec((tm, d), lambda i: (jnp.minimum(i, last) + gate_off, 0))
    vec_spec = pl.BlockSpec((1, d), lambda i: (0, 0))
    lane_spec = pl.BlockSpec((1, LANES), lambda i: (0, 0))
    w_spec = pl.BlockSpec((d, d), lambda i: (0, 0), pipeline_mode=pl.Buffered(1))
    in_specs = [in_spec, in_spec, gate_spec, gate_spec, in_spec, w_spec, w_spec, w_spec,
                vec_spec, vec_spec, vec_spec, vec_spec,
                pl.BlockSpec((d, 2 * LANES), lambda i: (0, 0)), lane_spec, lane_spec]
    args = [ua, pb, sga, sgb, x, wa, wb, wo, lnag, lnab, ln1g, ln1b, wr, br, cnt0]
    n_rows, n_steps = n_tiles * tm, n_tiles
    if tail is not None:
        n_tail = tail[0].shape[0]
        assert n_tail <= tm
        in_specs += [pl.BlockSpec((n_tail, d), lambda i: (0, 0)), pl.BlockSpec((n_tail, LANES), lambda i: (0, 0)),
                     pl.BlockSpec((2 * TOP_K, n_tail), lambda i: (0, 0))]
        args += list(tail)
        n_rows, n_steps = n_rows + n_tail, n_steps + 1
    return pl.pallas_call(
        functools.partial(_mixer_kernel, alpha, n_tiles),
        grid=(n_steps,),
        in_specs=in_specs,
        out_specs=[pl.BlockSpec((tm, d), lambda i: (i, 0)), pl.BlockSpec((tm, LANES), lambda i: (i, 0)),
                   pl.BlockSpec((2 * TOP_K, tm), lambda i: (0, i)), lane_spec],
        out_shape=[jax.ShapeDtypeStruct((n_rows, d), F32), jax.ShapeDtypeStruct((n_rows, LANES), F32),
                   jax.ShapeDtypeStruct((2 * TOP_K, n_rows), jnp.int32), jax.ShapeDtypeStruct((1, LANES), F32)],
        scratch_shapes=[pltpu.VMEM((1, LANES), F32)],
        compiler_params=_cparams(1),
        name="mixer",
    )(*args)


def _plan_kernel(n_tok, n_blocks, *refs):
    id_refs = refs[0:TOP_K]
    rank_refs = refs[TOP_K:2 * TOP_K]
    cnt_ref, be_ref, tok_ref, dst_ref, nu_ref, start_ref = refs[2 * TOP_K:]
    shift = MOE_BM.bit_length() - 1

    def per_expert(e, blk0):
        cnt = cnt_ref[0, N_GROUPS + e]
        nb = lax.shift_right_logical(cnt + (MOE_BM - 1), shift)
        start_ref[e] = blk0 * MOE_BM

        def fill(j, carry):
            be_ref[blk0 + j] = e
            return carry

        lax.fori_loop(0, nb, fill, 0)

        def pad(s, carry):
            tok_ref[s] = 0
            return carry

        lax.fori_loop(blk0 * MOE_BM + cnt, (blk0 + nb) * MOE_BM, pad, 0)
        return blk0 + nb

    n_used = lax.fori_loop(0, N_EXPERTS, per_expert, 0)
    nu_ref[0] = n_used

    def rest(b, carry):
        be_ref[b] = N_EXPERTS - 1

        def pad(s, c):
            tok_ref[b * MOE_BM + s] = 0
            return c

        lax.fori_loop(0, MOE_BM, pad, 0, unroll=8)
        return carry

    lax.fori_loop(n_used, n_blocks, rest, 0)

    def place(t, carry):
        for k in range(TOP_K):
            slot = start_ref[id_refs[k][t]] + rank_refs[k][t]
            dst_ref[k * n_tok + t] = slot
            tok_ref[slot] = t
        return carry

    lax.fori_loop(0, n_tok, place, 0, unroll=8)


def _plan(ids, ranks, cnt):
    n_tok = ids[0].shape[0]
    n_blocks = -(-n_tok * TOP_K // MOE_BM) + N_EXPERTS
    smem = pl.BlockSpec(memory_space=pltpu.SMEM)
    return pl.pallas_call(
        functools.partial(_plan_kernel, n_tok, n_blocks),
        in_specs=[smem] * (2 * TOP_K + 1),
        out_specs=[smem, smem, smem, smem],
        out_shape=[jax.ShapeDtypeStruct((n_blocks,), jnp.int32), jax.ShapeDtypeStruct((n_blocks * MOE_BM,), jnp.int32),
                   jax.ShapeDtypeStruct((TOP_K * n_tok,), jnp.int32), jax.ShapeDtypeStruct((1,), jnp.int32)],
        scratch_shapes=[pltpu.SMEM((N_EXPERTS,), jnp.int32)],
        name="plan",
    )(*ids, *ranks, cnt)


def _moe_kernel(be_ref, tok_ref, nused_ref, cnt_ref, x_hbm, wg_hbm, wu_hbm, wd_hbm, ys_ref,
                xbuf_ref, gsem_ref, wgf_ref, wuf_ref, wdf_ref, wsem_ref, wgu_ref, wdb_ref, ord_ref):
    b = pl.program_id(0)
    n_used = nused_ref[0]
    de2 = wgu_ref.shape[1]
    de = de2 // 2
    shift = MOE_BM.bit_length() - 1

    def weight_copies(e, slot):
        return [pltpu.make_async_copy(wg_hbm.at[e], wgf_ref.at[slot], wsem_ref.at[slot]),
                pltpu.make_async_copy(wu_hbm.at[e], wuf_ref.at[slot], wsem_ref.at[slot]),
                pltpu.make_async_copy(wd_hbm.at[e], wdf_ref.at[slot], wsem_ref.at[slot])]

    def row_copy(blk, slot, r):
        return pltpu.make_async_copy(x_hbm.at[pl.ds(tok_ref[blk * MOE_BM + r], 1)],
                                     xbuf_ref.at[slot, pl.ds(r, 1)], gsem_ref.at[slot])

    @pl.when(jnp.logical_and(b == 0, n_used > 0))
    def _():
        ord_ref[0] = 0
        for cp in weight_copies(be_ref[0], 0):
            cp.start(priority=WEIGHT_DMA_PRIORITY)
        for ahead in range(MOE_AHEAD):
            for r in range(MOE_BM):
                row_copy(jnp.minimum(ahead, n_used - 1), ahead, r).start()

    @pl.when(b < n_used)
    def _():
        e = be_ref[b]

        @pl.when(jnp.logical_or(b == 0, e != be_ref[jnp.maximum(b - 1, 0)]))
        def _():
            order = ord_ref[0]
            wslot = order % 2
            for cp in weight_copies(e, wslot):
                cp.wait()
            nxt = b + lax.shift_right_logical(cnt_ref[0, N_GROUPS + e] + (MOE_BM - 1), shift)

            @pl.when(nxt < n_used)
            def _():
                for cp in weight_copies(be_ref[jnp.minimum(nxt, be_ref.shape[0] - 1)], 1 - wslot):
                    cp.start(priority=WEIGHT_DMA_PRIORITY)

            wgu_ref[:, 0:de] = wgf_ref[wslot].astype(BF16)
            wgu_ref[:, de:de2] = wuf_ref[wslot].astype(BF16)
            wdb_ref[...] = wdf_ref[wslot].astype(BF16)
            ord_ref[0] = order + 1

        n_buf = MOE_AHEAD + 1
        slot = b % n_buf
        for r in range(MOE_BM):
            row_copy(b, slot, r).wait()
        nxt_blk = jnp.minimum(b + MOE_AHEAD, n_used - 1)
        nxt_slot = (b + MOE_AHEAD) % n_buf
        xb = xbuf_ref[slot].astype(BF16)
        n_chunks = de2 // MOE_NC
        per = MOE_BM // n_chunks
        gu = []
        for c in range(n_chunks):
            for r in range(c * per, (c + 1) * per):
                row_copy(nxt_blk, nxt_slot, r).start()
            gu.append(jnp.dot(xb, wgu_ref[:, c * MOE_NC:(c + 1) * MOE_NC], preferred_element_type=F32))
        half = n_chunks // 2
        y = None
        for c in range(half):
            g = gu[c]
            h = (g * _sigmoid(g) * gu[half + c]).astype(BF16)
            part = jnp.dot(h, wdb_ref[c * MOE_NC:(c + 1) * MOE_NC, :], preferred_element_type=F32)
            y = part if y is None else y + part
        ys_ref[...] = y

    @pl.when(b == n_used - 1)
    def _():
        for ahead in range(1, MOE_AHEAD + 1):
            for r in range(MOE_BM):
                row_copy(b, (b + ahead) % (MOE_AHEAD + 1), r).wait()

    @pl.when(b >= n_used)
    def _():
        ys_ref[...] = jnp.zeros_like(ys_ref)


def _moe(block_expert, slot_tok, n_used, cnt, x1, wg, wu, wd):
    n_blocks = block_expert.shape[0]
    n_slots = slot_tok.shape[0]
    _, d, de = wg.shape
    any_spec = pl.BlockSpec(memory_space=pl.ANY)
    grid_spec = pltpu.PrefetchScalarGridSpec(
        num_scalar_prefetch=4,
        grid=(n_blocks,),
        in_specs=[any_spec, any_spec, any_spec, any_spec],
        out_specs=pl.BlockSpec((MOE_BM, d), lambda b, *_: (b, 0)),
        scratch_shapes=[pltpu.VMEM((MOE_AHEAD + 1, MOE_BM, d), F32), pltpu.SemaphoreType.DMA((MOE_AHEAD + 1,)),
                        pltpu.VMEM((2, d, de), F32), pltpu.VMEM((2, d, de), F32), pltpu.VMEM((2, de, d), F32),
                        pltpu.SemaphoreType.DMA((2,)),
                        pltpu.VMEM((d, 2 * de), BF16), pltpu.VMEM((de, d), BF16), pltpu.SMEM((1,), jnp.int32)],
    )
    return pl.pallas_call(
        _moe_kernel,
        grid_spec=grid_spec,
        out_shape=jax.ShapeDtypeStruct((n_slots, d), F32),
        compiler_params=_cparams(1),
        name="moe_ffn",
    )(block_expert, slot_tok, n_used, cnt, x1, wg, wu, wd)


def _combine_kernel(alpha, tile_off, n_tok, dst_ref, ys_hbm, x1_ref, rt_ref, g_ref, b_ref, out_ref,
                    buf_ref, sem_ref):
    i = pl.program_id(0)
    n = pl.num_programs(0)
    tm = x1_ref.shape[0]

    def row_copy(tile, slot, r, k):
        src = dst_ref[k * n_tok + (tile + tile_off) * tm + r]
        return pltpu.make_async_copy(ys_hbm.at[pl.ds(src, 1)], buf_ref.at[slot, k, pl.ds(r, 1)], sem_ref.at[slot])

    @pl.when(i == 0)
    def _():
        for r in range(tm):
            for k in range(TOP_K):
                row_copy(0, 0, r, k).start()

    slot = i % 2
    for r in range(tm):
        for k in range(TOP_K):
            row_copy(i, slot, r, k).wait()
    nxt = jnp.minimum(i + 1, n - 1)
    for r0 in range(0, tm, CMB_RC):
        for r in range(r0, r0 + CMB_RC):
            for k in range(TOP_K):
                row_copy(nxt, 1 - slot, r, k).start()
        rt = rt_ref[r0:r0 + CMB_RC, :]
        f = (rt[:, ROUTE_W:ROUTE_W + 1] * buf_ref[slot, 0, r0:r0 + CMB_RC, :]
             + rt[:, ROUTE_W + 1:ROUTE_W + 2] * buf_ref[slot, 1, r0:r0 + CMB_RC, :])
        out_ref[r0:r0 + CMB_RC, :] = _layer_norm(alpha * x1_ref[r0:r0 + CMB_RC, :] + f, g_ref[...], b_ref[...])

    @pl.when(i == n - 1)
    def _():
        for r in range(tm):
            for k in range(TOP_K):
                row_copy(i, 1 - slot, r, k).wait()


def _combine(tm, n_tiles, tile_off, alpha, dst, ys, x1, rt, g, b):
    n_tok, d = x1.shape
    grid_spec = pltpu.PrefetchScalarGridSpec(
        num_scalar_prefetch=1,
        grid=(n_tiles,),
        in_specs=[pl.BlockSpec(memory_space=pl.ANY),
                  pl.BlockSpec((tm, d), lambda i, *_: (i + tile_off, 0)),
                  pl.BlockSpec((tm, LANES), lambda i, *_: (i + tile_off, 0)),
                  pl.BlockSpec((1, d), lambda i, *_: (0, 0)),
                  pl.BlockSpec((1, d), lambda i, *_: (0, 0))],
        out_specs=pl.BlockSpec((tm, d), lambda i, *_: (i, 0)),
        scratch_shapes=[pltpu.VMEM((2, TOP_K, tm, d), F32), pltpu.SemaphoreType.DMA((2,))],
    )
    return pl.pallas_call(
        functools.partial(_combine_kernel, alpha, tile_off, n_tok),
        grid_spec=grid_spec,
        out_shape=jax.ShapeDtypeStruct((n_tiles * tm, d), F32),
        compiler_params=_cparams(1),
        name="combine",
    )(dst, ys, x1, rt, g, b)


def kernel(x_prompt, x_sample, state_conv_a, state_conv_b, meta_tokens, w_in, b_in, conv_a_w, conv_a_b, ln_a_g, ln_a_b, w_a_out, conv_b_w, w_b_out, w_o, ln1_g, ln1_b, w_router_group, b_router_group, w_router_expert, b_router_expert, w_exp_gate, w_exp_up, w_exp_down, ln2_g, ln2_b):
    depth = w_in.shape[0]
    assert depth == 1, "single-layer step only"
    n_seq, seq, d = x_prompt.shape
    n_s = x_sample.shape[0]
    n_meta = meta_tokens.shape[0]
    ka, kb = conv_a_w.shape[1], conv_b_w.shape[1]
    assert x_sample.shape[1] == 1 and seq % IN_TM == 0 and IN_TM >= ka - 1
    assert n_meta <= HIST_A and ka - 1 <= HIST_A and kb - 1 <= HIST_B and kb - 1 <= n_meta
    assert n_s + n_meta <= IN_TM and n_s % CMB_TM == 0
    alpha = (2.0 * depth) ** 0.25
    rp = n_seq * seq

    xp = x_prompt.reshape(rp, d)
    xs = jnp.concatenate([x_sample.reshape(n_s, d), meta_tokens,
                          jnp.zeros((IN_TM - n_s - n_meta, d), F32)], axis=0).astype(BF16)
    (ua_p, pb_p, sga_p, sgb_p, new_a_p, new_b_p, ga_x, cb_x, bg_x, sga_x, sgb_x) = _inproj_conv(
        xp, xs, w_in[0], b_in, conv_a_w[0], conv_a_b, conv_b_w[0], n_seq, seq, n_s, n_meta)
    ua_s, pb_s = _conv_sample(state_conv_a[0], state_conv_b[0], ga_x, cb_x, bg_x, 0,
                              conv_a_w[0], conv_a_b, conv_b_w[0])

    t = rp + n_s
    wr_f = jnp.concatenate([w_router_group[0], w_router_expert[0].transpose(1, 0, 2).reshape(d, N_EXPERTS),
                            jnp.zeros((d, LANES - N_GROUPS - N_EXPERTS), F32)], axis=1)
    wr_hi = wr_f.astype(BF16)
    wr_lo = (wr_f - wr_hi.astype(F32)).astype(BF16)
    wr = jnp.concatenate([wr_hi, wr_lo], axis=1)
    br = jnp.concatenate([b_router_group[0], b_router_expert[0].reshape(-1),
                          jnp.zeros((LANES - N_GROUPS - N_EXPERTS,), F32)])[None, :]
    wa_bf, wb_bf, wo_bf = w_a_out[0].astype(BF16), w_b_out[0].astype(BF16), w_o[0].astype(BF16)
    mix = functools.partial(_mixer, alpha=alpha, wa=wa_bf, wb=wb_bf, wo=wo_bf, lnag=ln_a_g, lnab=ln_a_b,
                            ln1g=ln1_g, ln1b=ln1_b, wr=wr, br=br)
    cnt0 = jnp.zeros((1, LANES), F32)
    x1_s, rt_s, ri_s, cnt_s = mix(n_s, 1, 0, ua=ua_s, pb=pb_s, sga=sga_x, sgb=sgb_x,
                                  x=x_sample.reshape(n_s, d), cnt0=cnt0)
    x1, rt, ri, cnt = mix(MIX_TM, rp // MIX_TM, 0, ua=ua_p, pb=pb_p, sga=sga_p, sgb=sgb_p, x=xp, cnt0=cnt_s,
                          tail=(x1_s, rt_s, ri_s))

    cnt_i = cnt.astype(jnp.int32)
    ids = [ri[ROUTE_ID + k] for k in range(TOP_K)]
    ranks = [ri[ROUTE_RANK + k] for k in range(TOP_K)]
    block_expert, slot_tok, dest, n_used = _plan(ids, ranks, cnt_i)
    ys = _moe(block_expert, slot_tok, n_used, cnt_i, x1, w_exp_gate[0], w_exp_up[0], w_exp_down[0])
    y_p = _combine(CMB_TM, rp // CMB_TM, 0, alpha, dest, ys, x1, rt, ln2_g, ln2_b)
    y_s = _combine(CMB_TM, n_s // CMB_TM, rp // CMB_TM, alpha, dest, ys, x1, rt, ln2_g, ln2_b)

    new_a_s = jnp.concatenate([state_conv_a[0][:, 1:], ga_x[:n_s][:, None]], axis=1)
    new_b_s = jnp.concatenate([state_conv_b[0][:, 1:], cb_x[:n_s][:, None]], axis=1)
    return (y_p.reshape(n_seq, seq, d), y_s.reshape(n_s, 1, d), new_a_p[None], new_b_p[None], new_a_s[None], new_b_s[None])
```

```python
import functools

import jax
import jax.numpy as jnp
from jax import lax
from jax.experimental import pallas as pl
from jax.experimental.pallas import tpu as pltpu

F32 = jnp.float32
BF16 = jnp.bfloat16

LN_EPS = 1e-5
N_GROUPS = 4
EXPERTS_PER_GROUP = 8
N_EXPERTS = N_GROUPS * EXPERTS_PER_GROUP
TOP_K = 2
N_PROJ_BLOCKS = 7

VMEM_LIMIT_BYTES = 56 * 1024 * 1024
LANES = 128

IN_TM = 512
IN_TN = 256
IN_HM = 256
W_UNIT_ROWS = 1024
CONV_TC = 256
CONV_RC = 64
HIST_A = 32
HIST_B = 8
MIX_TM = 256
MOE_BM = 128
MOE_NC = 256
MOE_AHEAD = 4
WEIGHT_DMA_PRIORITY = 1
CMB_TM = 128
CMB_RC = 32
CMB_AHEAD = 3


def _cparams(n_axes):
    return pltpu.CompilerParams(dimension_semantics=("arbitrary",) * n_axes,
                                vmem_limit_bytes=VMEM_LIMIT_BYTES)


def _sigmoid(x):
    return 1.0 / (1.0 + jnp.exp(-x))


def _layer_norm(x, g, b):
    mu = jnp.mean(x, axis=-1, keepdims=True)
    xc = x - mu
    var = jnp.mean(xc * xc, axis=-1, keepdims=True)
    return xc * lax.rsqrt(var + LN_EPS) * g + b


def _conv_taps(src_ref, w_ref, n_taps, hist, r0, rc, c0):
    base = hist - (n_taps - 1)
    acc = None
    for rho in range(8):
        offs = [o for o in range(base, base + n_taps) if o % 8 == rho]
        if not offs:
            continue
        lo = offs[0]
        x = src_ref[pl.ds(r0 + lo, rc + offs[-1] - lo), c0:c0 + LANES]
        for o in offs:
            term = w_ref[o - base:o - base + 1, c0:c0 + LANES] * x[o - lo:o - lo + rc]
            acc = term if acc is None else acc + term
    return acc


def _inproj_conv_kernel(ka, kb, tiles_per_seq, n_s, n_meta, xp_ref, xs_ref, wnext_ref, w_hbm, *refs):
    b_refs = refs[0:7]
    wa_ref, ba_ref, wb_ref = refs[7:10]
    ua_ref, pb_ref, sga_ref, sgb_ref, na_ref, nb_ref = refs[10:16]
    gas_ref, cbs_ref, bgs_ref, sgas_ref, sgbs_ref = refs[16:21]
    wbf_ref, stage_ref, sem_ref, xbf_ref, sa_ref, sb_ref, bg_ref, ha_ref, hb_ref = refs[21:30]
    j = pl.program_id(0)
    i = pl.program_id(1)
    nj = pl.num_programs(0)
    tm, d = xbf_ref.shape
    tn = sa_ref.shape[1]
    unit_rows = stage_ref.shape[1]
    units_per_block = d // unit_rows
    n_units = N_PROJ_BLOCKS * units_per_block

    @pl.when(jnp.logical_and(j == 0, i == 0))
    def _():
        def unit_copy(u):
            k, h = divmod(u, units_per_block)
            return pltpu.make_async_copy(
                w_hbm.at[pl.ds(h * unit_rows, unit_rows), pl.ds(k * d, tn)], stage_ref.at[u % 2], sem_ref.at[u % 2])

        unit_copy(0).start()
        for u in range(n_units):
            if u + 1 < n_units:
                unit_copy(u + 1).start()
            unit_copy(u).wait()
            k, h = divmod(u, units_per_block)
            wbf_ref[0, k, h * unit_rows:(h + 1) * unit_rows, :] = stage_ref[u % 2].astype(BF16)

    @pl.when(jnp.logical_and(i < n_units, j + 1 < nj))
    def _():
        k = i // units_per_block
        h = i % units_per_block
        row0 = pl.multiple_of(h * unit_rows, unit_rows)
        wbf_ref[(j + 1) % 2, k, pl.ds(row0, unit_rows), :] = wnext_ref[...].astype(BF16)

    @pl.when(i == 0)
    def _():
        xbf_ref[...] = xs_ref[...]

    @pl.when(i > 0)
    def _():
        xbf_ref[...] = xp_ref[...].astype(BF16)

    seq_pos = (i - 1) % tiles_per_seq

    @pl.when(jnp.logical_and(i > 0, seq_pos == 0))
    def _():
        sa_ref[0:HIST_A, :] = ha_ref[...]
        sb_ref[0:HIST_B, :] = hb_ref[...]

    slot = j % 2
    halves = list(range(0, tm, IN_HM))

    def proj(k, m0):
        return (jnp.dot(xbf_ref[m0:m0 + IN_HM, :], wbf_ref[slot, k], preferred_element_type=F32)
                + b_refs[k][...])

    chunks = [(c0, r0) for c0 in range(0, tn, LANES) for r0 in range(0, tm, CONV_RC)]

    def conv_a(c0, r0):
        acc = _conv_taps(sa_ref, wa_ref, ka, HIST_A, r0, CONV_RC, c0)
        ua_ref[r0:r0 + CONV_RC, c0:c0 + LANES] = acc + ba_ref[:, c0:c0 + LANES]

    def conv_b(c0, r0):
        accb = _conv_taps(sb_ref, wb_ref, kb, HIST_B, r0, CONV_RC, c0)
        pb_ref[r0:r0 + CONV_RC, c0:c0 + LANES] = (bg_ref[r0:r0 + CONV_RC, c0:c0 + LANES] * accb).astype(BF16)

    todo_a = [functools.partial(conv_a, c0, r0) for c0, r0 in chunks]
    todo_b = [functools.partial(conv_b, c0, r0) for c0, r0 in chunks]

    def run(todo, n):
        for _ in range(min(n, len(todo))):
            todo.pop(0)()

    for m0 in halves:
        sa_ref[HIST_A + m0:HIST_A + m0 + IN_HM, :] = proj(0, m0) * _sigmoid(proj(1, m0))
    for m0 in halves:
        p3 = proj(3, m0)
        run(todo_a, 2)
        p4 = proj(4, m0)
        run(todo_a, 2)
        sb_ref[HIST_B + m0:HIST_B + m0 + IN_HM, :] = p3 * p4
    for m0 in halves:
        bg_ref[m0:m0 + IN_HM, :] = proj(2, m0)
        run(todo_a, 2)
    for m0 in halves:
        sga_ref[m0:m0 + IN_HM, :] = _sigmoid(proj(5, m0)).astype(BF16)
        run(todo_a, 1)
        run(todo_b, 4)
    for m0 in halves:
        sgb_ref[m0:m0 + IN_HM, :] = _sigmoid(proj(6, m0)).astype(BF16)
        run(todo_a, 1)
        run(todo_b, 4)
    run(todo_a, len(todo_a))
    run(todo_b, len(todo_b))

    @pl.when(jnp.logical_and(i > 0, seq_pos == tiles_per_seq - 1))
    def _():
        na_ref[...] = sa_ref[HIST_A + tm - (ka - 1):HIST_A + tm, :]
        nb_ref[...] = sb_ref[HIST_B + tm - (kb - 1):HIST_B + tm, :]

    @pl.when(i == 0)
    def _():
        gas_ref[...] = sa_ref[HIST_A:HIST_A + tm, :]
        cbs_ref[...] = sb_ref[HIST_B:HIST_B + tm, :]
        bgs_ref[...] = bg_ref[...]
        sgas_ref[...] = sga_ref[...]
        sgbs_ref[...] = sgb_ref[...]
        ma, mb = min(n_meta, HIST_A), min(n_meta, HIST_B)
        meta_end = n_s + n_meta
        if ma < HIST_A:
            ha_ref[0:HIST_A - ma, :] = jnp.zeros((HIST_A - ma, tn), F32)
        if mb < HIST_B:
            hb_ref[0:HIST_B - mb, :] = jnp.zeros((HIST_B - mb, tn), F32)
        ha_ref[HIST_A - ma:HIST_A, :] = sa_ref[HIST_A + meta_end - ma:HIST_A + meta_end, :]
        hb_ref[HIST_B - mb:HIST_B, :] = sb_ref[HIST_B + meta_end - mb:HIST_B + meta_end, :]

    sa_ref[0:HIST_A, :] = sa_ref[tm:tm + HIST_A, :]
    sb_ref[0:HIST_B, :] = sb_ref[tm:tm + HIST_B, :]


def _inproj_conv(xp, xs_bf, w, b, wa, ba, wb, n_seq, seq, n_s, n_meta):
    rp, d = xp.shape
    ka, kb = wa.shape[0], wb.shape[0]
    tiles_per_seq = seq // IN_TM
    n_prompt_tiles = rp // IN_TM
    nj = d // IN_TN
    units_per_block = d // W_UNIT_ROWS
    n_units = N_PROJ_BLOCKS * units_per_block

    def wnext_map(j, i):
        u = jnp.where(j == nj - 1, n_units - 1, jnp.minimum(i, n_units - 1))
        col = jnp.minimum(j + 1, nj - 1)
        return (u % units_per_block, (u // units_per_block) * nj + col)

    def prow(j, i):
        return (jnp.maximum(i - 1, 0), j)

    b_specs = [pl.BlockSpec((1, IN_TN), functools.partial(lambda j, i, k: (0, k * nj + j), k=k))
               for k in range(N_PROJ_BLOCKS)]
    ch_spec = lambda rows: pl.BlockSpec((rows, IN_TN), lambda j, i: (0, j))
    prompt_spec = pl.BlockSpec((IN_TM, IN_TN), prow)
    state_spec = lambda rows: pl.BlockSpec((None, rows, IN_TN),
                                           lambda j, i: (jnp.maximum(i - 1, 0) // tiles_per_seq, 0, j))
    small_spec = pl.BlockSpec((IN_TM, IN_TN), lambda j, i: (0, j))
    sds = jax.ShapeDtypeStruct
    return pl.pallas_call(
        functools.partial(_inproj_conv_kernel, ka, kb, tiles_per_seq, n_s, n_meta),
        grid=(nj, n_prompt_tiles + 1),
        in_specs=[pl.BlockSpec((IN_TM, d), lambda j, i: (jnp.maximum(i - 1, 0), 0)),
                  pl.BlockSpec((IN_TM, d), lambda j, i: (0, 0)),
                  pl.BlockSpec((W_UNIT_ROWS, IN_TN), wnext_map),
                  pl.BlockSpec(memory_space=pl.ANY)] + b_specs + [ch_spec(ka), ch_spec(1), ch_spec(kb)],
        out_specs=[prompt_spec, prompt_spec, prompt_spec, prompt_spec, state_spec(ka - 1), state_spec(kb - 1),
                   small_spec, small_spec, small_spec, small_spec, small_spec],
        out_shape=[sds((rp, d), F32), sds((rp, d), BF16), sds((rp, d), BF16), sds((rp, d), BF16),
                   sds((n_seq, ka - 1, d), F32), sds((n_seq, kb - 1, d), F32),
                   sds((IN_TM, d), F32), sds((IN_TM, d), F32), sds((IN_TM, d), F32),
                   sds((IN_TM, d), BF16), sds((IN_TM, d), BF16)],
        scratch_shapes=[pltpu.VMEM((2, N_PROJ_BLOCKS, d, IN_TN), BF16),
                        pltpu.VMEM((2, W_UNIT_ROWS, IN_TN), F32), pltpu.SemaphoreType.DMA((2,)),
                        pltpu.VMEM((IN_TM, d), BF16),
                        pltpu.VMEM((HIST_A + IN_TM, IN_TN), F32), pltpu.VMEM((HIST_B + IN_TM, IN_TN), F32),
                        pltpu.VMEM((IN_TM, IN_TN), F32),
                        pltpu.VMEM((HIST_A, IN_TN), F32), pltpu.VMEM((HIST_B, IN_TN), F32)],
        compiler_params=_cparams(2),
        name="inproj_conv",
    )(xp, xs_bf, w, w, *([b] * N_PROJ_BLOCKS), wa, ba, wb)


def _inproj_pipe_kernel(ka, kb, tiles_per_seq, n_tiles, n_s, n_meta, xp_ref, xs_ref, wnext_ref, w_hbm, *refs):
    b_refs = refs[0:7]
    wa_ref, ba_ref, wb_ref = refs[7:10]
    ua_ref, pb_ref, sga_ref, sgb_ref, na_ref, nb_ref = refs[10:16]
    gas_ref, cbs_ref, bgs_ref, sgas_ref, sgbs_ref = refs[16:21]
    wbf_ref, stage_ref, sem_ref, xbf_ref = refs[21:25]
    bufs = (refs[25:28], refs[28:31])
    ha_ref, hb_ref = refs[31:33]
    j = pl.program_id(0)
    i = pl.program_id(1)
    nj = pl.num_programs(0)
    tm, d = xbf_ref.shape
    tn = ha_ref.shape[1]
    unit_rows = stage_ref.shape[1]
    units_per_block = d // unit_rows
    n_units = N_PROJ_BLOCKS * units_per_block
    t = i - 1

    @pl.when(jnp.logical_and(j == 0, i == 0))
    def _():
        def unit_copy(u):
            k, h = divmod(u, units_per_block)
            return pltpu.make_async_copy(
                w_hbm.at[pl.ds(h * unit_rows, unit_rows), pl.ds(k * d, tn)], stage_ref.at[u % 2], sem_ref.at[u % 2])

        unit_copy(0).start()
        for u in range(n_units):
            if u + 1 < n_units:
                unit_copy(u + 1).start()
            unit_copy(u).wait()
            k, h = divmod(u, units_per_block)
            wbf_ref[0, k, h * unit_rows:(h + 1) * unit_rows, :] = stage_ref[u % 2].astype(BF16)
        for buf in bufs:
            for ref in buf:
                ref[...] = jnp.zeros_like(ref)

    @pl.when(jnp.logical_and(i < n_units, j + 1 < nj))
    def _():
        k = i // units_per_block
        h = i % units_per_block
        row0 = pl.multiple_of(h * unit_rows, unit_rows)
        wbf_ref[(j + 1) % 2, k, pl.ds(row0, unit_rows), :] = wnext_ref[...].astype(BF16)

    @pl.when(i == 0)
    def _():
        xbf_ref[...] = xs_ref[...]

    @pl.when(jnp.logical_and(i > 0, i < n_tiles))
    def _():
        xbf_ref[...] = xp_ref[...].astype(BF16)

    @pl.when(i == 1)
    def _():
        sa_ref, sb_ref, bg_ref = bufs[0]
        gas_ref[...] = sa_ref[HIST_A:HIST_A + tm, :]
        cbs_ref[...] = sb_ref[HIST_B:HIST_B + tm, :]
        bgs_ref[...] = bg_ref[...]
        sgas_ref[...] = sga_ref[...]
        sgbs_ref[...] = sgb_ref[...]
        ma, mb = min(n_meta, HIST_A), min(n_meta, HIST_B)
        meta_end = n_s + n_meta
        if ma < HIST_A:
            ha_ref[0:HIST_A - ma, :] = jnp.zeros((HIST_A - ma, tn), F32)
        if mb < HIST_B:
            hb_ref[0:HIST_B - mb, :] = jnp.zeros((HIST_B - mb, tn), F32)
        ha_ref[HIST_A - ma:HIST_A, :] = sa_ref[HIST_A + meta_end - ma:HIST_A + meta_end, :]
        hb_ref[HIST_B - mb:HIST_B, :] = sb_ref[HIST_B + meta_end - mb:HIST_B + meta_end, :]

    seq_pos = (t - 1) % tiles_per_seq
    for par in range(2):
        sa_ref, sb_ref, _ = bufs[par]
        sa_prev, sb_prev, _ = bufs[1 - par]
        is_t = jnp.logical_and(t >= 1, t % 2 == par)

        @pl.when(jnp.logical_and(is_t, seq_pos == 0))
        def _():
            sa_ref[0:HIST_A, :] = ha_ref[...]
            sb_ref[0:HIST_B, :] = hb_ref[...]

        @pl.when(jnp.logical_and(is_t, seq_pos != 0))
        def _():
            sa_ref[0:HIST_A, :] = sa_prev[tm:tm + HIST_A, :]
            sb_ref[0:HIST_B, :] = sb_prev[tm:tm + HIST_B, :]

    slot = j % 2
    chunks = [(c0, r0) for c0 in range(0, tn, LANES) for r0 in range(0, tm, CONV_RC)]

    def conv_items(buf):
        sa_ref, sb_ref, bg_ref = buf

        def conv_a(c0, r0):
            acc = _conv_taps(sa_ref, wa_ref, ka, HIST_A, r0, CONV_RC, c0)
            ua_ref[r0:r0 + CONV_RC, c0:c0 + LANES] = acc + ba_ref[:, c0:c0 + LANES]

        def conv_b(c0, r0):
            accb = _conv_taps(sb_ref, wb_ref, kb, HIST_B, r0, CONV_RC, c0)
            pb_ref[r0:r0 + CONV_RC, c0:c0 + LANES] = (bg_ref[r0:r0 + CONV_RC, c0:c0 + LANES] * accb).astype(BF16)

        return [[functools.partial(conv_a, c0, r0), functools.partial(conv_b, c0, r0)] for c0, r0 in chunks]

    def run(todo, n):
        for _ in range(min(n, len(todo))):
            for item in todo.pop(0):
                item()

    def project(buf, todo):
        sa_ref, sb_ref, bg_ref = buf

        def proj(k, m0):
            return (jnp.dot(xbf_ref[m0:m0 + IN_HM, :], wbf_ref[slot, k], preferred_element_type=F32)
                    + b_refs[k][...])

        run(todo, 1)
        for m0 in range(0, tm, IN_HM):
            p0 = proj(0, m0)
            run(todo, 1)
            p1 = proj(1, m0)
            run(todo, 1)
            sa_ref[HIST_A + m0:HIST_A + m0 + IN_HM, :] = p0 * _sigmoid(p1)
            p3 = proj(3, m0)
            run(todo, 1)
            p4 = proj(4, m0)
            run(todo, 1)
            sb_ref[HIST_B + m0:HIST_B + m0 + IN_HM, :] = p3 * p4
            bg_ref[m0:m0 + IN_HM, :] = proj(2, m0)
            run(todo, 1)
            sga_ref[m0:m0 + IN_HM, :] = _sigmoid(proj(5, m0)).astype(BF16)
            run(todo, 1)
            sgb_ref[m0:m0 + IN_HM, :] = _sigmoid(proj(6, m0)).astype(BF16)
            run(todo, 1)
        run(todo, len(todo))

    for par in range(2):
        @pl.when(jnp.logical_and(i < n_tiles, i % 2 == par))
        def _():
            project(bufs[par], conv_items(bufs[1 - par]))

    @pl.when(i == n_tiles)
    def _():
        todo = conv_items(bufs[(n_tiles - 1) % 2])
        run(todo, len(todo))

    for par in range(2):
        sa_ref, sb_ref, _ = bufs[par]

        @pl.when(jnp.logical_and(jnp.logical_and(t >= 1, t % 2 == par), seq_pos == tiles_per_seq - 1))
        def _():
            na_ref[...] = sa_ref[HIST_A + tm - (ka - 1):HIST_A + tm, :]
            nb_ref[...] = sb_ref[HIST_B + tm - (kb - 1):HIST_B + tm, :]


def _inproj_pipe(xp, xs_bf, w, b, wa, ba, wb, n_seq, seq, n_s, n_meta):
    rp, d = xp.shape
    ka, kb = wa.shape[0], wb.shape[0]
    tiles_per_seq = seq // IN_TM
    n_prompt_tiles = rp // IN_TM
    n_tiles = n_prompt_tiles + 1
    nj = d // IN_TN
    units_per_block = d // W_UNIT_ROWS
    n_units = N_PROJ_BLOCKS * units_per_block
    last = n_prompt_tiles - 1

    def wnext_map(j, i):
        u = jnp.where(j == nj - 1, n_units - 1, jnp.minimum(i, n_units - 1))
        col = jnp.minimum(j + 1, nj - 1)
        return (u % units_per_block, (u // units_per_block) * nj + col)

    def projected(i):
        return jnp.clip(i - 1, 0, last)

    def convolved(i):
        return jnp.clip(i - 2, 0, last)

    b_specs = [pl.BlockSpec((1, IN_TN), functools.partial(lambda j, i, k: (0, k * nj + j), k=k))
               for k in range(N_PROJ_BLOCKS)]
    ch_spec = lambda rows: pl.BlockSpec((rows, IN_TN), lambda j, i: (0, j))
    proj_spec = pl.BlockSpec((IN_TM, IN_TN), lambda j, i: (projected(i), j))
    conv_spec = pl.BlockSpec((IN_TM, IN_TN), lambda j, i: (convolved(i), j))
    state_spec = lambda rows: pl.BlockSpec((None, rows, IN_TN), lambda j, i: (convolved(i) // tiles_per_seq, 0, j))
    small_spec = pl.BlockSpec((IN_TM, IN_TN), lambda j, i: (0, j))
    sds = jax.ShapeDtypeStruct
    tile_bufs = [pltpu.VMEM((HIST_A + IN_TM, IN_TN), F32), pltpu.VMEM((HIST_B + IN_TM, IN_TN), F32),
                 pltpu.VMEM((IN_TM, IN_TN), F32)]
    return pl.pallas_call(
        functools.partial(_inproj_pipe_kernel, ka, kb, tiles_per_seq, n_tiles, n_s, n_meta),
        grid=(nj, n_tiles + 1),
        in_specs=[pl.BlockSpec((IN_TM, d), lambda j, i: (projected(i), 0)),
                  pl.BlockSpec((IN_TM, d), lambda j, i: (0, 0)),
                  pl.BlockSpec((W_UNIT_ROWS, IN_TN), wnext_map),
                  pl.BlockSpec(memory_space=pl.ANY)] + b_specs + [ch_spec(ka), ch_spec(1), ch_spec(kb)],
        out_specs=[conv_spec, conv_spec, proj_spec, proj_spec, state_spec(ka - 1), state_spec(kb - 1),
                   small_spec, small_spec, small_spec, small_spec, small_spec],
        out_shape=[sds((rp, d), F32), sds((rp, d), BF16), sds((rp, d), BF16), sds((rp, d), BF16),
                   sds((n_seq, ka - 1, d), F32), sds((n_seq, kb - 1, d), F32),
                   sds((IN_TM, d), F32), sds((IN_TM, d), F32), sds((IN_TM, d), F32),
                   sds((IN_TM, d), BF16), sds((IN_TM, d), BF16)],
        scratch_shapes=[pltpu.VMEM((2, N_PROJ_BLOCKS, d, IN_TN), BF16),
                        pltpu.VMEM((2, W_UNIT_ROWS, IN_TN), F32), pltpu.SemaphoreType.DMA((2,)),
                        pltpu.VMEM((IN_TM, d), BF16)] + tile_bufs + tile_bufs + [
                        pltpu.VMEM((HIST_A, IN_TN), F32), pltpu.VMEM((HIST_B, IN_TN), F32)],
        compiler_params=_cparams(2),
        name="inproj_conv",
    )(xp, xs_bf, w, w, *([b] * N_PROJ_BLOCKS), wa, ba, wb)


def _conv_sample_kernel(ka, kb, sta_ref, stb_ref, ga_ref, cb_ref, bg_ref, wa_ref, ba_ref, wb_ref,
                        ua_ref, pb_ref):
    acc = wa_ref[ka - 1:ka, :] * ga_ref[...]
    for k in range(ka - 1):
        acc = acc + wa_ref[k:k + 1, :] * sta_ref[:, k, :]
    ua_ref[...] = acc + ba_ref[...]
    accb = wb_ref[kb - 1:kb, :] * cb_ref[...]
    for k in range(kb - 1):
        accb = accb + wb_ref[k:k + 1, :] * stb_ref[:, k, :]
    pb_ref[...] = (bg_ref[...] * accb).astype(BF16)


def _conv_sample(state_a, state_b, ga, cb, bg, row_block, wa, ba, wb):
    n, _, d = state_a.shape
    ka, kb = wa.shape[0], wb.shape[0]
    row_spec = pl.BlockSpec((n, CONV_TC), lambda c: (row_block, c))
    out_spec = pl.BlockSpec((n, CONV_TC), lambda c: (0, c))
    return pl.pallas_call(
        functools.partial(_conv_sample_kernel, ka, kb),
        grid=(d // CONV_TC,),
        in_specs=[pl.BlockSpec((n, ka - 1, CONV_TC), lambda c: (0, 0, c)),
                  pl.BlockSpec((n, kb - 1, CONV_TC), lambda c: (0, 0, c)),
                  row_spec, row_spec, row_spec,
                  pl.BlockSpec((ka, CONV_TC), lambda c: (0, c)),
                  pl.BlockSpec((1, CONV_TC), lambda c: (0, c)),
                  pl.BlockSpec((kb, CONV_TC), lambda c: (0, c))],
        out_specs=[out_spec, out_spec],
        out_shape=[jax.ShapeDtypeStruct((n, d), F32), jax.ShapeDtypeStruct((n, d), BF16)],
        compiler_params=_cparams(1),
        name="conv_sample",
    )(state_a, state_b, ga, cb, bg, wa, ba, wb)


ROUTE_ID, ROUTE_RANK, ROUTE_W = 0, TOP_K, 2 * TOP_K


def _route_tile(lg, carry):
    tm = lg.shape[0]
    lane = lax.broadcasted_iota(jnp.int32, (tm, LANES), 1)
    neg_inf = jnp.float32(-jnp.inf)

    def first_max(v):
        m = jnp.max(v, axis=-1, keepdims=True)
        return m, jnp.min(jnp.where(v == m, lane, LANES), axis=-1, keepdims=True)

    g_mask = lane < N_GROUPS
    g_max, g_sel = first_max(jnp.where(g_mask, lg, neg_inf))
    g_w = 1.0 / jnp.sum(jnp.where(g_mask, jnp.exp(lg - g_max), 0.0), axis=-1, keepdims=True)
    lane0 = N_GROUPS + g_sel * EXPERTS_PER_GROUP
    e_lg = jnp.where(jnp.logical_and(lane >= lane0, lane < lane0 + EXPERTS_PER_GROUP), lg, neg_inf)
    m1, l1 = first_max(e_lg)
    m2, l2 = first_max(jnp.where(lane == l1, neg_inf, e_lg))
    r = jnp.exp(m2 - m1)
    c1 = g_w / (1.0 + r)
    c2 = g_w * r / (1.0 + r)

    a1 = lane == l1
    a2 = lane == l2
    hit = jnp.where(jnp.logical_or(a1, a2), 1.0, 0.0)
    row = lax.broadcasted_iota(jnp.int32, (tm, tm), 0)
    col = lax.broadcasted_iota(jnp.int32, (tm, tm), 1)
    before = jnp.where(col < row, 1.0, 0.0).astype(BF16)
    seen = jnp.dot(before, hit.astype(BF16), preferred_element_type=F32) + carry
    rank1 = jnp.sum(jnp.where(a1, seen, 0.0), axis=-1, keepdims=True)
    rank2 = jnp.sum(jnp.where(a2, seen, 0.0), axis=-1, keepdims=True)
    carry = carry + jnp.sum(hit, axis=0, keepdims=True)

    rec = jnp.zeros((tm, LANES), F32)
    fields = [(l1 - N_GROUPS).astype(F32), (l2 - N_GROUPS).astype(F32), rank1, rank2, c1, c2]
    for n, v in enumerate(fields):
        rec = jnp.where(lane == n, v, rec)
    return rec, carry


def _mixer_kernel(alpha, n_tiles, ua_ref, pb_ref, sga_ref, sgb_ref, x_ref, wa_ref, wb_ref, wo_ref,
                  lnag_ref, lnab_ref, ln1g_ref, ln1b_ref, wr_ref, br_ref, cnt0_ref, *refs):
    x1_ref, rt_ref, ri_ref, cnt_ref, carry_ref = refs[-5:]
    i = pl.program_id(0)

    @pl.when(i == 0)
    def _():
        carry_ref[...] = cnt0_ref[...]

    @pl.when(i < n_tiles)
    def _():
        un = _layer_norm(ua_ref[...], lnag_ref[...], lnab_ref[...])
        act = (un * _sigmoid(un)).astype(BF16)
        ya = jnp.dot(act, wa_ref[...], preferred_element_type=F32)
        yb = jnp.dot(pb_ref[...], wb_ref[...], preferred_element_type=F32)
        m = (sga_ref[...].astype(F32) * ya + sgb_ref[...].astype(F32) * yb).astype(BF16)
        mixed = jnp.dot(m, wo_ref[...], preferred_element_type=F32)
        x1 = _layer_norm(alpha * x_ref[...] + mixed, ln1g_ref[...], ln1b_ref[...])
        x1_ref[...] = x1
        hi = x1.astype(BF16)
        lo = (x1 - hi.astype(F32)).astype(BF16)
        a = jnp.dot(hi, wr_ref[...], preferred_element_type=F32)
        b = jnp.dot(lo, wr_ref[...], preferred_element_type=F32)
        lg = a[:, :LANES] + a[:, LANES:] + b[:, :LANES] + br_ref[...]
        rec, carry = _route_tile(lg, carry_ref[...])
        carry_ref[...] = carry
        rt_ref[...] = rec
        ri_ref[...] = rec.T[0:2 * TOP_K, :].astype(jnp.int32)

    if len(refs) == 8:
        x1_tail_ref, rt_tail_ref, ri_tail_ref = refs[:3]
        n_tail = x1_tail_ref.shape[0]

        @pl.when(i == n_tiles)
        def _():
            x1_ref[0:n_tail, :] = x1_tail_ref[...]
            rt_ref[0:n_tail, :] = rt_tail_ref[...]
            ri_ref[:, 0:n_tail] = ri_tail_ref[...]

    cnt_ref[...] = carry_ref[...]


def _mixer(tm, n_tiles, gate_off, alpha, ua, pb, sga, sgb, x, wa, wb, wo,
           lnag, lnab, ln1g, ln1b, wr, br, cnt0, tail=None):
    d = x.shape[1]
    last = n_tiles - 1
    in_spec = pl.BlockSpec((tm, d), lambda i: (jnp.minimum(i, last), 0))
    gate_spec = pl.BlockSpec((tm, d), lambda i: (jnp.minimum(i, last) + gate_off, 0))
    vec_spec = pl.BlockSpec((1, d), lambda i: (0, 0))
    lane_spec = pl.BlockSpec((1, LANES), lambda i: (0, 0))
    w_spec = pl.BlockSpec((d, d), lambda i: (0, 0), pipeline_mode=pl.Buffered(1))
    in_specs = [in_spec, in_spec, gate_spec, gate_spec, in_spec, w_spec, w_spec, w_spec,
                vec_spec, vec_spec, vec_spec, vec_spec,
                pl.BlockSpec((d, 2 * LANES), lambda i: (0, 0)), lane_spec, lane_spec]
    args = [ua, pb, sga, sgb, x, wa, wb, wo, lnag, lnab, ln1g, ln1b, wr, br, cnt0]
    n_rows, n_steps = n_tiles * tm, n_tiles
    if tail is not None:
        n_tail = tail[0].shape[0]
        assert n_tail <= tm
        in_specs += [pl.BlockSpec((n_tail, d), lambda i: (0, 0)), pl.BlockSpec((n_tail, LANES), lambda i: (0, 0)),
                     pl.BlockSpec((2 * TOP_K, n_tail), lambda i: (0, 0))]
        args += list(tail)
        n_rows, n_steps = n_rows + n_tail, n_steps + 1
    return pl.pallas_call(
        functools.partial(_mixer_kernel, alpha, n_tiles),
        grid=(n_steps,),
        in_specs=in_specs,
        out_specs=[pl.BlockSpec((tm, d), lambda i: (i, 0)), pl.BlockSpec((tm, LANES), lambda i: (i, 0)),
                   pl.BlockSpec((2 * TOP_K, tm), lambda i: (0, i)), lane_spec],
        out_shape=[jax.ShapeDtypeStruct((n_rows, d), F32), jax.ShapeDtypeStruct((n_rows, LANES), F32),
                   jax.ShapeDtypeStruct((2 * TOP_K, n_rows), jnp.int32), jax.ShapeDtypeStruct((1, LANES), F32)],
        scratch_shapes=[pltpu.VMEM((1, LANES), F32)],
        compiler_params=_cparams(1),
        name="mixer",
    )(*args)


def _plan_kernel(n_tok, n_blocks, *refs):
    id_refs = refs[0:TOP_K]
    rank_refs = refs[TOP_K:2 * TOP_K]
    cnt_ref, be_ref, tok_ref, dst_ref, nu_ref, start_ref = refs[2 * TOP_K:]
    shift = MOE_BM.bit_length() - 1

    def per_expert(e, blk0):
        cnt = cnt_ref[0, N_GROUPS + e]
        nb = lax.shift_right_logical(cnt + (MOE_BM - 1), shift)
        start_ref[e] = blk0 * MOE_BM

        def fill(j, carry):
            be_ref[blk0 + j] = e
            return carry

        lax.fori_loop(0, nb, fill, 0)

        def pad(s, carry):
            tok_ref[s] = 0
            return carry

        lax.fori_loop(blk0 * MOE_BM + cnt, (blk0 + nb) * MOE_BM, pad, 0)
        return blk0 + nb

    n_used = lax.fori_loop(0, N_EXPERTS, per_expert, 0)
    nu_ref[0] = n_used

    def rest(b, carry):
        be_ref[b] = N_EXPERTS - 1

        def pad(s, c):
            tok_ref[b * MOE_BM + s] = 0
            return c

        lax.fori_loop(0, MOE_BM, pad, 0, unroll=8)
        return carry

    lax.fori_loop(n_used, n_blocks, rest, 0)

    def place(t, carry):
        for k in range(TOP_K):
            slot = start_ref[id_refs[k][t]] + rank_refs[k][t]
            dst_ref[k * n_tok + t] = slot
            tok_ref[slot] = t
        return carry

    lax.fori_loop(0, n_tok, place, 0, unroll=8)


def _plan(ids, ranks, cnt):
    n_tok = ids[0].shape[0]
    n_blocks = -(-n_tok * TOP_K // MOE_BM) + N_EXPERTS
    smem = pl.BlockSpec(memory_space=pltpu.SMEM)
    return pl.pallas_call(
        functools.partial(_plan_kernel, n_tok, n_blocks),
        in_specs=[smem] * (2 * TOP_K + 1),
        out_specs=[smem, smem, smem, smem],
        out_shape=[jax.ShapeDtypeStruct((n_blocks,), jnp.int32), jax.ShapeDtypeStruct((n_blocks * MOE_BM,), jnp.int32),
                   jax.ShapeDtypeStruct((TOP_K * n_tok,), jnp.int32), jax.ShapeDtypeStruct((1,), jnp.int32)],
        scratch_shapes=[pltpu.SMEM((N_EXPERTS,), jnp.int32)],
        name="plan",
    )(*ids, *ranks, cnt)


def _moe_kernel(be_ref, tok_ref, nused_ref, cnt_ref, x_hbm, wg_hbm, wu_hbm, wd_hbm, ys_ref,
                xbuf_ref, gsem_ref, wgf_ref, wuf_ref, wdf_ref, wsem_ref, wgu_ref, wdb_ref, ord_ref):
    b = pl.program_id(0)
    n_used = nused_ref[0]
    de2 = wgu_ref.shape[1]
    de = de2 // 2
    shift = MOE_BM.bit_length() - 1

    def weight_copies(e, slot):
        return [pltpu.make_async_copy(wg_hbm.at[e], wgf_ref.at[slot], wsem_ref.at[slot]),
                pltpu.make_async_copy(wu_hbm.at[e], wuf_ref.at[slot], wsem_ref.at[slot]),
                pltpu.make_async_copy(wd_hbm.at[e], wdf_ref.at[slot], wsem_ref.at[slot])]

    def row_copy(blk, slot, r):
        return pltpu.make_async_copy(x_hbm.at[pl.ds(tok_ref[blk * MOE_BM + r], 1)],
                                     xbuf_ref.at[slot, pl.ds(r, 1)], gsem_ref.at[slot])

    @pl.when(jnp.logical_and(b == 0, n_used > 0))
    def _():
        ord_ref[0] = 0
        for cp in weight_copies(be_ref[0], 0):
            cp.start(priority=WEIGHT_DMA_PRIORITY)
        for ahead in range(MOE_AHEAD):
            for r in range(MOE_BM):
                row_copy(jnp.minimum(ahead, n_used - 1), ahead, r).start()

    @pl.when(b < n_used)
    def _():
        e = be_ref[b]

        @pl.when(jnp.logical_or(b == 0, e != be_ref[jnp.maximum(b - 1, 0)]))
        def _():
            order = ord_ref[0]
            wslot = order % 2
            for cp in weight_copies(e, wslot):
                cp.wait()
            nxt = b + lax.shift_right_logical(cnt_ref[0, N_GROUPS + e] + (MOE_BM - 1), shift)

            @pl.when(nxt < n_used)
            def _():
                for cp in weight_copies(be_ref[jnp.minimum(nxt, be_ref.shape[0] - 1)], 1 - wslot):
                    cp.start(priority=WEIGHT_DMA_PRIORITY)

            wgu_ref[:, 0:de] = wgf_ref[wslot].astype(BF16)
            wgu_ref[:, de:de2] = wuf_ref[wslot].astype(BF16)
            wdb_ref[...] = wdf_ref[wslot].astype(BF16)
            ord_ref[0] = order + 1

        n_buf = MOE_AHEAD + 1
        slot = b % n_buf
        for r in range(MOE_BM):
            row_copy(b, slot, r).wait()
        nxt_blk = jnp.minimum(b + MOE_AHEAD, n_used - 1)
        nxt_slot = (b + MOE_AHEAD) % n_buf
        xb = xbuf_ref[slot].astype(BF16)
        n_chunks = de2 // MOE_NC
        per = MOE_BM // n_chunks
        gu = []
        for c in range(n_chunks):
            for r in range(c * per, (c + 1) * per):
                row_copy(nxt_blk, nxt_slot, r).start()
            gu.append(jnp.dot(xb, wgu_ref[:, c * MOE_NC:(c + 1) * MOE_NC], preferred_element_type=F32))
        half = n_chunks // 2
        y = None
        for c in range(half):
            g = gu[c]
            h = (g * _sigmoid(g) * gu[half + c]).astype(BF16)
            part = jnp.dot(h, wdb_ref[c * MOE_NC:(c + 1) * MOE_NC, :], preferred_element_type=F32)
            y = part if y is None else y + part
        ys_ref[...] = y

    @pl.when(b == n_used - 1)
    def _():
        for ahead in range(1, MOE_AHEAD + 1):
            for r in range(MOE_BM):
                row_copy(b, (b + ahead) % (MOE_AHEAD + 1), r).wait()

    @pl.when(b >= n_used)
    def _():
        ys_ref[...] = jnp.zeros_like(ys_ref)


def _moe(block_expert, slot_tok, n_used, cnt, x1, wg, wu, wd):
    n_blocks = block_expert.shape[0]
    n_slots = slot_tok.shape[0]
    _, d, de = wg.shape
    any_spec = pl.BlockSpec(memory_space=pl.ANY)
    grid_spec = pltpu.PrefetchScalarGridSpec(
        num_scalar_prefetch=4,
        grid=(n_blocks,),
        in_specs=[any_spec, any_spec, any_spec, any_spec],
        out_specs=pl.BlockSpec((MOE_BM, d), lambda b, *_: (b, 0)),
        scratch_shapes=[pltpu.VMEM((MOE_AHEAD + 1, MOE_BM, d), F32), pltpu.SemaphoreType.DMA((MOE_AHEAD + 1,)),
                        pltpu.VMEM((2, d, de), F32), pltpu.VMEM((2, d, de), F32), pltpu.VMEM((2, de, d), F32),
                        pltpu.SemaphoreType.DMA((2,)),
                        pltpu.VMEM((d, 2 * de), BF16), pltpu.VMEM((de, d), BF16), pltpu.SMEM((1,), jnp.int32)],
    )
    return pl.pallas_call(
        _moe_kernel,
        grid_spec=grid_spec,
        out_shape=jax.ShapeDtypeStruct((n_slots, d), F32),
        compiler_params=_cparams(1),
        name="moe_ffn",
    )(block_expert, slot_tok, n_used, cnt, x1, wg, wu, wd)


def _combine_kernel(alpha, tile_off, n_tok, dst_ref, ys_hbm, x1_ref, rt_ref, g_ref, b_ref, out_ref,
                    buf_ref, sem_ref):
    i = pl.program_id(0)
    n = pl.num_programs(0)
    tm = x1_ref.shape[0]

    def row_copy(tile, slot, r, k):
        src = dst_ref[k * n_tok + (tile + tile_off) * tm + r]
        return pltpu.make_async_copy(ys_hbm.at[pl.ds(src, 1)], buf_ref.at[slot, k, pl.ds(r, 1)], sem_ref.at[slot])

    n_buf = CMB_AHEAD + 1

    @pl.when(i == 0)
    def _():
        for ahead in range(CMB_AHEAD):
            for r in range(tm):
                for k in range(TOP_K):
                    row_copy(jnp.minimum(ahead, n - 1), ahead, r, k).start()

    slot = i % n_buf
    for r in range(tm):
        for k in range(TOP_K):
            row_copy(i, slot, r, k).wait()
    nxt = jnp.minimum(i + CMB_AHEAD, n - 1)
    nxt_slot = (i + CMB_AHEAD) % n_buf
    for r0 in range(0, tm, CMB_RC):
        for r in range(r0, r0 + CMB_RC):
            for k in range(TOP_K):
                row_copy(nxt, nxt_slot, r, k).start()
        rt = rt_ref[r0:r0 + CMB_RC, :]
        f = (rt[:, ROUTE_W:ROUTE_W + 1] * buf_ref[slot, 0, r0:r0 + CMB_RC, :]
             + rt[:, ROUTE_W + 1:ROUTE_W + 2] * buf_ref[slot, 1, r0:r0 + CMB_RC, :])
        out_ref[r0:r0 + CMB_RC, :] = _layer_norm(alpha * x1_ref[r0:r0 + CMB_RC, :] + f, g_ref[...], b_ref[...])

    @pl.when(i == n - 1)
    def _():
        for ahead in range(1, n_buf):
            for r in range(tm):
                for k in range(TOP_K):
                    row_copy(i, (i + ahead) % n_buf, r, k).wait()


def _combine(tm, n_tiles, tile_off, alpha, dst, ys, x1, rt, g, b):
    n_tok, d = x1.shape
    grid_spec = pltpu.PrefetchScalarGridSpec(
        num_scalar_prefetch=1,
        grid=(n_tiles,),
        in_specs=[pl.BlockSpec(memory_space=pl.ANY),
                  pl.BlockSpec((tm, d), lambda i, *_: (i + tile_off, 0)),
                  pl.BlockSpec((tm, LANES), lambda i, *_: (i + tile_off, 0)),
                  pl.BlockSpec((1, d), lambda i, *_: (0, 0)),
                  pl.BlockSpec((1, d), lambda i, *_: (0, 0))],
        out_specs=pl.BlockSpec((tm, d), lambda i, *_: (i, 0)),
        scratch_shapes=[pltpu.VMEM((CMB_AHEAD + 1, TOP_K, tm, d), F32), pltpu.SemaphoreType.DMA((CMB_AHEAD + 1,))],
    )
    return pl.pallas_call(
        functools.partial(_combine_kernel, alpha, tile_off, n_tok),
        grid_spec=grid_spec,
        out_shape=jax.ShapeDtypeStruct((n_tiles * tm, d), F32),
        compiler_params=_cparams(1),
        name="combine",
    )(dst, ys, x1, rt, g, b)


def kernel(x_prompt, x_sample, state_conv_a, state_conv_b, meta_tokens, w_in, b_in, conv_a_w, conv_a_b, ln_a_g, ln_a_b, w_a_out, conv_b_w, w_b_out, w_o, ln1_g, ln1_b, w_router_group, b_router_group, w_router_expert, b_router_expert, w_exp_gate, w_exp_up, w_exp_down, ln2_g, ln2_b):
    depth = w_in.shape[0]
    assert depth == 1, "single-layer step only"
    n_seq, seq, d = x_prompt.shape
    n_s = x_sample.shape[0]
    n_meta = meta_tokens.shape[0]
    ka, kb = conv_a_w.shape[1], conv_b_w.shape[1]
    assert x_sample.shape[1] == 1 and seq % IN_TM == 0 and IN_TM >= ka - 1
    assert n_meta <= HIST_A and ka - 1 <= HIST_A and kb - 1 <= HIST_B and kb - 1 <= n_meta
    assert n_s + n_meta <= IN_TM and n_s % CMB_TM == 0
    alpha = (2.0 * depth) ** 0.25
    rp = n_seq * seq

    xp = x_prompt.reshape(rp, d)
    xs = jnp.concatenate([x_sample.reshape(n_s, d), meta_tokens,
                          jnp.zeros((IN_TM - n_s - n_meta, d), F32)], axis=0).astype(BF16)
    (ua_p, pb_p, sga_p, sgb_p, new_a_p, new_b_p, ga_x, cb_x, bg_x, sga_x, sgb_x) = _inproj_pipe(
        xp, xs, w_in[0], b_in, conv_a_w[0], conv_a_b, conv_b_w[0], n_seq, seq, n_s, n_meta)
    ua_s, pb_s = _conv_sample(state_conv_a[0], state_conv_b[0], ga_x, cb_x, bg_x, 0,
                              conv_a_w[0], conv_a_b, conv_b_w[0])

    t = rp + n_s
    wr_f = jnp.concatenate([w_router_group[0], w_router_expert[0].transpose(1, 0, 2).reshape(d, N_EXPERTS),
                            jnp.zeros((d, LANES - N_GROUPS - N_EXPERTS), F32)], axis=1)
    wr_hi = wr_f.astype(BF16)
    wr_lo = (wr_f - wr_hi.astype(F32)).astype(BF16)
    wr = jnp.concatenate([wr_hi, wr_lo], axis=1)
    br = jnp.concatenate([b_router_group[0], b_router_expert[0].reshape(-1),
                          jnp.zeros((LANES - N_GROUPS - N_EXPERTS,), F32)])[None, :]
    wa_bf, wb_bf, wo_bf = w_a_out[0].astype(BF16), w_b_out[0].astype(BF16), w_o[0].astype(BF16)
    mix = functools.partial(_mixer, alpha=alpha, wa=wa_bf, wb=wb_bf, wo=wo_bf, lnag=ln_a_g, lnab=ln_a_b,
                            ln1g=ln1_g, ln1b=ln1_b, wr=wr, br=br)
    cnt0 = jnp.zeros((1, LANES), F32)
    x1_s, rt_s, ri_s, cnt_s = mix(n_s, 1, 0, ua=ua_s, pb=pb_s, sga=sga_x, sgb=sgb_x,
                                  x=x_sample.reshape(n_s, d), cnt0=cnt0)
    x1, rt, ri, cnt = mix(MIX_TM, rp // MIX_TM, 0, ua=ua_p, pb=pb_p, sga=sga_p, sgb=sgb_p, x=xp, cnt0=cnt_s,
                          tail=(x1_s, rt_s, ri_s))

    cnt_i = cnt.astype(jnp.int32)
    ids = [ri[ROUTE_ID + k] for k in range(TOP_K)]
    ranks = [ri[ROUTE_RANK + k] for k in range(TOP_K)]
    block_expert, slot_tok, dest, n_used = _plan(ids, ranks, cnt_i)
    ys = _moe(block_expert, slot_tok, n_used, cnt_i, x1, w_exp_gate[0], w_exp_up[0], w_exp_down[0])
    y_p = _combine(CMB_TM, rp // CMB_TM, 0, alpha, dest, ys, x1, rt, ln2_g, ln2_b)
    y_s = _combine(CMB_TM, n_s // CMB_TM, rp // CMB_TM, alpha, dest, ys, x1, rt, ln2_g, ln2_b)

    new_a_s = jnp.concatenate([state_conv_a[0][:, 1:], ga_x[:n_s][:, None]], axis=1)
    new_b_s = jnp.concatenate([state_conv_b[0][:, 1:], cb_x[:n_s][:, None]], axis=1)
    return (y_p.reshape(n_seq, seq, d), y_s.reshape(n_s, 1, d), new_a_p[None], new_b_p[None], new_a_s[None], new_b_s[None])
```

```python
import functools

import jax
import jax.numpy as jnp
from jax import lax
from jax.experimental import pallas as pl
from jax.experimental.pallas import tpu as pltpu

F32 = jnp.float32
BF16 = jnp.bfloat16

LN_EPS = 1e-5
N_GROUPS = 4
EXPERTS_PER_GROUP = 8
N_EXPERTS = N_GROUPS * EXPERTS_PER_GROUP
TOP_K = 2
N_PROJ_BLOCKS = 7

VMEM_LIMIT_BYTES = 56 * 1024 * 1024
LANES = 128

IN_TM = 512
IN_TN = 256
IN_HM = 256
W_UNIT_ROWS = 1024
CONV_TC = 256
CONV_RC = 64
HIST_A = 32
HIST_B = 8
MIX_TM = 256
MOE_BM = 128
MOE_NC = 256
MOE_AHEAD = 4
WEIGHT_DMA_PRIORITY = 1
CMB_TM = 128
CMB_RC = 32
CMB_AHEAD = 3


def _cparams(n_axes):
    return pltpu.CompilerParams(dimension_semantics=("arbitrary",) * n_axes,
                                vmem_limit_bytes=VMEM_LIMIT_BYTES)


def _sigmoid(x):
    return 1.0 / (1.0 + jnp.exp(-x))


def _layer_norm(x, g, b):
    mu = jnp.mean(x, axis=-1, keepdims=True)
    xc = x - mu
    var = jnp.mean(xc * xc, axis=-1, keepdims=True)
    return xc * lax.rsqrt(var + LN_EPS) * g + b


def _conv_taps(src_ref, w_ref, n_taps, hist, r0, rc, c0):
    base = hist - (n_taps - 1)
    acc = None
    for rho in range(8):
        offs = [o for o in range(base, base + n_taps) if o % 8 == rho]
        if not offs:
            continue
        lo = offs[0]
        x = src_ref[pl.ds(r0 + lo, rc + offs[-1] - lo), c0:c0 + LANES]
        for o in offs:
            term = w_ref[o - base:o - base + 1, c0:c0 + LANES] * x[o - lo:o - lo + rc]
            acc = term if acc is None else acc + term
    return acc


def _inproj_conv_kernel(ka, kb, tiles_per_seq, n_s, n_meta, xp_ref, xs_ref, wnext_ref, w_hbm, *refs):
    b_refs = refs[0:7]
    wa_ref, ba_ref, wb_ref = refs[7:10]
    ua_ref, pb_ref, sga_ref, sgb_ref, na_ref, nb_ref = refs[10:16]
    gas_ref, cbs_ref, bgs_ref, sgas_ref, sgbs_ref = refs[16:21]
    wbf_ref, stage_ref, sem_ref, xbf_ref, sa_ref, sb_ref, bg_ref, ha_ref, hb_ref = refs[21:30]
    j = pl.program_id(0)
    i = pl.program_id(1)
    nj = pl.num_programs(0)
    tm, d = xbf_ref.shape
    tn = sa_ref.shape[1]
    unit_rows = stage_ref.shape[1]
    units_per_block = d // unit_rows
    n_units = N_PROJ_BLOCKS * units_per_block

    @pl.when(jnp.logical_and(j == 0, i == 0))
    def _():
        def unit_copy(u):
            k, h = divmod(u, units_per_block)
            return pltpu.make_async_copy(
                w_hbm.at[pl.ds(h * unit_rows, unit_rows), pl.ds(k * d, tn)], stage_ref.at[u % 2], sem_ref.at[u % 2])

        unit_copy(0).start()
        for u in range(n_units):
            if u + 1 < n_units:
                unit_copy(u + 1).start()
            unit_copy(u).wait()
            k, h = divmod(u, units_per_block)
            wbf_ref[0, k, h * unit_rows:(h + 1) * unit_rows, :] = stage_ref[u % 2].astype(BF16)

    @pl.when(jnp.logical_and(i < n_units, j + 1 < nj))
    def _():
        k = i // units_per_block
        h = i % units_per_block
        row0 = pl.multiple_of(h * unit_rows, unit_rows)
        wbf_ref[(j + 1) % 2, k, pl.ds(row0, unit_rows), :] = wnext_ref[...].astype(BF16)

    @pl.when(i == 0)
    def _():
        xbf_ref[...] = xs_ref[...]

    @pl.when(i > 0)
    def _():
        xbf_ref[...] = xp_ref[...].astype(BF16)

    seq_pos = (i - 1) % tiles_per_seq

    @pl.when(jnp.logical_and(i > 0, seq_pos == 0))
    def _():
        sa_ref[0:HIST_A, :] = ha_ref[...]
        sb_ref[0:HIST_B, :] = hb_ref[...]

    slot = j % 2
    halves = list(range(0, tm, IN_HM))

    def proj(k, m0):
        return (jnp.dot(xbf_ref[m0:m0 + IN_HM, :], wbf_ref[slot, k], preferred_element_type=F32)
                + b_refs[k][...])

    chunks = [(c0, r0) for c0 in range(0, tn, LANES) for r0 in range(0, tm, CONV_RC)]

    def conv_a(c0, r0):
        acc = _conv_taps(sa_ref, wa_ref, ka, HIST_A, r0, CONV_RC, c0)
        ua_ref[r0:r0 + CONV_RC, c0:c0 + LANES] = acc + ba_ref[:, c0:c0 + LANES]

    def conv_b(c0, r0):
        accb = _conv_taps(sb_ref, wb_ref, kb, HIST_B, r0, CONV_RC, c0)
        pb_ref[r0:r0 + CONV_RC, c0:c0 + LANES] = (bg_ref[r0:r0 + CONV_RC, c0:c0 + LANES] * accb).astype(BF16)

    todo_a = [functools.partial(conv_a, c0, r0) for c0, r0 in chunks]
    todo_b = [functools.partial(conv_b, c0, r0) for c0, r0 in chunks]

    def run(todo, n):
        for _ in range(min(n, len(todo))):
            todo.pop(0)()

    for m0 in halves:
        sa_ref[HIST_A + m0:HIST_A + m0 + IN_HM, :] = proj(0, m0) * _sigmoid(proj(1, m0))
    for m0 in halves:
        p3 = proj(3, m0)
        run(todo_a, 2)
        p4 = proj(4, m0)
        run(todo_a, 2)
        sb_ref[HIST_B + m0:HIST_B + m0 + IN_HM, :] = p3 * p4
    for m0 in halves:
        bg_ref[m0:m0 + IN_HM, :] = proj(2, m0)
        run(todo_a, 2)
    for m0 in halves:
        sga_ref[m0:m0 + IN_HM, :] = _sigmoid(proj(5, m0)).astype(BF16)
        run(todo_a, 1)
        run(todo_b, 4)
    for m0 in halves:
        sgb_ref[m0:m0 + IN_HM, :] = _sigmoid(proj(6, m0)).astype(BF16)
        run(todo_a, 1)
        run(todo_b, 4)
    run(todo_a, len(todo_a))
    run(todo_b, len(todo_b))

    @pl.when(jnp.logical_and(i > 0, seq_pos == tiles_per_seq - 1))
    def _():
        na_ref[...] = sa_ref[HIST_A + tm - (ka - 1):HIST_A + tm, :]
        nb_ref[...] = sb_ref[HIST_B + tm - (kb - 1):HIST_B + tm, :]

    @pl.when(i == 0)
    def _():
        gas_ref[...] = sa_ref[HIST_A:HIST_A + tm, :]
        cbs_ref[...] = sb_ref[HIST_B:HIST_B + tm, :]
        bgs_ref[...] = bg_ref[...]
        sgas_ref[...] = sga_ref[...]
        sgbs_ref[...] = sgb_ref[...]
        ma, mb = min(n_meta, HIST_A), min(n_meta, HIST_B)
        meta_end = n_s + n_meta
        if ma < HIST_A:
            ha_ref[0:HIST_A - ma, :] = jnp.zeros((HIST_A - ma, tn), F32)
        if mb < HIST_B:
            hb_ref[0:HIST_B - mb, :] = jnp.zeros((HIST_B - mb, tn), F32)
        ha_ref[HIST_A - ma:HIST_A, :] = sa_ref[HIST_A + meta_end - ma:HIST_A + meta_end, :]
        hb_ref[HIST_B - mb:HIST_B, :] = sb_ref[HIST_B + meta_end - mb:HIST_B + meta_end, :]

    sa_ref[0:HIST_A, :] = sa_ref[tm:tm + HIST_A, :]
    sb_ref[0:HIST_B, :] = sb_ref[tm:tm + HIST_B, :]


def _inproj_conv(xp, xs_bf, w, b, wa, ba, wb, n_seq, seq, n_s, n_meta):
    rp, d = xp.shape
    ka, kb = wa.shape[0], wb.shape[0]
    tiles_per_seq = seq // IN_TM
    n_prompt_tiles = rp // IN_TM
    nj = d // IN_TN
    units_per_block = d // W_UNIT_ROWS
    n_units = N_PROJ_BLOCKS * units_per_block

    def wnext_map(j, i):
        u = jnp.where(j == nj - 1, n_units - 1, jnp.minimum(i, n_units - 1))
        col = jnp.minimum(j + 1, nj - 1)
        return (u % units_per_block, (u // units_per_block) * nj + col)

    def prow(j, i):
        return (jnp.maximum(i - 1, 0), j)

    b_specs = [pl.BlockSpec((1, IN_TN), functools.partial(lambda j, i, k: (0, k * nj + j), k=k))
               for k in range(N_PROJ_BLOCKS)]
    ch_spec = lambda rows: pl.BlockSpec((rows, IN_TN), lambda j, i: (0, j))
    prompt_spec = pl.BlockSpec((IN_TM, IN_TN), prow)
    state_spec = lambda rows: pl.BlockSpec((None, rows, IN_TN),
                                           lambda j, i: (jnp.maximum(i - 1, 0) // tiles_per_seq, 0, j))
    small_spec = pl.BlockSpec((IN_TM, IN_TN), lambda j, i: (0, j))
    sds = jax.ShapeDtypeStruct
    return pl.pallas_call(
        functools.partial(_inproj_conv_kernel, ka, kb, tiles_per_seq, n_s, n_meta),
        grid=(nj, n_prompt_tiles + 1),
        in_specs=[pl.BlockSpec((IN_TM, d), lambda j, i: (jnp.maximum(i - 1, 0), 0)),
                  pl.BlockSpec((IN_TM, d), lambda j, i: (0, 0)),
                  pl.BlockSpec((W_UNIT_ROWS, IN_TN), wnext_map),
                  pl.BlockSpec(memory_space=pl.ANY)] + b_specs + [ch_spec(ka), ch_spec(1), ch_spec(kb)],
        out_specs=[prompt_spec, prompt_spec, prompt_spec, prompt_spec, state_spec(ka - 1), state_spec(kb - 1),
                   small_spec, small_spec, small_spec, small_spec, small_spec],
        out_shape=[sds((rp, d), F32), sds((rp, d), BF16), sds((rp, d), BF16), sds((rp, d), BF16),
                   sds((n_seq, ka - 1, d), F32), sds((n_seq, kb - 1, d), F32),
                   sds((IN_TM, d), F32), sds((IN_TM, d), F32), sds((IN_TM, d), F32),
                   sds((IN_TM, d), BF16), sds((IN_TM, d), BF16)],
        scratch_shapes=[pltpu.VMEM((2, N_PROJ_BLOCKS, d, IN_TN), BF16),
                        pltpu.VMEM((2, W_UNIT_ROWS, IN_TN), F32), pltpu.SemaphoreType.DMA((2,)),
                        pltpu.VMEM((IN_TM, d), BF16),
                        pltpu.VMEM((HIST_A + IN_TM, IN_TN), F32), pltpu.VMEM((HIST_B + IN_TM, IN_TN), F32),
                        pltpu.VMEM((IN_TM, IN_TN), F32),
                        pltpu.VMEM((HIST_A, IN_TN), F32), pltpu.VMEM((HIST_B, IN_TN), F32)],
        compiler_params=_cparams(2),
        name="inproj_conv",
    )(xp, xs_bf, w, w, *([b] * N_PROJ_BLOCKS), wa, ba, wb)


def _residues(n_taps, hist):
    return sorted({(hist - (n_taps - 1) + k) % 8 for k in range(n_taps)})


def _inproj_pipe_kernel(ka, kb, tiles_per_seq, n_tiles, n_s, n_meta, xp_ref, xs_ref, wnext_ref, w_hbm, *refs):
    b_refs = refs[0:7]
    wa_ref, ba_ref, wb_ref = refs[7:10]
    ua_ref, pb_ref, sga_ref, sgb_ref, na_ref, nb_ref = refs[10:16]
    gas_ref, cbs_ref, bgs_ref, sgas_ref, sgbs_ref = refs[16:21]
    wbf_ref, stage_ref, sem_ref, xbf_ref = refs[21:25]
    bufs = (refs[25:28], refs[28:31])
    ha_ref, hb_ref = refs[31:33]
    j = pl.program_id(0)
    i = pl.program_id(1)
    nj = pl.num_programs(0)
    tm, d = xbf_ref.shape
    tn = ha_ref.shape[-1]
    unit_rows = stage_ref.shape[1]
    units_per_block = d // unit_rows
    n_units = N_PROJ_BLOCKS * units_per_block
    t = i - 1

    @pl.when(jnp.logical_and(j == 0, i == 0))
    def _():
        def unit_copy(u):
            k, h = divmod(u, units_per_block)
            return pltpu.make_async_copy(
                w_hbm.at[pl.ds(h * unit_rows, unit_rows), pl.ds(k * d, tn)], stage_ref.at[u % 2], sem_ref.at[u % 2])

        unit_copy(0).start()
        for u in range(n_units):
            if u + 1 < n_units:
                unit_copy(u + 1).start()
            unit_copy(u).wait()
            k, h = divmod(u, units_per_block)
            wbf_ref[0, k, h * unit_rows:(h + 1) * unit_rows, :] = stage_ref[u % 2].astype(BF16)
        for buf in bufs:
            for ref in buf:
                ref[...] = jnp.zeros_like(ref)

    @pl.when(jnp.logical_and(i < n_units, j + 1 < nj))
    def _():
        k = i // units_per_block
        h = i % units_per_block
        row0 = pl.multiple_of(h * unit_rows, unit_rows)
        wbf_ref[(j + 1) % 2, k, pl.ds(row0, unit_rows), :] = wnext_ref[...].astype(BF16)

    @pl.when(i == 0)
    def _():
        xbf_ref[...] = xs_ref[...]

    @pl.when(jnp.logical_and(i > 0, i < n_tiles))
    def _():
        xbf_ref[...] = xp_ref[...].astype(BF16)

    res_a, res_b = _residues(ka, HIST_A), _residues(kb, HIST_B)
    meta_end = n_s + n_meta

    def raw_rows(ref, res, hist):
        return ref[res.index(0), hist:hist + tm, :]

    def last_rows(ref, res, hist, n_taps):
        rho = (hist - (n_taps - 1)) % 8
        start = hist + tm - (n_taps - 1) - rho
        return ref[res.index(rho), start:start + n_taps - 1, :]

    def set_history(ref, res, hist, src, row0):
        for q, rho in enumerate(res):
            if hist - rho > 0:
                ref[q, 0:hist - rho, :] = src[q, row0:row0 + hist - rho, :]

    @pl.when(i == 1)
    def _():
        sa_ref, sb_ref, bg_ref = bufs[0]
        gas_ref[...] = raw_rows(sa_ref, res_a, HIST_A)
        cbs_ref[...] = raw_rows(sb_ref, res_b, HIST_B)
        bgs_ref[...] = bg_ref[...]
        sgas_ref[...] = sga_ref[...]
        sgbs_ref[...] = sgb_ref[...]
        for h_ref, src, res, hist in ((ha_ref, sa_ref, res_a, HIST_A), (hb_ref, sb_ref, res_b, HIST_B)):
            for q, rho in enumerate(res):
                h_ref[q] = src[q, meta_end:meta_end + hist, :]
                n_zero = hist - n_meta - rho
                if n_zero > 0:
                    h_ref[q, 0:n_zero, :] = jnp.zeros((n_zero, tn), F32)

    seq_pos = (t - 1) % tiles_per_seq
    for par in range(2):
        sa_ref, sb_ref, _ = bufs[par]
        sa_prev, sb_prev, _ = bufs[1 - par]
        is_t = jnp.logical_and(t >= 1, t % 2 == par)

        @pl.when(jnp.logical_and(is_t, seq_pos == 0))
        def _():
            set_history(sa_ref, res_a, HIST_A, ha_ref, 0)
            set_history(sb_ref, res_b, HIST_B, hb_ref, 0)

        @pl.when(jnp.logical_and(is_t, seq_pos != 0))
        def _():
            set_history(sa_ref, res_a, HIST_A, sa_prev, tm)
            set_history(sb_ref, res_b, HIST_B, sb_prev, tm)

    slot = j % 2
    chunks = [(c0, r0) for c0 in range(0, tn, LANES) for r0 in range(0, tm, CONV_RC)]

    def conv_taps(ref, res, w_ref, n_taps, hist, r0, c0):
        acc = None
        for k in range(n_taps):
            o = hist - (n_taps - 1) + k
            term = w_ref[k:k + 1, c0:c0 + LANES] * ref[res.index(o % 8), pl.ds(r0 + o - o % 8, CONV_RC), c0:c0 + LANES]
            acc = term if acc is None else acc + term
        return acc

    def conv_items(buf):
        sa_ref, sb_ref, bg_ref = buf

        def conv_a(c0, r0):
            acc = conv_taps(sa_ref, res_a, wa_ref, ka, HIST_A, r0, c0)
            ua_ref[r0:r0 + CONV_RC, c0:c0 + LANES] = acc + ba_ref[:, c0:c0 + LANES]

        def conv_b(c0, r0):
            accb = conv_taps(sb_ref, res_b, wb_ref, kb, HIST_B, r0, c0)
            pb_ref[r0:r0 + CONV_RC, c0:c0 + LANES] = (bg_ref[r0:r0 + CONV_RC, c0:c0 + LANES] * accb).astype(BF16)

        return [[functools.partial(conv_a, c0, r0), functools.partial(conv_b, c0, r0)] for c0, r0 in chunks]

    def store_shifted(ref, res, hist, m0, value):
        for q, rho in enumerate(res):
            ref[q, pl.ds(hist + m0 - rho, value.shape[0]), :] = value

    def run(todo, n):
        for _ in range(min(n, len(todo))):
            for item in todo.pop(0):
                item()

    def project(buf, todo):
        sa_ref, sb_ref, bg_ref = buf

        def proj(k, m0):
            return (jnp.dot(xbf_ref[m0:m0 + IN_HM, :], wbf_ref[slot, k], preferred_element_type=F32)
                    + b_refs[k][...])

        run(todo, 1)
        for m0 in range(0, tm, IN_HM):
            p0 = proj(0, m0)
            run(todo, 1)
            p1 = proj(1, m0)
            run(todo, 1)
            store_shifted(sa_ref, res_a, HIST_A, m0, p0 * _sigmoid(p1))
            p3 = proj(3, m0)
            run(todo, 1)
            p4 = proj(4, m0)
            run(todo, 1)
            store_shifted(sb_ref, res_b, HIST_B, m0, p3 * p4)
            bg_ref[m0:m0 + IN_HM, :] = proj(2, m0)
            run(todo, 1)
            sga_ref[m0:m0 + IN_HM, :] = _sigmoid(proj(5, m0)).astype(BF16)
            run(todo, 1)
            sgb_ref[m0:m0 + IN_HM, :] = _sigmoid(proj(6, m0)).astype(BF16)
            run(todo, 1)
        run(todo, len(todo))

    for par in range(2):
        @pl.when(jnp.logical_and(i < n_tiles, i % 2 == par))
        def _():
            project(bufs[par], conv_items(bufs[1 - par]))

    @pl.when(i == n_tiles)
    def _():
        todo = conv_items(bufs[(n_tiles - 1) % 2])
        run(todo, len(todo))

    for par in range(2):
        sa_ref, sb_ref, _ = bufs[par]

        @pl.when(jnp.logical_and(jnp.logical_and(t >= 1, t % 2 == par), seq_pos == tiles_per_seq - 1))
        def _():
            na_ref[...] = last_rows(sa_ref, res_a, HIST_A, ka)
            nb_ref[...] = last_rows(sb_ref, res_b, HIST_B, kb)


def _inproj_pipe(xp, xs_bf, w, b, wa, ba, wb, n_seq, seq, n_s, n_meta):
    rp, d = xp.shape
    ka, kb = wa.shape[0], wb.shape[0]
    tiles_per_seq = seq // IN_TM
    n_prompt_tiles = rp // IN_TM
    n_tiles = n_prompt_tiles + 1
    nj = d // IN_TN
    units_per_block = d // W_UNIT_ROWS
    n_units = N_PROJ_BLOCKS * units_per_block
    last = n_prompt_tiles - 1

    def wnext_map(j, i):
        u = jnp.where(j == nj - 1, n_units - 1, jnp.minimum(i, n_units - 1))
        col = jnp.minimum(j + 1, nj - 1)
        return (u % units_per_block, (u // units_per_block) * nj + col)

    def projected(i):
        return jnp.clip(i - 1, 0, last)

    def convolved(i):
        return jnp.clip(i - 2, 0, last)

    b_specs = [pl.BlockSpec((1, IN_TN), functools.partial(lambda j, i, k: (0, k * nj + j), k=k))
               for k in range(N_PROJ_BLOCKS)]
    ch_spec = lambda rows: pl.BlockSpec((rows, IN_TN), lambda j, i: (0, j))
    proj_spec = pl.BlockSpec((IN_TM, IN_TN), lambda j, i: (projected(i), j))
    conv_spec = pl.BlockSpec((IN_TM, IN_TN), lambda j, i: (convolved(i), j))
    state_spec = lambda rows: pl.BlockSpec((None, rows, IN_TN), lambda j, i: (convolved(i) // tiles_per_seq, 0, j))
    small_spec = pl.BlockSpec((IN_TM, IN_TN), lambda j, i: (0, j))
    sds = jax.ShapeDtypeStruct
    n_res_a, n_res_b = len(_residues(ka, HIST_A)), len(_residues(kb, HIST_B))
    assert (n_s + n_meta) % 8 == 0 and n_s + n_meta + max(HIST_A, HIST_B) <= IN_TM
    tile_bufs = [pltpu.VMEM((n_res_a, HIST_A + IN_TM, IN_TN), F32), pltpu.VMEM((n_res_b, HIST_B + IN_TM, IN_TN), F32),
                 pltpu.VMEM((IN_TM, IN_TN), F32)]
    return pl.pallas_call(
        functools.partial(_inproj_pipe_kernel, ka, kb, tiles_per_seq, n_tiles, n_s, n_meta),
        grid=(nj, n_tiles + 1),
        in_specs=[pl.BlockSpec((IN_TM, d), lambda j, i: (projected(i), 0)),
                  pl.BlockSpec((IN_TM, d), lambda j, i: (0, 0)),
                  pl.BlockSpec((W_UNIT_ROWS, IN_TN), wnext_map),
                  pl.BlockSpec(memory_space=pl.ANY)] + b_specs + [ch_spec(ka), ch_spec(1), ch_spec(kb)],
        out_specs=[conv_spec, conv_spec, proj_spec, proj_spec, state_spec(ka - 1), state_spec(kb - 1),
                   small_spec, small_spec, small_spec, small_spec, small_spec],
        out_shape=[sds((rp, d), F32), sds((rp, d), BF16), sds((rp, d), BF16), sds((rp, d), BF16),
                   sds((n_seq, ka - 1, d), F32), sds((n_seq, kb - 1, d), F32),
                   sds((IN_TM, d), F32), sds((IN_TM, d), F32), sds((IN_TM, d), F32),
                   sds((IN_TM, d), BF16), sds((IN_TM, d), BF16)],
        scratch_shapes=[pltpu.VMEM((2, N_PROJ_BLOCKS, d, IN_TN), BF16),
                        pltpu.VMEM((2, W_UNIT_ROWS, IN_TN), F32), pltpu.SemaphoreType.DMA((2,)),
                        pltpu.VMEM((IN_TM, d), BF16)] + tile_bufs + tile_bufs + [
                        pltpu.VMEM((n_res_a, HIST_A, IN_TN), F32), pltpu.VMEM((n_res_b, HIST_B, IN_TN), F32)],
        compiler_params=_cparams(2),
        name="inproj_conv",
    )(xp, xs_bf, w, w, *([b] * N_PROJ_BLOCKS), wa, ba, wb)


def _conv_sample_kernel(ka, kb, sta_ref, stb_ref, ga_ref, cb_ref, bg_ref, wa_ref, ba_ref, wb_ref,
                        ua_ref, pb_ref):
    acc = wa_ref[ka - 1:ka, :] * ga_ref[...]
    for k in range(ka - 1):
        acc = acc + wa_ref[k:k + 1, :] * sta_ref[:, k, :]
    ua_ref[...] = acc + ba_ref[...]
    accb = wb_ref[kb - 1:kb, :] * cb_ref[...]
    for k in range(kb - 1):
        accb = accb + wb_ref[k:k + 1, :] * stb_ref[:, k, :]
    pb_ref[...] = (bg_ref[...] * accb).astype(BF16)


def _conv_sample(state_a, state_b, ga, cb, bg, row_block, wa, ba, wb):
    n, _, d = state_a.shape
    ka, kb = wa.shape[0], wb.shape[0]
    row_spec = pl.BlockSpec((n, CONV_TC), lambda c: (row_block, c))
    out_spec = pl.BlockSpec((n, CONV_TC), lambda c: (0, c))
    return pl.pallas_call(
        functools.partial(_conv_sample_kernel, ka, kb),
        grid=(d // CONV_TC,),
        in_specs=[pl.BlockSpec((n, ka - 1, CONV_TC), lambda c: (0, 0, c)),
                  pl.BlockSpec((n, kb - 1, CONV_TC), lambda c: (0, 0, c)),
                  row_spec, row_spec, row_spec,
                  pl.BlockSpec((ka, CONV_TC), lambda c: (0, c)),
                  pl.BlockSpec((1, CONV_TC), lambda c: (0, c)),
                  pl.BlockSpec((kb, CONV_TC), lambda c: (0, c))],
        out_specs=[out_spec, out_spec],
        out_shape=[jax.ShapeDtypeStruct((n, d), F32), jax.ShapeDtypeStruct((n, d), BF16)],
        compiler_params=_cparams(1),
        name="conv_sample",
    )(state_a, state_b, ga, cb, bg, wa, ba, wb)


ROUTE_ID, ROUTE_RANK, ROUTE_W = 0, TOP_K, 2 * TOP_K


def _route_tile(lg, carry):
    tm = lg.shape[0]
    lane = lax.broadcasted_iota(jnp.int32, (tm, LANES), 1)
    neg_inf = jnp.float32(-jnp.inf)

    def first_max(v):
        m = jnp.max(v, axis=-1, keepdims=True)
        return m, jnp.min(jnp.where(v == m, lane, LANES), axis=-1, keepdims=True)

    g_mask = lane < N_GROUPS
    g_max, g_sel = first_max(jnp.where(g_mask, lg, neg_inf))
    g_w = 1.0 / jnp.sum(jnp.where(g_mask, jnp.exp(lg - g_max), 0.0), axis=-1, keepdims=True)
    lane0 = N_GROUPS + g_sel * EXPERTS_PER_GROUP
    e_lg = jnp.where(jnp.logical_and(lane >= lane0, lane < lane0 + EXPERTS_PER_GROUP), lg, neg_inf)
    m1, l1 = first_max(e_lg)
    m2, l2 = first_max(jnp.where(lane == l1, neg_inf, e_lg))
    r = jnp.exp(m2 - m1)
    c1 = g_w / (1.0 + r)
    c2 = g_w * r / (1.0 + r)

    a1 = lane == l1
    a2 = lane == l2
    hit = jnp.where(jnp.logical_or(a1, a2), 1.0, 0.0)
    row = lax.broadcasted_iota(jnp.int32, (tm, tm), 0)
    col = lax.broadcasted_iota(jnp.int32, (tm, tm), 1)
    before = jnp.where(col < row, 1.0, 0.0).astype(BF16)
    seen = jnp.dot(before, hit.astype(BF16), preferred_element_type=F32) + carry
    rank1 = jnp.sum(jnp.where(a1, seen, 0.0), axis=-1, keepdims=True)
    rank2 = jnp.sum(jnp.where(a2, seen, 0.0), axis=-1, keepdims=True)
    carry = carry + jnp.sum(hit, axis=0, keepdims=True)

    rec = jnp.zeros((tm, LANES), F32)
    fields = [(l1 - N_GROUPS).astype(F32), (l2 - N_GROUPS).astype(F32), rank1, rank2, c1, c2]
    for n, v in enumerate(fields):
        rec = jnp.where(lane == n, v, rec)
    return rec, carry


def _mixer_kernel(alpha, n_tiles, ua_ref, pb_ref, sga_ref, sgb_ref, x_ref, wa_ref, wb_ref, wo_ref,
                  lnag_ref, lnab_ref, ln1g_ref, ln1b_ref, wr_ref, br_ref, cnt0_ref, *refs):
    x1_ref, rt_ref, ri_ref, cnt_ref, carry_ref = refs[-5:]
    i = pl.program_id(0)

    @pl.when(i == 0)
    def _():
        carry_ref[...] = cnt0_ref[...]

    @pl.when(i < n_tiles)
    def _():
        un = _layer_norm(ua_ref[...], lnag_ref[...], lnab_ref[...])
        act = (un * _sigmoid(un)).astype(BF16)
        ya = jnp.dot(act, wa_ref[...], preferred_element_type=F32)
        yb = jnp.dot(pb_ref[...], wb_ref[...], preferred_element_type=F32)
        m = (sga_ref[...].astype(F32) * ya + sgb_ref[...].astype(F32) * yb).astype(BF16)
        mixed = jnp.dot(m, wo_ref[...], preferred_element_type=F32)
        x1 = _layer_norm(alpha * x_ref[...] + mixed, ln1g_ref[...], ln1b_ref[...])
        x1_ref[...] = x1
        hi = x1.astype(BF16)
        lo = (x1 - hi.astype(F32)).astype(BF16)
        a = jnp.dot(hi, wr_ref[...], preferred_element_type=F32)
        b = jnp.dot(lo, wr_ref[...], preferred_element_type=F32)
        lg = a[:, :LANES] + a[:, LANES:] + b[:, :LANES] + br_ref[...]
        rec, carry = _route_tile(lg, carry_ref[...])
        carry_ref[...] = carry
        rt_ref[...] = rec
        ri_ref[...] = rec.T[0:2 * TOP_K, :].astype(jnp.int32)

    if len(refs) == 8:
        x1_tail_ref, rt_tail_ref, ri_tail_ref = refs[:3]
        n_tail = x1_tail_ref.shape[0]

        @pl.when(i == n_tiles)
        def _():
            x1_ref[0:n_tail, :] = x1_tail_ref[...]
            rt_ref[0:n_tail, :] = rt_tail_ref[...]
            ri_ref[:, 0:n_tail] = ri_tail_ref[...]

    cnt_ref[...] = carry_ref[...]


def _mixer(tm, n_tiles, gate_off, alpha, ua, pb, sga, sgb, x, wa, wb, wo,
           lnag, lnab, ln1g, ln1b, wr, br, cnt0, tail=None):
    d = x.shape[1]
    last = n_tiles - 1
    in_spec = pl.BlockSpec((tm, d), lambda i: (jnp.minimum(i, last), 0))
    gate_spec = pl.BlockSpec((tm, d), lambda i: (jnp.minimum(i, last) + gate_off, 0))
    vec_spec = pl.BlockSpec((1, d), lambda i: (0, 0))
    lane_spec = pl.BlockSpec((1, LANES), lambda i: (0, 0))
    w_spec = pl.BlockSpec((d, d), lambda i: (0, 0), pipeline_mode=pl.Buffered(1))
    in_specs = [in_spec, in_spec, gate_spec, gate_spec, in_spec, w_spec, w_spec, w_spec,
                vec_spec, vec_spec, vec_spec, vec_spec,
                pl.BlockSpec((d, 2 * LANES), lambda i: (0, 0)), lane_spec, lane_spec]
    args = [ua, pb, sga, sgb, x, wa, wb, wo, lnag, lnab, ln1g, ln1b, wr, br, cnt0]
    n_rows, n_steps = n_tiles * tm, n_tiles
    if tail is not None:
        n_tail = tail[0].shape[0]
        assert n_tail <= tm
        in_specs += [pl.BlockSpec((n_tail, d), lambda i: (0, 0)), pl.BlockSpec((n_tail, LANES), lambda i: (0, 0)),
                     pl.BlockSpec((2 * TOP_K, n_tail), lambda i: (0, 0))]
        args += list(tail)
        n_rows, n_steps = n_rows + n_tail, n_steps + 1
    return pl.pallas_call(
        functools.partial(_mixer_kernel, alpha, n_tiles),
        grid=(n_steps,),
        in_specs=in_specs,
        out_specs=[pl.BlockSpec((tm, d), lambda i: (i, 0)), pl.BlockSpec((tm, LANES), lambda i: (i, 0)),
                   pl.BlockSpec((2 * TOP_K, tm), lambda i: (0, i)), lane_spec],
        out_shape=[jax.ShapeDtypeStruct((n_rows, d), F32), jax.ShapeDtypeStruct((n_rows, LANES), F32),
                   jax.ShapeDtypeStruct((2 * TOP_K, n_rows), jnp.int32), jax.ShapeDtypeStruct((1, LANES), F32)],
        scratch_shapes=[pltpu.VMEM((1, LANES), F32)],
        compiler_params=_cparams(1),
        name="mixer",
    )(*args)


def _plan_kernel(n_tok, n_blocks, *refs):
    id_refs = refs[0:TOP_K]
    rank_refs = refs[TOP_K:2 * TOP_K]
    cnt_ref, be_ref, tok_ref, dst_ref, nu_ref, start_ref = refs[2 * TOP_K:]
    shift = MOE_BM.bit_length() - 1

    def per_expert(e, blk0):
        cnt = cnt_ref[0, N_GROUPS + e]
        nb = lax.shift_right_logical(cnt + (MOE_BM - 1), shift)
        start_ref[e] = blk0 * MOE_BM

        def fill(j, carry):
            be_ref[blk0 + j] = e
            return carry

        lax.fori_loop(0, nb, fill, 0)

        def pad(s, carry):
            tok_ref[s] = 0
            return carry

        lax.fori_loop(blk0 * MOE_BM + cnt, (blk0 + nb) * MOE_BM, pad, 0)
        return blk0 + nb

    n_used = lax.fori_loop(0, N_EXPERTS, per_expert, 0)
    nu_ref[0] = n_used

    def rest(b, carry):
        be_ref[b] = N_EXPERTS - 1

        def pad(s, c):
            tok_ref[b * MOE_BM + s] = 0
            return c

        lax.fori_loop(0, MOE_BM, pad, 0, unroll=8)
        return carry

    lax.fori_loop(n_used, n_blocks, rest, 0)

    def place(t, carry):
        for k in range(TOP_K):
            slot = start_ref[id_refs[k][t]] + rank_refs[k][t]
            dst_ref[k * n_tok + t] = slot
            tok_ref[slot] = t
        return carry

    lax.fori_loop(0, n_tok, place, 0, unroll=8)


def _plan(ids, ranks, cnt):
    n_tok = ids[0].shape[0]
    n_blocks = -(-n_tok * TOP_K // MOE_BM) + N_EXPERTS
    smem = pl.BlockSpec(memory_space=pltpu.SMEM)
    return pl.pallas_call(
        functools.partial(_plan_kernel, n_tok, n_blocks),
        in_specs=[smem] * (2 * TOP_K + 1),
        out_specs=[smem, smem, smem, smem],
        out_shape=[jax.ShapeDtypeStruct((n_blocks,), jnp.int32), jax.ShapeDtypeStruct((n_blocks * MOE_BM,), jnp.int32),
                   jax.ShapeDtypeStruct((TOP_K * n_tok,), jnp.int32), jax.ShapeDtypeStruct((1,), jnp.int32)],
        scratch_shapes=[pltpu.SMEM((N_EXPERTS,), jnp.int32)],
        name="plan",
    )(*ids, *ranks, cnt)


def _moe_kernel(be_ref, tok_ref, nused_ref, cnt_ref, x_hbm, wg_hbm, wu_hbm, wd_hbm, ys_ref,
                xbuf_ref, gsem_ref, wgf_ref, wuf_ref, wdf_ref, wsem_ref, wgu_ref, wdb_ref, ord_ref):
    b = pl.program_id(0)
    n_used = nused_ref[0]
    de2 = wgu_ref.shape[1]
    de = de2 // 2
    shift = MOE_BM.bit_length() - 1

    def weight_copies(e, slot):
        return [pltpu.make_async_copy(wg_hbm.at[e], wgf_ref.at[slot], wsem_ref.at[slot]),
                pltpu.make_async_copy(wu_hbm.at[e], wuf_ref.at[slot], wsem_ref.at[slot]),
                pltpu.make_async_copy(wd_hbm.at[e], wdf_ref.at[slot], wsem_ref.at[slot])]

    def row_copy(blk, slot, r):
        return pltpu.make_async_copy(x_hbm.at[pl.ds(tok_ref[blk * MOE_BM + r], 1)],
                                     xbuf_ref.at[slot, pl.ds(r, 1)], gsem_ref.at[slot])

    @pl.when(jnp.logical_and(b == 0, n_used > 0))
    def _():
        ord_ref[0] = 0
        for cp in weight_copies(be_ref[0], 0):
            cp.start(priority=WEIGHT_DMA_PRIORITY)
        for ahead in range(MOE_AHEAD):
            for r in range(MOE_BM):
                row_copy(jnp.minimum(ahead, n_used - 1), ahead, r).start()

    @pl.when(b < n_used)
    def _():
        e = be_ref[b]

        @pl.when(jnp.logical_or(b == 0, e != be_ref[jnp.maximum(b - 1, 0)]))
        def _():
            order = ord_ref[0]
            wslot = order % 2
            for cp in weight_copies(e, wslot):
                cp.wait()
            nxt = b + lax.shift_right_logical(cnt_ref[0, N_GROUPS + e] + (MOE_BM - 1), shift)

            @pl.when(nxt < n_used)
            def _():
                for cp in weight_copies(be_ref[jnp.minimum(nxt, be_ref.shape[0] - 1)], 1 - wslot):
                    cp.start(priority=WEIGHT_DMA_PRIORITY)

            wgu_ref[:, 0:de] = wgf_ref[wslot].astype(BF16)
            wgu_ref[:, de:de2] = wuf_ref[wslot].astype(BF16)
            wdb_ref[...] = wdf_ref[wslot].astype(BF16)
            ord_ref[0] = order + 1

        n_buf = MOE_AHEAD + 1
        slot = b % n_buf
        for r in range(MOE_BM):
            row_copy(b, slot, r).wait()
        nxt_blk = jnp.minimum(b + MOE_AHEAD, n_used - 1)
        nxt_slot = (b + MOE_AHEAD) % n_buf
        xb = xbuf_ref[slot].astype(BF16)
        n_chunks = de2 // MOE_NC
        per = MOE_BM // n_chunks
        gu = []
        for c in range(n_chunks):
            for r in range(c * per, (c + 1) * per):
                row_copy(nxt_blk, nxt_slot, r).start()
            gu.append(jnp.dot(xb, wgu_ref[:, c * MOE_NC:(c + 1) * MOE_NC], preferred_element_type=F32))
        half = n_chunks // 2
        y = None
        for c in range(half):
            g = gu[c]
            h = (g * _sigmoid(g) * gu[half + c]).astype(BF16)
            part = jnp.dot(h, wdb_ref[c * MOE_NC:(c + 1) * MOE_NC, :], preferred_element_type=F32)
            y = part if y is None else y + part
        ys_ref[...] = y

    @pl.when(b == n_used - 1)
    def _():
        for ahead in range(1, MOE_AHEAD + 1):
            for r in range(MOE_BM):
                row_copy(b, (b + ahead) % (MOE_AHEAD + 1), r).wait()

    @pl.when(b >= n_used)
    def _():
        ys_ref[...] = jnp.zeros_like(ys_ref)


def _moe(block_expert, slot_tok, n_used, cnt, x1, wg, wu, wd):
    n_blocks = block_expert.shape[0]
    n_slots = slot_tok.shape[0]
    _, d, de = wg.shape
    any_spec = pl.BlockSpec(memory_space=pl.ANY)
    grid_spec = pltpu.PrefetchScalarGridSpec(
        num_scalar_prefetch=4,
        grid=(n_blocks,),
        in_specs=[any_spec, any_spec, any_spec, any_spec],
        out_specs=pl.BlockSpec((MOE_BM, d), lambda b, *_: (b, 0)),
        scratch_shapes=[pltpu.VMEM((MOE_AHEAD + 1, MOE_BM, d), F32), pltpu.SemaphoreType.DMA((MOE_AHEAD + 1,)),
                        pltpu.VMEM((2, d, de), F32), pltpu.VMEM((2, d, de), F32), pltpu.VMEM((2, de, d), F32),
                        pltpu.SemaphoreType.DMA((2,)),
                        pltpu.VMEM((d, 2 * de), BF16), pltpu.VMEM((de, d), BF16), pltpu.SMEM((1,), jnp.int32)],
    )
    return pl.pallas_call(
        _moe_kernel,
        grid_spec=grid_spec,
        out_shape=jax.ShapeDtypeStruct((n_slots, d), F32),
        compiler_params=_cparams(1),
        name="moe_ffn",
    )(block_expert, slot_tok, n_used, cnt, x1, wg, wu, wd)


def _combine_kernel(alpha, tile_off, n_tok, dst_ref, ys_hbm, x1_ref, rt_ref, g_ref, b_ref, out_ref,
                    buf_ref, sem_ref):
    i = pl.program_id(0)
    n = pl.num_programs(0)
    tm = x1_ref.shape[0]

    def row_copy(tile, slot, r, k):
        src = dst_ref[k * n_tok + (tile + tile_off) * tm + r]
        return pltpu.make_async_copy(ys_hbm.at[pl.ds(src, 1)], buf_ref.at[slot, k, pl.ds(r, 1)], sem_ref.at[slot])

    n_buf = CMB_AHEAD + 1

    @pl.when(i == 0)
    def _():
        for ahead in range(CMB_AHEAD):
            for r in range(tm):
                for k in range(TOP_K):
                    row_copy(jnp.minimum(ahead, n - 1), ahead, r, k).start()

    slot = i % n_buf
    for r in range(tm):
        for k in range(TOP_K):
            row_copy(i, slot, r, k).wait()
    nxt = jnp.minimum(i + CMB_AHEAD, n - 1)
    nxt_slot = (i + CMB_AHEAD) % n_buf
    for r0 in range(0, tm, CMB_RC):
        for r in range(r0, r0 + CMB_RC):
            for k in range(TOP_K):
                row_copy(nxt, nxt_slot, r, k).start()
        rt = rt_ref[r0:r0 + CMB_RC, :]
        f = (rt[:, ROUTE_W:ROUTE_W + 1] * buf_ref[slot, 0, r0:r0 + CMB_RC, :]
             + rt[:, ROUTE_W + 1:ROUTE_W + 2] * buf_ref[slot, 1, r0:r0 + CMB_RC, :])
        out_ref[r0:r0 + CMB_RC, :] = _layer_norm(alpha * x1_ref[r0:r0 + CMB_RC, :] + f, g_ref[...], b_ref[...])

    @pl.when(i == n - 1)
    def _():
        for ahead in range(1, n_buf):
            for r in range(tm):
                for k in range(TOP_K):
                    row_copy(i, (i + ahead) % n_buf, r, k).wait()


def _combine(tm, n_tiles, tile_off, alpha, dst, ys, x1, rt, g, b):
    n_tok, d = x1.shape
    grid_spec = pltpu.PrefetchScalarGridSpec(
        num_scalar_prefetch=1,
        grid=(n_tiles,),
        in_specs=[pl.BlockSpec(memory_space=pl.ANY),
                  pl.BlockSpec((tm, d), lambda i, *_: (i + tile_off, 0)),
                  pl.BlockSpec((tm, LANES), lambda i, *_: (i + tile_off, 0)),
                  pl.BlockSpec((1, d), lambda i, *_: (0, 0)),
                  pl.BlockSpec((1, d), lambda i, *_: (0, 0))],
        out_specs=pl.BlockSpec((tm, d), lambda i, *_: (i, 0)),
        scratch_shapes=[pltpu.VMEM((CMB_AHEAD + 1, TOP_K, tm, d), F32), pltpu.SemaphoreType.DMA((CMB_AHEAD + 1,))],
    )
    return pl.pallas_call(
        functools.partial(_combine_kernel, alpha, tile_off, n_tok),
        grid_spec=grid_spec,
        out_shape=jax.ShapeDtypeStruct((n_tiles * tm, d), F32),
        compiler_params=_cparams(1),
        name="combine",
    )(dst, ys, x1, rt, g, b)


def kernel(x_prompt, x_sample, state_conv_a, state_conv_b, meta_tokens, w_in, b_in, conv_a_w, conv_a_b, ln_a_g, ln_a_b, w_a_out, conv_b_w, w_b_out, w_o, ln1_g, ln1_b, w_router_group, b_router_group, w_router_expert, b_router_expert, w_exp_gate, w_exp_up, w_exp_down, ln2_g, ln2_b):
    depth = w_in.shape[0]
    assert depth == 1, "single-layer step only"
    n_seq, seq, d = x_prompt.shape
    n_s = x_sample.shape[0]
    n_meta = meta_tokens.shape[0]
    ka, kb = conv_a_w.shape[1], conv_b_w.shape[1]
    assert x_sample.shape[1] == 1 and seq % IN_TM == 0 and IN_TM >= ka - 1
    assert n_meta <= HIST_A and ka - 1 <= HIST_A and kb - 1 <= HIST_B and kb - 1 <= n_meta
    assert n_s + n_meta <= IN_TM and n_s % CMB_TM == 0
    alpha = (2.0 * depth) ** 0.25
    rp = n_seq * seq

    xp = x_prompt.reshape(rp, d)
    xs = jnp.concatenate([x_sample.reshape(n_s, d), meta_tokens,
                          jnp.zeros((IN_TM - n_s - n_meta, d), F32)], axis=0).astype(BF16)
    (ua_p, pb_p, sga_p, sgb_p, new_a_p, new_b_p, ga_x, cb_x, bg_x, sga_x, sgb_x) = _inproj_pipe(
        xp, xs, w_in[0], b_in, conv_a_w[0], conv_a_b, conv_b_w[0], n_seq, seq, n_s, n_meta)
    ua_s, pb_s = _conv_sample(state_conv_a[0], state_conv_b[0], ga_x, cb_x, bg_x, 0,
                              conv_a_w[0], conv_a_b, conv_b_w[0])

    t = rp + n_s
    wr_f = jnp.concatenate([w_router_group[0], w_router_expert[0].transpose(1, 0, 2).reshape(d, N_EXPERTS),
                            jnp.zeros((d, LANES - N_GROUPS - N_EXPERTS), F32)], axis=1)
    wr_hi = wr_f.astype(BF16)
    wr_lo = (wr_f - wr_hi.astype(F32)).astype(BF16)
    wr = jnp.concatenate([wr_hi, wr_lo], axis=1)
    br = jnp.concatenate([b_router_group[0], b_router_expert[0].reshape(-1),
                          jnp.zeros((LANES - N_GROUPS - N_EXPERTS,), F32)])[None, :]
    wa_bf, wb_bf, wo_bf = w_a_out[0].astype(BF16), w_b_out[0].astype(BF16), w_o[0].astype(BF16)
    mix = functools.partial(_mixer, alpha=alpha, wa=wa_bf, wb=wb_bf, wo=wo_bf, lnag=ln_a_g, lnab=ln_a_b,
                            ln1g=ln1_g, ln1b=ln1_b, wr=wr, br=br)
    cnt0 = jnp.zeros((1, LANES), F32)
    x1_s, rt_s, ri_s, cnt_s = mix(n_s, 1, 0, ua=ua_s, pb=pb_s, sga=sga_x, sgb=sgb_x,
                                  x=x_sample.reshape(n_s, d), cnt0=cnt0)
    x1, rt, ri, cnt = mix(MIX_TM, rp // MIX_TM, 0, ua=ua_p, pb=pb_p, sga=sga_p, sgb=sgb_p, x=xp, cnt0=cnt_s,
                          tail=(x1_s, rt_s, ri_s))

    cnt_i = cnt.astype(jnp.int32)
    ids = [ri[ROUTE_ID + k] for k in range(TOP_K)]
    ranks = [ri[ROUTE_RANK + k] for k in range(TOP_K)]
    block_expert, slot_tok, dest, n_used = _plan(ids, ranks, cnt_i)
    ys = _moe(block_expert, slot_tok, n_used, cnt_i, x1, w_exp_gate[0], w_exp_up[0], w_exp_down[0])
    y_p = _combine(CMB_TM, rp // CMB_TM, 0, alpha, dest, ys, x1, rt, ln2_g, ln2_b)
    y_s = _combine(CMB_TM, n_s // CMB_TM, rp // CMB_TM, alpha, dest, ys, x1, rt, ln2_g, ln2_b)

    new_a_s = jnp.concatenate([state_conv_a[0][:, 1:], ga_x[:n_s][:, None]], axis=1)
    new_b_s = jnp.concatenate([state_conv_b[0][:, 1:], cb_x[:n_s][:, None]], axis=1)
    return (y_p.reshape(n_seq, seq, d), y_s.reshape(n_s, 1, d), new_a_p[None], new_b_p[None], new_a_s[None], new_b_s[None])
```

```python
import functools

import jax
import jax.numpy as jnp
from jax import lax
from jax.experimental import pallas as pl
from jax.experimental.pallas import tpu as pltpu

F32 = jnp.float32
BF16 = jnp.bfloat16

LN_EPS = 1e-5
N_GROUPS = 4
EXPERTS_PER_GROUP = 8
N_EXPERTS = N_GROUPS * EXPERTS_PER_GROUP
TOP_K = 2
N_PROJ_BLOCKS = 7

VMEM_LIMIT_BYTES = 56 * 1024 * 1024
LANES = 128

IN_TM = 512
IN_TN = 256
IN_HM = 256
W_UNIT_ROWS = 1024
CONV_TC = 256
CONV_RC = 64
HIST_A = 32
HIST_B = 8
MIX_TM = 256
MIX_WROWS = 128
MOE_BM = 128
MOE_NC = 256
MOE_AHEAD = 4
WEIGHT_DMA_PRIORITY = 1
CMB_TM = 128
CMB_RC = 32
CMB_AHEAD = 3


def _cparams(n_axes, vmem_bytes=None):
    limit = VMEM_LIMIT_BYTES if vmem_bytes is None else min(int(vmem_bytes), VMEM_LIMIT_BYTES)
    return pltpu.CompilerParams(dimension_semantics=("arbitrary",) * n_axes, vmem_limit_bytes=limit)


def _sigmoid(x):
    return 1.0 / (1.0 + jnp.exp(-x))


def _layer_norm(x, g, b):
    mu = jnp.mean(x, axis=-1, keepdims=True)
    xc = x - mu
    var = jnp.mean(xc * xc, axis=-1, keepdims=True)
    return xc * lax.rsqrt(var + LN_EPS) * g + b


def _conv_taps(src_ref, w_ref, n_taps, hist, r0, rc, c0):
    base = hist - (n_taps - 1)
    acc = None
    for rho in range(8):
        offs = [o for o in range(base, base + n_taps) if o % 8 == rho]
        if not offs:
            continue
        lo = offs[0]
        x = src_ref[pl.ds(r0 + lo, rc + offs[-1] - lo), c0:c0 + LANES]
        for o in offs:
            term = w_ref[o - base:o - base + 1, c0:c0 + LANES] * x[o - lo:o - lo + rc]
            acc = term if acc is None else acc + term
    return acc


def _inproj_conv_kernel(ka, kb, tiles_per_seq, n_s, n_meta, xp_ref, xs_ref, wnext_ref, w_hbm, *refs):
    b_refs = refs[0:7]
    wa_ref, ba_ref, wb_ref = refs[7:10]
    ua_ref, pb_ref, sga_ref, sgb_ref, na_ref, nb_ref = refs[10:16]
    gas_ref, cbs_ref, bgs_ref, sgas_ref, sgbs_ref = refs[16:21]
    wbf_ref, stage_ref, sem_ref, xbf_ref, sa_ref, sb_ref, bg_ref, ha_ref, hb_ref = refs[21:30]
    j = pl.program_id(0)
    i = pl.program_id(1)
    nj = pl.num_programs(0)
    tm, d = xbf_ref.shape
    tn = sa_ref.shape[1]
    unit_rows = stage_ref.shape[1]
    units_per_block = d // unit_rows
    n_units = N_PROJ_BLOCKS * units_per_block

    @pl.when(jnp.logical_and(j == 0, i == 0))
    def _():
        def unit_copy(u):
            k, h = divmod(u, units_per_block)
            return pltpu.make_async_copy(
                w_hbm.at[pl.ds(h * unit_rows, unit_rows), pl.ds(k * d, tn)], stage_ref.at[u % 2], sem_ref.at[u % 2])

        unit_copy(0).start()
        for u in range(n_units):
            if u + 1 < n_units:
                unit_copy(u + 1).start()
            unit_copy(u).wait()
            k, h = divmod(u, units_per_block)
            wbf_ref[0, k, h * unit_rows:(h + 1) * unit_rows, :] = stage_ref[u % 2].astype(BF16)

    @pl.when(jnp.logical_and(i < n_units, j + 1 < nj))
    def _():
        k = i // units_per_block
        h = i % units_per_block
        row0 = pl.multiple_of(h * unit_rows, unit_rows)
        wbf_ref[(j + 1) % 2, k, pl.ds(row0, unit_rows), :] = wnext_ref[...].astype(BF16)

    @pl.when(i == 0)
    def _():
        xbf_ref[...] = xs_ref[...]

    @pl.when(i > 0)
    def _():
        xbf_ref[...] = xp_ref[...].astype(BF16)

    seq_pos = (i - 1) % tiles_per_seq

    @pl.when(jnp.logical_and(i > 0, seq_pos == 0))
    def _():
        sa_ref[0:HIST_A, :] = ha_ref[...]
        sb_ref[0:HIST_B, :] = hb_ref[...]

    slot = j % 2
    halves = list(range(0, tm, IN_HM))

    def proj(k, m0):
        return (jnp.dot(xbf_ref[m0:m0 + IN_HM, :], wbf_ref[slot, k], preferred_element_type=F32)
                + b_refs[k][...])

    chunks = [(c0, r0) for c0 in range(0, tn, LANES) for r0 in range(0, tm, CONV_RC)]

    def conv_a(c0, r0):
        acc = _conv_taps(sa_ref, wa_ref, ka, HIST_A, r0, CONV_RC, c0)
        ua_ref[r0:r0 + CONV_RC, c0:c0 + LANES] = acc + ba_ref[:, c0:c0 + LANES]

    def conv_b(c0, r0):
        accb = _conv_taps(sb_ref, wb_ref, kb, HIST_B, r0, CONV_RC, c0)
        pb_ref[r0:r0 + CONV_RC, c0:c0 + LANES] = (bg_ref[r0:r0 + CONV_RC, c0:c0 + LANES] * accb).astype(BF16)

    todo_a = [functools.partial(conv_a, c0, r0) for c0, r0 in chunks]
    todo_b = [functools.partial(conv_b, c0, r0) for c0, r0 in chunks]

    def run(todo, n):
        for _ in range(min(n, len(todo))):
            todo.pop(0)()

    for m0 in halves:
        sa_ref[HIST_A + m0:HIST_A + m0 + IN_HM, :] = proj(0, m0) * _sigmoid(proj(1, m0))
    for m0 in halves:
        p3 = proj(3, m0)
        run(todo_a, 2)
        p4 = proj(4, m0)
        run(todo_a, 2)
        sb_ref[HIST_B + m0:HIST_B + m0 + IN_HM, :] = p3 * p4
    for m0 in halves:
        bg_ref[m0:m0 + IN_HM, :] = proj(2, m0)
        run(todo_a, 2)
    for m0 in halves:
        sga_ref[m0:m0 + IN_HM, :] = _sigmoid(proj(5, m0)).astype(BF16)
        run(todo_a, 1)
        run(todo_b, 4)
    for m0 in halves:
        sgb_ref[m0:m0 + IN_HM, :] = _sigmoid(proj(6, m0)).astype(BF16)
        run(todo_a, 1)
        run(todo_b, 4)
    run(todo_a, len(todo_a))
    run(todo_b, len(todo_b))

    @pl.when(jnp.logical_and(i > 0, seq_pos == tiles_per_seq - 1))
    def _():
        na_ref[...] = sa_ref[HIST_A + tm - (ka - 1):HIST_A + tm, :]
        nb_ref[...] = sb_ref[HIST_B + tm - (kb - 1):HIST_B + tm, :]

    @pl.when(i == 0)
    def _():
        gas_ref[...] = sa_ref[HIST_A:HIST_A + tm, :]
        cbs_ref[...] = sb_ref[HIST_B:HIST_B + tm, :]
        bgs_ref[...] = bg_ref[...]
        sgas_ref[...] = sga_ref[...]
        sgbs_ref[...] = sgb_ref[...]
        ma, mb = min(n_meta, HIST_A), min(n_meta, HIST_B)
        meta_end = n_s + n_meta
        if ma < HIST_A:
            ha_ref[0:HIST_A - ma, :] = jnp.zeros((HIST_A - ma, tn), F32)
        if mb < HIST_B:
            hb_ref[0:HIST_B - mb, :] = jnp.zeros((HIST_B - mb, tn), F32)
        ha_ref[HIST_A - ma:HIST_A, :] = sa_ref[HIST_A + meta_end - ma:HIST_A + meta_end, :]
        hb_ref[HIST_B - mb:HIST_B, :] = sb_ref[HIST_B + meta_end - mb:HIST_B + meta_end, :]

    sa_ref[0:HIST_A, :] = sa_ref[tm:tm + HIST_A, :]
    sb_ref[0:HIST_B, :] = sb_ref[tm:tm + HIST_B, :]


def _inproj_conv(xp, xs_bf, w, b, wa, ba, wb, n_seq, seq, n_s, n_meta):
    rp, d = xp.shape
    ka, kb = wa.shape[0], wb.shape[0]
    tiles_per_seq = seq // IN_TM
    n_prompt_tiles = rp // IN_TM
    nj = d // IN_TN
    units_per_block = d // W_UNIT_ROWS
    n_units = N_PROJ_BLOCKS * units_per_block

    def wnext_map(j, i):
        u = jnp.where(j == nj - 1, n_units - 1, jnp.minimum(i, n_units - 1))
        col = jnp.minimum(j + 1, nj - 1)
        return (u % units_per_block, (u // units_per_block) * nj + col)

    def prow(j, i):
        return (jnp.maximum(i - 1, 0), j)

    b_specs = [pl.BlockSpec((1, IN_TN), functools.partial(lambda j, i, k: (0, k * nj + j), k=k))
               for k in range(N_PROJ_BLOCKS)]
    ch_spec = lambda rows: pl.BlockSpec((rows, IN_TN), lambda j, i: (0, j))
    prompt_spec = pl.BlockSpec((IN_TM, IN_TN), prow)
    state_spec = lambda rows: pl.BlockSpec((None, rows, IN_TN),
                                           lambda j, i: (jnp.maximum(i - 1, 0) // tiles_per_seq, 0, j))
    small_spec = pl.BlockSpec((IN_TM, IN_TN), lambda j, i: (0, j))
    sds = jax.ShapeDtypeStruct
    return pl.pallas_call(
        functools.partial(_inproj_conv_kernel, ka, kb, tiles_per_seq, n_s, n_meta),
        grid=(nj, n_prompt_tiles + 1),
        in_specs=[pl.BlockSpec((IN_TM, d), lambda j, i: (jnp.maximum(i - 1, 0), 0)),
                  pl.BlockSpec((IN_TM, d), lambda j, i: (0, 0)),
                  pl.BlockSpec((W_UNIT_ROWS, IN_TN), wnext_map),
                  pl.BlockSpec(memory_space=pl.ANY)] + b_specs + [ch_spec(ka), ch_spec(1), ch_spec(kb)],
        out_specs=[prompt_spec, prompt_spec, prompt_spec, prompt_spec, state_spec(ka - 1), state_spec(kb - 1),
                   small_spec, small_spec, small_spec, small_spec, small_spec],
        out_shape=[sds((rp, d), F32), sds((rp, d), BF16), sds((rp, d), BF16), sds((rp, d), BF16),
                   sds((n_seq, ka - 1, d), F32), sds((n_seq, kb - 1, d), F32),
                   sds((IN_TM, d), F32), sds((IN_TM, d), F32), sds((IN_TM, d), F32),
                   sds((IN_TM, d), BF16), sds((IN_TM, d), BF16)],
        scratch_shapes=[pltpu.VMEM((2, N_PROJ_BLOCKS, d, IN_TN), BF16),
                        pltpu.VMEM((2, W_UNIT_ROWS, IN_TN), F32), pltpu.SemaphoreType.DMA((2,)),
                        pltpu.VMEM((IN_TM, d), BF16),
                        pltpu.VMEM((HIST_A + IN_TM, IN_TN), F32), pltpu.VMEM((HIST_B + IN_TM, IN_TN), F32),
                        pltpu.VMEM((IN_TM, IN_TN), F32),
                        pltpu.VMEM((HIST_A, IN_TN), F32), pltpu.VMEM((HIST_B, IN_TN), F32)],
        compiler_params=_cparams(2),
        name="inproj_conv",
    )(xp, xs_bf, w, w, *([b] * N_PROJ_BLOCKS), wa, ba, wb)


def _residues(n_taps, hist):
    return sorted({(hist - (n_taps - 1) + k) % 8 for k in range(n_taps)})


def _inproj_pipe_kernel(ka, kb, tiles_per_seq, n_tiles, n_s, n_meta, xp_ref, xs_ref, wnext_ref, w_hbm, *refs):
    b_refs = refs[0:7]
    wa_ref, ba_ref, wb_ref = refs[7:10]
    ua_ref, pb_ref, sga_ref, sgb_ref, na_ref, nb_ref = refs[10:16]
    gas_ref, cbs_ref, bgs_ref, sgas_ref, sgbs_ref = refs[16:21]
    wbf_ref, stage_ref, sem_ref, xbf_ref = refs[21:25]
    bufs = (refs[25:28], refs[28:31])
    ha_ref, hb_ref = refs[31:33]
    j = pl.program_id(0)
    i = pl.program_id(1)
    nj = pl.num_programs(0)
    tm, d = xbf_ref.shape
    tn = ha_ref.shape[-1]
    unit_rows = stage_ref.shape[1]
    units_per_block = d // unit_rows
    n_units = N_PROJ_BLOCKS * units_per_block
    t = i - 1

    @pl.when(jnp.logical_and(j == 0, i == 0))
    def _():
        def unit_copy(u):
            k, h = divmod(u, units_per_block)
            return pltpu.make_async_copy(
                w_hbm.at[pl.ds(h * unit_rows, unit_rows), pl.ds(k * d, tn)], stage_ref.at[u % 2], sem_ref.at[u % 2])

        unit_copy(0).start()
        for u in range(n_units):
            if u + 1 < n_units:
                unit_copy(u + 1).start()
            unit_copy(u).wait()
            k, h = divmod(u, units_per_block)
            wbf_ref[0, k, h * unit_rows:(h + 1) * unit_rows, :] = stage_ref[u % 2].astype(BF16)
        for buf in bufs:
            for ref in buf:
                ref[...] = jnp.zeros_like(ref)

    @pl.when(jnp.logical_and(i < n_units, j + 1 < nj))
    def _():
        k = i // units_per_block
        h = i % units_per_block
        row0 = pl.multiple_of(h * unit_rows, unit_rows)
        wbf_ref[(j + 1) % 2, k, pl.ds(row0, unit_rows), :] = wnext_ref[...].astype(BF16)

    @pl.when(i == 0)
    def _():
        xbf_ref[...] = xs_ref[...]

    @pl.when(jnp.logical_and(i > 0, i < n_tiles))
    def _():
        xbf_ref[...] = xp_ref[...].astype(BF16)

    res_a, res_b = _residues(ka, HIST_A), _residues(kb, HIST_B)
    meta_end = n_s + n_meta

    def raw_rows(ref, res, hist):
        return ref[res.index(0), hist:hist + tm, :]

    def last_rows(ref, res, hist, n_taps):
        rho = (hist - (n_taps - 1)) % 8
        start = hist + tm - (n_taps - 1) - rho
        return ref[res.index(rho), start:start + n_taps - 1, :]

    def set_history(ref, res, hist, src, row0):
        for q, rho in enumerate(res):
            if hist - rho > 0:
                ref[q, 0:hist - rho, :] = src[q, row0:row0 + hist - rho, :]

    @pl.when(i == 1)
    def _():
        sa_ref, sb_ref, bg_ref = bufs[0]
        gas_ref[...] = raw_rows(sa_ref, res_a, HIST_A)
        cbs_ref[...] = raw_rows(sb_ref, res_b, HIST_B)
        bgs_ref[...] = bg_ref[...]
        sgas_ref[...] = sga_ref[...]
        sgbs_ref[...] = sgb_ref[...]
        for h_ref, src, res, hist in ((ha_ref, sa_ref, res_a, HIST_A), (hb_ref, sb_ref, res_b, HIST_B)):
            for q, rho in enumerate(res):
                h_ref[q] = src[q, meta_end:meta_end + hist, :]
                n_zero = hist - n_meta - rho
                if n_zero > 0:
                    h_ref[q, 0:n_zero, :] = jnp.zeros((n_zero, tn), F32)

    seq_pos = (t - 1) % tiles_per_seq
    for par in range(2):
        sa_ref, sb_ref, _ = bufs[par]
        sa_prev, sb_prev, _ = bufs[1 - par]
        is_t = jnp.logical_and(t >= 1, t % 2 == par)

        @pl.when(jnp.logical_and(is_t, seq_pos == 0))
        def _():
            set_history(sa_ref, res_a, HIST_A, ha_ref, 0)
            set_history(sb_ref, res_b, HIST_B, hb_ref, 0)

        @pl.when(jnp.logical_and(is_t, seq_pos != 0))
        def _():
            set_history(sa_ref, res_a, HIST_A, sa_prev, tm)
            set_history(sb_ref, res_b, HIST_B, sb_prev, tm)

    slot = j % 2
    chunks = [(c0, r0) for c0 in range(0, tn, LANES) for r0 in range(0, tm, CONV_RC)]

    def conv_taps(ref, res, w_ref, n_taps, hist, r0, c0):
        acc = None
        for k in range(n_taps):
            o = hist - (n_taps - 1) + k
            term = w_ref[k:k + 1, c0:c0 + LANES] * ref[res.index(o % 8), pl.ds(r0 + o - o % 8, CONV_RC), c0:c0 + LANES]
            acc = term if acc is None else acc + term
        return acc

    def conv_items(buf):
        sa_ref, sb_ref, bg_ref = buf

        def conv_a(c0, r0):
            acc = conv_taps(sa_ref, res_a, wa_ref, ka, HIST_A, r0, c0)
            ua_ref[r0:r0 + CONV_RC, c0:c0 + LANES] = acc + ba_ref[:, c0:c0 + LANES]

        def conv_b(c0, r0):
            accb = conv_taps(sb_ref, res_b, wb_ref, kb, HIST_B, r0, c0)
            pb_ref[r0:r0 + CONV_RC, c0:c0 + LANES] = (bg_ref[r0:r0 + CONV_RC, c0:c0 + LANES] * accb).astype(BF16)

        return [[functools.partial(conv_a, c0, r0), functools.partial(conv_b, c0, r0)] for c0, r0 in chunks]

    def store_shifted(ref, res, hist, m0, value):
        for q, rho in enumerate(res):
            ref[q, pl.ds(hist + m0 - rho, value.shape[0]), :] = value

    def run(todo, n):
        for _ in range(min(n, len(todo))):
            for item in todo.pop(0):
                item()

    def project(buf, todo):
        sa_ref, sb_ref, bg_ref = buf

        def proj(k, m0):
            return (jnp.dot(xbf_ref[m0:m0 + IN_HM, :], wbf_ref[slot, k], preferred_element_type=F32)
                    + b_refs[k][...])

        run(todo, 1)
        for m0 in range(0, tm, IN_HM):
            p0 = proj(0, m0)
            run(todo, 1)
            p1 = proj(1, m0)
            run(todo, 1)
            store_shifted(sa_ref, res_a, HIST_A, m0, p0 * _sigmoid(p1))
            p3 = proj(3, m0)
            run(todo, 1)
            p4 = proj(4, m0)
            run(todo, 1)
            store_shifted(sb_ref, res_b, HIST_B, m0, p3 * p4)
            bg_ref[m0:m0 + IN_HM, :] = proj(2, m0)
            run(todo, 1)
            sga_ref[m0:m0 + IN_HM, :] = _sigmoid(proj(5, m0)).astype(BF16)
            run(todo, 1)
            sgb_ref[m0:m0 + IN_HM, :] = _sigmoid(proj(6, m0)).astype(BF16)
            run(todo, 1)
        run(todo, len(todo))

    for par in range(2):
        @pl.when(jnp.logical_and(i < n_tiles, i % 2 == par))
        def _():
            project(bufs[par], conv_items(bufs[1 - par]))

    @pl.when(i == n_tiles)
    def _():
        todo = conv_items(bufs[(n_tiles - 1) % 2])
        run(todo, len(todo))

    for par in range(2):
        sa_ref, sb_ref, _ = bufs[par]

        @pl.when(jnp.logical_and(jnp.logical_and(t >= 1, t % 2 == par), seq_pos == tiles_per_seq - 1))
        def _():
            na_ref[...] = last_rows(sa_ref, res_a, HIST_A, ka)
            nb_ref[...] = last_rows(sb_ref, res_b, HIST_B, kb)


def _inproj_pipe(xp, xs_bf, w, b, wa, ba, wb, n_seq, seq, n_s, n_meta):
    rp, d = xp.shape
    ka, kb = wa.shape[0], wb.shape[0]
    tiles_per_seq = seq // IN_TM
    n_prompt_tiles = rp // IN_TM
    n_tiles = n_prompt_tiles + 1
    nj = d // IN_TN
    units_per_block = d // W_UNIT_ROWS
    n_units = N_PROJ_BLOCKS * units_per_block
    last = n_prompt_tiles - 1

    def wnext_map(j, i):
        u = jnp.where(j == nj - 1, n_units - 1, jnp.minimum(i, n_units - 1))
        col = jnp.minimum(j + 1, nj - 1)
        return (u % units_per_block, (u // units_per_block) * nj + col)

    def projected(i):
        return jnp.clip(i - 1, 0, last)

    def convolved(i):
        return jnp.clip(i - 2, 0, last)

    b_specs = [pl.BlockSpec((1, IN_TN), functools.partial(lambda j, i, k: (0, k * nj + j), k=k))
               for k in range(N_PROJ_BLOCKS)]
    ch_spec = lambda rows: pl.BlockSpec((rows, IN_TN), lambda j, i: (0, j))
    proj_spec = pl.BlockSpec((IN_TM, IN_TN), lambda j, i: (projected(i), j))
    conv_spec = pl.BlockSpec((IN_TM, IN_TN), lambda j, i: (convolved(i), j))
    state_spec = lambda rows: pl.BlockSpec((None, rows, IN_TN), lambda j, i: (convolved(i) // tiles_per_seq, 0, j))
    small_spec = pl.BlockSpec((IN_TM, IN_TN), lambda j, i: (0, j))
    sds = jax.ShapeDtypeStruct
    n_res_a, n_res_b = len(_residues(ka, HIST_A)), len(_residues(kb, HIST_B))
    assert (n_s + n_meta) % 8 == 0 and n_s + n_meta + max(HIST_A, HIST_B) <= IN_TM
    tile_bufs = [pltpu.VMEM((n_res_a, HIST_A + IN_TM, IN_TN), F32), pltpu.VMEM((n_res_b, HIST_B + IN_TM, IN_TN), F32),
                 pltpu.VMEM((IN_TM, IN_TN), F32)]
    return pl.pallas_call(
        functools.partial(_inproj_pipe_kernel, ka, kb, tiles_per_seq, n_tiles, n_s, n_meta),
        grid=(nj, n_tiles + 1),
        in_specs=[pl.BlockSpec((IN_TM, d), lambda j, i: (projected(i), 0)),
                  pl.BlockSpec((IN_TM, d), lambda j, i: (0, 0)),
                  pl.BlockSpec((W_UNIT_ROWS, IN_TN), wnext_map),
                  pl.BlockSpec(memory_space=pl.ANY)] + b_specs + [ch_spec(ka), ch_spec(1), ch_spec(kb)],
        out_specs=[conv_spec, conv_spec, proj_spec, proj_spec, state_spec(ka - 1), state_spec(kb - 1),
                   small_spec, small_spec, small_spec, small_spec, small_spec],
        out_shape=[sds((rp, d), F32), sds((rp, d), BF16), sds((rp, d), BF16), sds((rp, d), BF16),
                   sds((n_seq, ka - 1, d), F32), sds((n_seq, kb - 1, d), F32),
                   sds((IN_TM, d), F32), sds((IN_TM, d), F32), sds((IN_TM, d), F32),
                   sds((IN_TM, d), BF16), sds((IN_TM, d), BF16)],
        scratch_shapes=[pltpu.VMEM((2, N_PROJ_BLOCKS, d, IN_TN), BF16),
                        pltpu.VMEM((2, W_UNIT_ROWS, IN_TN), F32), pltpu.SemaphoreType.DMA((2,)),
                        pltpu.VMEM((IN_TM, d), BF16)] + tile_bufs + tile_bufs + [
                        pltpu.VMEM((n_res_a, HIST_A, IN_TN), F32), pltpu.VMEM((n_res_b, HIST_B, IN_TN), F32)],
        compiler_params=_cparams(2),
        name="inproj_conv",
    )(xp, xs_bf, w, w, *([b] * N_PROJ_BLOCKS), wa, ba, wb)


def _conv_sample_kernel(ka, kb, sta_ref, stb_ref, ga_ref, cb_ref, bg_ref, wa_ref, ba_ref, wb_ref,
                        ua_ref, pb_ref, nsa_ref, nsb_ref):
    ga = ga_ref[...]
    cb = cb_ref[...]
    acc = wa_ref[ka - 1:ka, :] * ga
    for k in range(ka - 1):
        acc = acc + wa_ref[k:k + 1, :] * sta_ref[:, k, :]
    ua_ref[...] = acc + ba_ref[...]
    accb = wb_ref[kb - 1:kb, :] * cb
    for k in range(kb - 1):
        accb = accb + wb_ref[k:k + 1, :] * stb_ref[:, k, :]
    pb_ref[...] = (bg_ref[...] * accb).astype(BF16)
    for k in range(ka - 2):
        nsa_ref[:, k, :] = sta_ref[:, k + 1, :]
    nsa_ref[:, ka - 2, :] = ga
    for k in range(kb - 2):
        nsb_ref[:, k, :] = stb_ref[:, k + 1, :]
    nsb_ref[:, kb - 2, :] = cb


def _conv_sample(state_a, state_b, ga, cb, bg, row_block, wa, ba, wb):
    n, _, d = state_a.shape
    ka, kb = wa.shape[0], wb.shape[0]
    row_spec = pl.BlockSpec((n, CONV_TC), lambda c: (row_block, c))
    out_spec = pl.BlockSpec((n, CONV_TC), lambda c: (0, c))
    return pl.pallas_call(
        functools.partial(_conv_sample_kernel, ka, kb),
        grid=(d // CONV_TC,),
        in_specs=[pl.BlockSpec((n, ka - 1, CONV_TC), lambda c: (0, 0, c)),
                  pl.BlockSpec((n, kb - 1, CONV_TC), lambda c: (0, 0, c)),
                  row_spec, row_spec, row_spec,
                  pl.BlockSpec((ka, CONV_TC), lambda c: (0, c)),
                  pl.BlockSpec((1, CONV_TC), lambda c: (0, c)),
                  pl.BlockSpec((kb, CONV_TC), lambda c: (0, c))],
        out_specs=[out_spec, out_spec,
                   pl.BlockSpec((n, ka - 1, CONV_TC), lambda c: (0, 0, c)),
                   pl.BlockSpec((n, kb - 1, CONV_TC), lambda c: (0, 0, c))],
        out_shape=[jax.ShapeDtypeStruct((n, d), F32), jax.ShapeDtypeStruct((n, d), BF16),
                   jax.ShapeDtypeStruct(state_a.shape, F32), jax.ShapeDtypeStruct(state_b.shape, F32)],
        compiler_params=_cparams(1),
        name="conv_sample",
    )(state_a, state_b, ga, cb, bg, wa, ba, wb)


ROUTE_ID, ROUTE_RANK, ROUTE_W = 0, TOP_K, 2 * TOP_K


def _route_tile(lg, carry):
    tm = lg.shape[0]
    lane = lax.broadcasted_iota(jnp.int32, (tm, LANES), 1)
    neg_inf = jnp.float32(-jnp.inf)

    def first_max(v):
        m = jnp.max(v, axis=-1, keepdims=True)
        return m, jnp.min(jnp.where(v == m, lane, LANES), axis=-1, keepdims=True)

    g_mask = lane < N_GROUPS
    g_max, g_sel = first_max(jnp.where(g_mask, lg, neg_inf))
    g_w = 1.0 / jnp.sum(jnp.where(g_mask, jnp.exp(lg - g_max), 0.0), axis=-1, keepdims=True)
    lane0 = N_GROUPS + g_sel * EXPERTS_PER_GROUP
    e_lg = jnp.where(jnp.logical_and(lane >= lane0, lane < lane0 + EXPERTS_PER_GROUP), lg, neg_inf)
    m1, l1 = first_max(e_lg)
    m2, l2 = first_max(jnp.where(lane == l1, neg_inf, e_lg))
    r = jnp.exp(m2 - m1)
    c1 = g_w / (1.0 + r)
    c2 = g_w * r / (1.0 + r)

    a1 = lane == l1
    a2 = lane == l2
    hit = jnp.where(jnp.logical_or(a1, a2), 1.0, 0.0)
    row = lax.broadcasted_iota(jnp.int32, (tm, tm), 0)
    col = lax.broadcasted_iota(jnp.int32, (tm, tm), 1)
    before = jnp.where(col < row, 1.0, 0.0).astype(BF16)
    seen = jnp.dot(before, hit.astype(BF16), preferred_element_type=F32) + carry
    rank1 = jnp.sum(jnp.where(a1, seen, 0.0), axis=-1, keepdims=True)
    rank2 = jnp.sum(jnp.where(a2, seen, 0.0), axis=-1, keepdims=True)
    carry = carry + jnp.sum(hit, axis=0, keepdims=True)

    rec = jnp.zeros((tm, LANES), F32)
    fields = [(l1 - N_GROUPS).astype(F32), (l2 - N_GROUPS).astype(F32), rank1, rank2, c1, c2]
    for n, v in enumerate(fields):
        rec = jnp.where(lane == n, v, rec)
    return rec, carry


def _mixer_kernel(alpha, n_tiles, ua_ref, pb_ref, sga_ref, sgb_ref, x_ref, wa_hbm, wb_hbm, wo_hbm,
                  lnag_ref, lnab_ref, ln1g_ref, ln1b_ref, wrf_ref, br_ref, *refs):
    x1_ref, rt_ref, ri_ref, cnt_ref = refs[-11:-7]
    carry_ref, wa_ref, wb_ref, wo_ref, wr_ref, stage_ref, sem_ref = refs[-7:]
    i = pl.program_id(0)

    @pl.when(i == 0)
    def _():
        carry_ref[...] = jnp.zeros_like(carry_ref)
        rows = stage_ref.shape[1]
        jobs = [(src, dst, r0) for src, dst in ((wa_hbm, wa_ref), (wb_hbm, wb_ref), (wo_hbm, wo_ref))
                for r0 in range(0, src.shape[0], rows)]

        def chunk_copy(n):
            src, _, r0 = jobs[n]
            return pltpu.make_async_copy(src.at[pl.ds(r0, rows)], stage_ref.at[n % 2], sem_ref.at[n % 2])

        chunk_copy(0).start()
        for n, (_, dst, r0) in enumerate(jobs):
            if n + 1 < len(jobs):
                chunk_copy(n + 1).start()
            chunk_copy(n).wait()
            dst[r0:r0 + rows, :] = stage_ref[n % 2].astype(BF16)
        wr = wrf_ref[...]
        wr_hi = wr.astype(BF16)
        wr_ref[:, 0:LANES] = wr_hi
        wr_ref[:, LANES:2 * LANES] = (wr - wr_hi.astype(F32)).astype(BF16)

    def tile(rows, ua_t, pb_t, sga_t, sgb_t, x_t):
        un = _layer_norm(ua_t[...], lnag_ref[...], lnab_ref[...])
        act = (un * _sigmoid(un)).astype(BF16)
        ya = jnp.dot(act, wa_ref[...], preferred_element_type=F32)
        yb = jnp.dot(pb_t[...], wb_ref[...], preferred_element_type=F32)
        m = (sga_t[...].astype(F32) * ya + sgb_t[...].astype(F32) * yb).astype(BF16)
        mixed = jnp.dot(m, wo_ref[...], preferred_element_type=F32)
        x1 = _layer_norm(alpha * x_t[...] + mixed, ln1g_ref[...], ln1b_ref[...])
        x1_ref[0:rows, :] = x1
        hi = x1.astype(BF16)
        lo = (x1 - hi.astype(F32)).astype(BF16)
        a = jnp.dot(hi, wr_ref[...], preferred_element_type=F32)
        b = jnp.dot(lo, wr_ref[...], preferred_element_type=F32)
        lg = a[:, :LANES] + a[:, LANES:] + b[:, :LANES] + br_ref[...]
        rec, carry = _route_tile(lg, carry_ref[...])
        carry_ref[...] = carry
        rt_ref[0:rows, :] = rec
        ri_ref[:, 0:rows] = rec.T[0:2 * TOP_K, :].astype(jnp.int32)

    @pl.when(i < n_tiles)
    def _():
        tile(x_ref.shape[0], ua_ref, pb_ref, sga_ref, sgb_ref, x_ref)

    if len(refs) == 16:
        tail_refs = refs[:5]

        @pl.when(i == n_tiles)
        def _():
            tile(tail_refs[0].shape[0], *tail_refs)

    cnt_ref[...] = carry_ref[...]


def _mixer(tm, n_tiles, alpha, ua, pb, sga, sgb, x, wa, wb, wo, lnag, lnab, ln1g, ln1b, wr, br, tail=None):
    d = x.shape[1]
    last = n_tiles - 1
    in_spec = pl.BlockSpec((tm, d), lambda i: (jnp.minimum(i, last), 0))
    vec_spec = pl.BlockSpec((1, d), lambda i: (0, 0))
    lane_spec = pl.BlockSpec((1, LANES), lambda i: (0, 0))
    w_spec = pl.BlockSpec(memory_space=pl.ANY)
    in_specs = [in_spec, in_spec, in_spec, in_spec, in_spec, w_spec, w_spec, w_spec,
                vec_spec, vec_spec, vec_spec, vec_spec,
                pl.BlockSpec((d, LANES), lambda i: (0, 0)), lane_spec]
    args = [ua, pb, sga, sgb, x, wa, wb, wo, lnag, lnab, ln1g, ln1b, wr, br]
    n_rows, n_steps, n_tail = n_tiles * tm, n_tiles, 0
    if tail is not None:
        n_tail = tail[0].shape[0]
        assert n_tail <= tm and all(a.shape[0] >= n_tail for a in tail)
        in_specs += [pl.BlockSpec((n_tail, d), lambda i: (0, 0), pipeline_mode=pl.Buffered(1))] * len(tail)
        args += list(tail)
        n_rows, n_steps = n_rows + n_tail, n_steps + 1
    return pl.pallas_call(
        functools.partial(_mixer_kernel, alpha, n_tiles),
        grid=(n_steps,),
        in_specs=in_specs,
        out_specs=[pl.BlockSpec((tm, d), lambda i: (i, 0)), pl.BlockSpec((tm, LANES), lambda i: (i, 0)),
                   pl.BlockSpec((2 * TOP_K, tm), lambda i: (0, i)), lane_spec],
        out_shape=[jax.ShapeDtypeStruct((n_rows, d), F32), jax.ShapeDtypeStruct((n_rows, LANES), F32),
                   jax.ShapeDtypeStruct((2 * TOP_K, n_rows), jnp.int32), jax.ShapeDtypeStruct((1, LANES), F32)],
        scratch_shapes=[pltpu.VMEM((1, LANES), F32), pltpu.VMEM((d, d), BF16), pltpu.VMEM((d, d), BF16),
                        pltpu.VMEM((d, d), BF16), pltpu.VMEM((d, 2 * LANES), BF16),
                        pltpu.VMEM((2, MIX_WROWS, d), F32), pltpu.SemaphoreType.DMA((2,))],
        compiler_params=_cparams(1, 3 * d * d * 2 + 2 * MIX_WROWS * d * 4 + (2 * 18 + 12) * tm * d
                                 + 18 * n_tail * d + (2 << 20)),
        name="mixer",
    )(*args)


def _plan_kernel(n_tok, n_blocks, *refs):
    id_refs = refs[0:TOP_K]
    rank_refs = refs[TOP_K:2 * TOP_K]
    cnt_ref, be_ref, tok_ref, dst_ref, nu_ref, start_ref = refs[2 * TOP_K:]
    shift = MOE_BM.bit_length() - 1

    def per_expert(e, blk0):
        cnt = cnt_ref[0, N_GROUPS + e]
        nb = lax.shift_right_logical(cnt + (MOE_BM - 1), shift)
        start_ref[e] = blk0 * MOE_BM

        def fill(j, carry):
            be_ref[blk0 + j] = e
            return carry

        lax.fori_loop(0, nb, fill, 0)

        def pad(s, carry):
            tok_ref[s] = 0
            return carry

        lax.fori_loop(blk0 * MOE_BM + cnt, (blk0 + nb) * MOE_BM, pad, 0)
        return blk0 + nb

    n_used = lax.fori_loop(0, N_EXPERTS, per_expert, 0)
    nu_ref[0] = n_used

    def rest(b, carry):
        be_ref[b] = N_EXPERTS - 1

        def pad(s, c):
            tok_ref[b * MOE_BM + s] = 0
            return c

        lax.fori_loop(0, MOE_BM, pad, 0, unroll=8)
        return carry

    lax.fori_loop(n_used, n_blocks, rest, 0)

    def place(t, carry):
        for k in range(TOP_K):
            slot = start_ref[id_refs[k][t]] + rank_refs[k][t]
            dst_ref[k * n_tok + t] = slot
            tok_ref[slot] = t
        return carry

    lax.fori_loop(0, n_tok, place, 0, unroll=8)


def _plan(ids, ranks, cnt):
    n_tok = ids[0].shape[0]
    n_blocks = -(-n_tok * TOP_K // MOE_BM) + N_EXPERTS
    smem = pl.BlockSpec(memory_space=pltpu.SMEM)
    return pl.pallas_call(
        functools.partial(_plan_kernel, n_tok, n_blocks),
        in_specs=[smem] * (2 * TOP_K + 1),
        out_specs=[smem, smem, smem, smem],
        out_shape=[jax.ShapeDtypeStruct((n_blocks,), jnp.int32), jax.ShapeDtypeStruct((n_blocks * MOE_BM,), jnp.int32),
                   jax.ShapeDtypeStruct((TOP_K * n_tok,), jnp.int32), jax.ShapeDtypeStruct((1,), jnp.int32)],
        scratch_shapes=[pltpu.SMEM((N_EXPERTS,), jnp.int32)],
        name="plan",
    )(*ids, *ranks, cnt)


def _moe_kernel(be_ref, tok_ref, nused_ref, cnt_ref, x_hbm, wg_hbm, wu_hbm, wd_hbm, ys_ref,
                xbuf_ref, gsem_ref, wgf_ref, wuf_ref, wdf_ref, wsem_ref, wgu_ref, wdb_ref, ord_ref):
    b = pl.program_id(0)
    n_used = nused_ref[0]
    de2 = wgu_ref.shape[1]
    de = de2 // 2
    shift = MOE_BM.bit_length() - 1

    def weight_copies(e, slot):
        return [pltpu.make_async_copy(wg_hbm.at[e], wgf_ref.at[slot], wsem_ref.at[slot]),
                pltpu.make_async_copy(wu_hbm.at[e], wuf_ref.at[slot], wsem_ref.at[slot]),
                pltpu.make_async_copy(wd_hbm.at[e], wdf_ref.at[slot], wsem_ref.at[slot])]

    def row_copy(blk, slot, r):
        return pltpu.make_async_copy(x_hbm.at[pl.ds(tok_ref[blk * MOE_BM + r], 1)],
                                     xbuf_ref.at[slot, pl.ds(r, 1)], gsem_ref.at[slot])

    @pl.when(jnp.logical_and(b == 0, n_used > 0))
    def _():
        ord_ref[0] = 0
        for cp in weight_copies(be_ref[0], 0):
            cp.start(priority=WEIGHT_DMA_PRIORITY)
        for ahead in range(MOE_AHEAD):
            for r in range(MOE_BM):
                row_copy(jnp.minimum(ahead, n_used - 1), ahead, r).start()

    @pl.when(b < n_used)
    def _():
        e = be_ref[b]

        @pl.when(jnp.logical_or(b == 0, e != be_ref[jnp.maximum(b - 1, 0)]))
        def _():
            order = ord_ref[0]
            wslot = order % 2
            for cp in weight_copies(e, wslot):
                cp.wait()
            nxt = b + lax.shift_right_logical(cnt_ref[0, N_GROUPS + e] + (MOE_BM - 1), shift)

            @pl.when(nxt < n_used)
            def _():
                for cp in weight_copies(be_ref[jnp.minimum(nxt, be_ref.shape[0] - 1)], 1 - wslot):
                    cp.start(priority=WEIGHT_DMA_PRIORITY)

            wgu_ref[:, 0:de] = wgf_ref[wslot].astype(BF16)
            wgu_ref[:, de:de2] = wuf_ref[wslot].astype(BF16)
            wdb_ref[...] = wdf_ref[wslot].astype(BF16)
            ord_ref[0] = order + 1

        n_buf = MOE_AHEAD + 1
        slot = b % n_buf
        for r in range(MOE_BM):
            row_copy(b, slot, r).wait()
        nxt_blk = jnp.minimum(b + MOE_AHEAD, n_used - 1)
        nxt_slot = (b + MOE_AHEAD) % n_buf
        xb = xbuf_ref[slot].astype(BF16)
        n_chunks = de2 // MOE_NC
        per = MOE_BM // n_chunks
        gu = []
        for c in range(n_chunks):
            for r in range(c * per, (c + 1) * per):
                row_copy(nxt_blk, nxt_slot, r).start()
            gu.append(jnp.dot(xb, wgu_ref[:, c * MOE_NC:(c + 1) * MOE_NC], preferred_element_type=F32))
        half = n_chunks // 2
        y = None
        for c in range(half):
            g = gu[c]
            h = (g * _sigmoid(g) * gu[half + c]).astype(BF16)
            part = jnp.dot(h, wdb_ref[c * MOE_NC:(c + 1) * MOE_NC, :], preferred_element_type=F32)
            y = part if y is None else y + part
        ys_ref[...] = y

    @pl.when(b == n_used - 1)
    def _():
        for ahead in range(1, MOE_AHEAD + 1):
            for r in range(MOE_BM):
                row_copy(b, (b + ahead) % (MOE_AHEAD + 1), r).wait()

    @pl.when(b >= n_used)
    def _():
        ys_ref[...] = jnp.zeros_like(ys_ref)


def _moe(block_expert, slot_tok, n_used, cnt, x1, wg, wu, wd):
    n_blocks = block_expert.shape[0]
    n_slots = slot_tok.shape[0]
    _, d, de = wg.shape
    any_spec = pl.BlockSpec(memory_space=pl.ANY)
    grid_spec = pltpu.PrefetchScalarGridSpec(
        num_scalar_prefetch=4,
        grid=(n_blocks,),
        in_specs=[any_spec, any_spec, any_spec, any_spec],
        out_specs=pl.BlockSpec((MOE_BM, d), lambda b, *_: (b, 0)),
        scratch_shapes=[pltpu.VMEM((MOE_AHEAD + 1, MOE_BM, d), F32), pltpu.SemaphoreType.DMA((MOE_AHEAD + 1,)),
                        pltpu.VMEM((2, d, de), F32), pltpu.VMEM((2, d, de), F32), pltpu.VMEM((2, de, d), F32),
                        pltpu.SemaphoreType.DMA((2,)),
                        pltpu.VMEM((d, 2 * de), BF16), pltpu.VMEM((de, d), BF16), pltpu.SMEM((1,), jnp.int32)],
    )
    return pl.pallas_call(
        _moe_kernel,
        grid_spec=grid_spec,
        out_shape=jax.ShapeDtypeStruct((n_slots, d), F32),
        compiler_params=_cparams(1),
        name="moe_ffn",
    )(block_expert, slot_tok, n_used, cnt, x1, wg, wu, wd)


def _combine_kernel(alpha, tile_off, n_tok, dst_ref, ys_hbm, x1_ref, rt_ref, g_ref, b_ref, out_ref,
                    buf_ref, sem_ref):
    i = pl.program_id(0)
    n = pl.num_programs(0)
    tm = x1_ref.shape[0]

    def row_copy(tile, slot, r, k):
        src = dst_ref[k * n_tok + (tile + tile_off) * tm + r]
        return pltpu.make_async_copy(ys_hbm.at[pl.ds(src, 1)], buf_ref.at[slot, k, pl.ds(r, 1)], sem_ref.at[slot])

    n_buf = CMB_AHEAD + 1

    @pl.when(i == 0)
    def _():
        for ahead in range(CMB_AHEAD):
            for r in range(tm):
                for k in range(TOP_K):
                    row_copy(jnp.minimum(ahead, n - 1), ahead, r, k).start()

    slot = i % n_buf
    for r in range(tm):
        for k in range(TOP_K):
            row_copy(i, slot, r, k).wait()
    nxt = jnp.minimum(i + CMB_AHEAD, n - 1)
    nxt_slot = (i + CMB_AHEAD) % n_buf
    for r0 in range(0, tm, CMB_RC):
        for r in range(r0, r0 + CMB_RC):
            for k in range(TOP_K):
                row_copy(nxt, nxt_slot, r, k).start()
        rt = rt_ref[r0:r0 + CMB_RC, :]
        f = (rt[:, ROUTE_W:ROUTE_W + 1] * buf_ref[slot, 0, r0:r0 + CMB_RC, :]
             + rt[:, ROUTE_W + 1:ROUTE_W + 2] * buf_ref[slot, 1, r0:r0 + CMB_RC, :])
        out_ref[r0:r0 + CMB_RC, :] = _layer_norm(alpha * x1_ref[r0:r0 + CMB_RC, :] + f, g_ref[...], b_ref[...])

    @pl.when(i == n - 1)
    def _():
        for ahead in range(1, n_buf):
            for r in range(tm):
                for k in range(TOP_K):
                    row_copy(i, (i + ahead) % n_buf, r, k).wait()


def _combine(tm, n_tiles, tile_off, alpha, dst, ys, x1, rt, g, b):
    n_tok, d = x1.shape
    grid_spec = pltpu.PrefetchScalarGridSpec(
        num_scalar_prefetch=1,
        grid=(n_tiles,),
        in_specs=[pl.BlockSpec(memory_space=pl.ANY),
                  pl.BlockSpec((tm, d), lambda i, *_: (i + tile_off, 0)),
                  pl.BlockSpec((tm, LANES), lambda i, *_: (i + tile_off, 0)),
                  pl.BlockSpec((1, d), lambda i, *_: (0, 0)),
                  pl.BlockSpec((1, d), lambda i, *_: (0, 0))],
        out_specs=pl.BlockSpec((tm, d), lambda i, *_: (i, 0)),
        scratch_shapes=[pltpu.VMEM((CMB_AHEAD + 1, TOP_K, tm, d), F32), pltpu.SemaphoreType.DMA((CMB_AHEAD + 1,))],
    )
    return pl.pallas_call(
        functools.partial(_combine_kernel, alpha, tile_off, n_tok),
        grid_spec=grid_spec,
        out_shape=jax.ShapeDtypeStruct((n_tiles * tm, d), F32),
        compiler_params=_cparams(1),
        name="combine",
    )(dst, ys, x1, rt, g, b)


def kernel(x_prompt, x_sample, state_conv_a, state_conv_b, meta_tokens, w_in, b_in, conv_a_w, conv_a_b, ln_a_g, ln_a_b, w_a_out, conv_b_w, w_b_out, w_o, ln1_g, ln1_b, w_router_group, b_router_group, w_router_expert, b_router_expert, w_exp_gate, w_exp_up, w_exp_down, ln2_g, ln2_b):
    depth = w_in.shape[0]
    assert depth == 1, "single-layer step only"
    n_seq, seq, d = x_prompt.shape
    n_s = x_sample.shape[0]
    n_meta = meta_tokens.shape[0]
    ka, kb = conv_a_w.shape[1], conv_b_w.shape[1]
    assert x_sample.shape[1] == 1 and seq % IN_TM == 0 and IN_TM >= ka - 1
    assert n_meta <= HIST_A and ka - 1 <= HIST_A and kb - 1 <= HIST_B and kb - 1 <= n_meta
    assert n_s + n_meta <= IN_TM and n_s % CMB_TM == 0
    alpha = (2.0 * depth) ** 0.25
    rp = n_seq * seq

    xp = x_prompt.reshape(rp, d)
    xs = jnp.concatenate([x_sample.reshape(n_s, d), meta_tokens,
                          jnp.zeros((IN_TM - n_s - n_meta, d), F32)], axis=0).astype(BF16)
    (ua_p, pb_p, sga_p, sgb_p, new_a_p, new_b_p, ga_x, cb_x, bg_x, sga_x, sgb_x) = _inproj_pipe(
        xp, xs, w_in[0], b_in, conv_a_w[0], conv_a_b, conv_b_w[0], n_seq, seq, n_s, n_meta)
    ua_s, pb_s, new_a_s, new_b_s = _conv_sample(state_conv_a[0], state_conv_b[0], ga_x, cb_x, bg_x, 0,
                                                conv_a_w[0], conv_a_b, conv_b_w[0])

    t = rp + n_s
    wr_f = jnp.concatenate([w_router_group[0], w_router_expert[0].transpose(1, 0, 2).reshape(d, N_EXPERTS),
                            jnp.zeros((d, LANES - N_GROUPS - N_EXPERTS), F32)], axis=1)
    br = jnp.concatenate([b_router_group[0], b_router_expert[0].reshape(-1),
                          jnp.zeros((LANES - N_GROUPS - N_EXPERTS,), F32)])[None, :]
    x1, rt, ri, cnt = _mixer(MIX_TM, rp // MIX_TM, alpha, ua_p, pb_p, sga_p, sgb_p, xp, w_a_out[0], w_b_out[0], w_o[0],
                             ln_a_g, ln_a_b, ln1_g, ln1_b, wr_f, br,
                             tail=(ua_s, pb_s, sga_x, sgb_x, x_sample.reshape(n_s, d)))

    cnt_i = cnt.astype(jnp.int32)
    ids = [ri[ROUTE_ID + k] for k in range(TOP_K)]
    ranks = [ri[ROUTE_RANK + k] for k in range(TOP_K)]
    block_expert, slot_tok, dest, n_used = _plan(ids, ranks, cnt_i)
    ys = _moe(block_expert, slot_tok, n_used, cnt_i, x1, w_exp_gate[0], w_exp_up[0], w_exp_down[0])
    y_p = _combine(CMB_TM, rp // CMB_TM, 0, alpha, dest, ys, x1, rt, ln2_g, ln2_b)
    y_s = _combine(CMB_TM, n_s // CMB_TM, rp // CMB_TM, alpha, dest, ys, x1, rt, ln2_g, ln2_b)

    return (y_p.reshape(n_seq, seq, d), y_s.reshape(n_s, 1, d), new_a_p[None], new_b_p[None], new_a_s[None], new_b_s[None])
```

```python
import functools

import jax
import jax.numpy as jnp
from jax import lax
from jax.experimental import pallas as pl
from jax.experimental.pallas import tpu as pltpu

F32 = jnp.float32
BF16 = jnp.bfloat16

LN_EPS = 1e-5
N_GROUPS = 4
EXPERTS_PER_GROUP = 8
N_EXPERTS = N_GROUPS * EXPERTS_PER_GROUP
TOP_K = 2
N_PROJ_BLOCKS = 7

VMEM_LIMIT_BYTES = 56 * 1024 * 1024
LANES = 128

IN_TM = 512
IN_TN = 256
IN_HM = 256
W_UNIT_ROWS = 1024
CONV_TC = 256
CONV_RC = 64
HIST_A = 32
HIST_B = 8
MIX_TM = 256
MIX_WROWS = 128
MOE_BM = 128
MOE_NC = 256
MOE_AHEAD = 4
WEIGHT_DMA_PRIORITY = 1
CMB_TM = 128
CMB_RC = 32
CMB_AHEAD = 3


def _cparams(n_axes, vmem_bytes=None):
    limit = VMEM_LIMIT_BYTES if vmem_bytes is None else min(int(vmem_bytes), VMEM_LIMIT_BYTES)
    return pltpu.CompilerParams(dimension_semantics=("arbitrary",) * n_axes, vmem_limit_bytes=limit)


def _sigmoid(x):
    return 1.0 / (1.0 + jnp.exp(-x))


def _layer_norm(x, g, b):
    mu = jnp.mean(x, axis=-1, keepdims=True)
    xc = x - mu
    var = jnp.mean(xc * xc, axis=-1, keepdims=True)
    return xc * lax.rsqrt(var + LN_EPS) * g + b


def _conv_taps(src_ref, w_ref, n_taps, hist, r0, rc, c0):
    base = hist - (n_taps - 1)
    acc = None
    for rho in range(8):
        offs = [o for o in range(base, base + n_taps) if o % 8 == rho]
        if not offs:
            continue
        lo = offs[0]
        x = src_ref[pl.ds(r0 + lo, rc + offs[-1] - lo), c0:c0 + LANES]
        for o in offs:
            term = w_ref[o - base:o - base + 1, c0:c0 + LANES] * x[o - lo:o - lo + rc]
            acc = term if acc is None else acc + term
    return acc


def _inproj_conv_kernel(ka, kb, tiles_per_seq, n_s, n_meta, xp_ref, xs_ref, wnext_ref, w_hbm, *refs):
    b_refs = refs[0:7]
    wa_ref, ba_ref, wb_ref = refs[7:10]
    ua_ref, pb_ref, sga_ref, sgb_ref, na_ref, nb_ref = refs[10:16]
    gas_ref, cbs_ref, bgs_ref, sgas_ref, sgbs_ref = refs[16:21]
    wbf_ref, stage_ref, sem_ref, xbf_ref, sa_ref, sb_ref, bg_ref, ha_ref, hb_ref = refs[21:30]
    j = pl.program_id(0)
    i = pl.program_id(1)
    nj = pl.num_programs(0)
    tm, d = xbf_ref.shape
    tn = sa_ref.shape[1]
    unit_rows = stage_ref.shape[1]
    units_per_block = d // unit_rows
    n_units = N_PROJ_BLOCKS * units_per_block

    @pl.when(jnp.logical_and(j == 0, i == 0))
    def _():
        def unit_copy(u):
            k, h = divmod(u, units_per_block)
            return pltpu.make_async_copy(
                w_hbm.at[pl.ds(h * unit_rows, unit_rows), pl.ds(k * d, tn)], stage_ref.at[u % 2], sem_ref.at[u % 2])

        unit_copy(0).start()
        for u in range(n_units):
            if u + 1 < n_units:
                unit_copy(u + 1).start()
            unit_copy(u).wait()
            k, h = divmod(u, units_per_block)
            wbf_ref[0, k, h * unit_rows:(h + 1) * unit_rows, :] = stage_ref[u % 2].astype(BF16)

    @pl.when(jnp.logical_and(i < n_units, j + 1 < nj))
    def _():
        k = i // units_per_block
        h = i % units_per_block
        row0 = pl.multiple_of(h * unit_rows, unit_rows)
        wbf_ref[(j + 1) % 2, k, pl.ds(row0, unit_rows), :] = wnext_ref[...].astype(BF16)

    @pl.when(i == 0)
    def _():
        xbf_ref[...] = xs_ref[...]

    @pl.when(i > 0)
    def _():
        xbf_ref[...] = xp_ref[...].astype(BF16)

    seq_pos = (i - 1) % tiles_per_seq

    @pl.when(jnp.logical_and(i > 0, seq_pos == 0))
    def _():
        sa_ref[0:HIST_A, :] = ha_ref[...]
        sb_ref[0:HIST_B, :] = hb_ref[...]

    slot = j % 2
    halves = list(range(0, tm, IN_HM))

    def proj(k, m0):
        return (jnp.dot(xbf_ref[m0:m0 + IN_HM, :], wbf_ref[slot, k], preferred_element_type=F32)
                + b_refs[k][...])

    chunks = [(c0, r0) for c0 in range(0, tn, LANES) for r0 in range(0, tm, CONV_RC)]

    def conv_a(c0, r0):
        acc = _conv_taps(sa_ref, wa_ref, ka, HIST_A, r0, CONV_RC, c0)
        ua_ref[r0:r0 + CONV_RC, c0:c0 + LANES] = acc + ba_ref[:, c0:c0 + LANES]

    def conv_b(c0, r0):
        accb = _conv_taps(sb_ref, wb_ref, kb, HIST_B, r0, CONV_RC, c0)
        pb_ref[r0:r0 + CONV_RC, c0:c0 + LANES] = (bg_ref[r0:r0 + CONV_RC, c0:c0 + LANES] * accb).astype(BF16)

    todo_a = [functools.partial(conv_a, c0, r0) for c0, r0 in chunks]
    todo_b = [functools.partial(conv_b, c0, r0) for c0, r0 in chunks]

    def run(todo, n):
        for _ in range(min(n, len(todo))):
            todo.pop(0)()

    for m0 in halves:
        sa_ref[HIST_A + m0:HIST_A + m0 + IN_HM, :] = proj(0, m0) * _sigmoid(proj(1, m0))
    for m0 in halves:
        p3 = proj(3, m0)
        run(todo_a, 2)
        p4 = proj(4, m0)
        run(todo_a, 2)
        sb_ref[HIST_B + m0:HIST_B + m0 + IN_HM, :] = p3 * p4
    for m0 in halves:
        bg_ref[m0:m0 + IN_HM, :] = proj(2, m0)
        run(todo_a, 2)
    for m0 in halves:
        sga_ref[m0:m0 + IN_HM, :] = _sigmoid(proj(5, m0)).astype(BF16)
        run(todo_a, 1)
        run(todo_b, 4)
    for m0 in halves:
        sgb_ref[m0:m0 + IN_HM, :] = _sigmoid(proj(6, m0)).astype(BF16)
        run(todo_a, 1)
        run(todo_b, 4)
    run(todo_a, len(todo_a))
    run(todo_b, len(todo_b))

    @pl.when(jnp.logical_and(i > 0, seq_pos == tiles_per_seq - 1))
    def _():
        na_ref[...] = sa_ref[HIST_A + tm - (ka - 1):HIST_A + tm, :]
        nb_ref[...] = sb_ref[HIST_B + tm - (kb - 1):HIST_B + tm, :]

    @pl.when(i == 0)
    def _():
        gas_ref[...] = sa_ref[HIST_A:HIST_A + tm, :]
        cbs_ref[...] = sb_ref[HIST_B:HIST_B + tm, :]
        bgs_ref[...] = bg_ref[...]
        sgas_ref[...] = sga_ref[...]
        sgbs_ref[...] = sgb_ref[...]
        ma, mb = min(n_meta, HIST_A), min(n_meta, HIST_B)
        meta_end = n_s + n_meta
        if ma < HIST_A:
            ha_ref[0:HIST_A - ma, :] = jnp.zeros((HIST_A - ma, tn), F32)
        if mb < HIST_B:
            hb_ref[0:HIST_B - mb, :] = jnp.zeros((HIST_B - mb, tn), F32)
        ha_ref[HIST_A - ma:HIST_A, :] = sa_ref[HIST_A + meta_end - ma:HIST_A + meta_end, :]
        hb_ref[HIST_B - mb:HIST_B, :] = sb_ref[HIST_B + meta_end - mb:HIST_B + meta_end, :]

    sa_ref[0:HIST_A, :] = sa_ref[tm:tm + HIST_A, :]
    sb_ref[0:HIST_B, :] = sb_ref[tm:tm + HIST_B, :]


def _inproj_conv(xp, xs_bf, w, b, wa, ba, wb, n_seq, seq, n_s, n_meta):
    rp, d = xp.shape
    ka, kb = wa.shape[0], wb.shape[0]
    tiles_per_seq = seq // IN_TM
    n_prompt_tiles = rp // IN_TM
    nj = d // IN_TN
    units_per_block = d // W_UNIT_ROWS
    n_units = N_PROJ_BLOCKS * units_per_block

    def wnext_map(j, i):
        u = jnp.where(j == nj - 1, n_units - 1, jnp.minimum(i, n_units - 1))
        col = jnp.minimum(j + 1, nj - 1)
        return (u % units_per_block, (u // units_per_block) * nj + col)

    def prow(j, i):
        return (jnp.maximum(i - 1, 0), j)

    b_specs = [pl.BlockSpec((1, IN_TN), functools.partial(lambda j, i, k: (0, k * nj + j), k=k))
               for k in range(N_PROJ_BLOCKS)]
    ch_spec = lambda rows: pl.BlockSpec((rows, IN_TN), lambda j, i: (0, j))
    prompt_spec = pl.BlockSpec((IN_TM, IN_TN), prow)
    state_spec = lambda rows: pl.BlockSpec((None, rows, IN_TN),
                                           lambda j, i: (jnp.maximum(i - 1, 0) // tiles_per_seq, 0, j))
    small_spec = pl.BlockSpec((IN_TM, IN_TN), lambda j, i: (0, j))
    sds = jax.ShapeDtypeStruct
    return pl.pallas_call(
        functools.partial(_inproj_conv_kernel, ka, kb, tiles_per_seq, n_s, n_meta),
        grid=(nj, n_prompt_tiles + 1),
        in_specs=[pl.BlockSpec((IN_TM, d), lambda j, i: (jnp.maximum(i - 1, 0), 0)),
                  pl.BlockSpec((IN_TM, d), lambda j, i: (0, 0)),
                  pl.BlockSpec((W_UNIT_ROWS, IN_TN), wnext_map),
                  pl.BlockSpec(memory_space=pl.ANY)] + b_specs + [ch_spec(ka), ch_spec(1), ch_spec(kb)],
        out_specs=[prompt_spec, prompt_spec, prompt_spec, prompt_spec, state_spec(ka - 1), state_spec(kb - 1),
                   small_spec, small_spec, small_spec, small_spec, small_spec],
        out_shape=[sds((rp, d), F32), sds((rp, d), BF16), sds((rp, d), BF16), sds((rp, d), BF16),
                   sds((n_seq, ka - 1, d), F32), sds((n_seq, kb - 1, d), F32),
                   sds((IN_TM, d), F32), sds((IN_TM, d), F32), sds((IN_TM, d), F32),
                   sds((IN_TM, d), BF16), sds((IN_TM, d), BF16)],
        scratch_shapes=[pltpu.VMEM((2, N_PROJ_BLOCKS, d, IN_TN), BF16),
                        pltpu.VMEM((2, W_UNIT_ROWS, IN_TN), F32), pltpu.SemaphoreType.DMA((2,)),
                        pltpu.VMEM((IN_TM, d), BF16),
                        pltpu.VMEM((HIST_A + IN_TM, IN_TN), F32), pltpu.VMEM((HIST_B + IN_TM, IN_TN), F32),
                        pltpu.VMEM((IN_TM, IN_TN), F32),
                        pltpu.VMEM((HIST_A, IN_TN), F32), pltpu.VMEM((HIST_B, IN_TN), F32)],
        compiler_params=_cparams(2),
        name="inproj_conv",
    )(xp, xs_bf, w, w, *([b] * N_PROJ_BLOCKS), wa, ba, wb)


def _residues(n_taps, hist):
    return sorted({(hist - (n_taps - 1) + k) % 8 for k in range(n_taps)})


def _inproj_pipe_kernel(ka, kb, tiles_per_seq, n_tiles, n_s, n_meta, xp_ref, xs_ref, wnext_ref, w_hbm, *refs):
    b_refs = refs[0:7]
    wa_ref, ba_ref, wb_ref = refs[7:10]
    ua_ref, pb_ref, sga_ref, sgb_ref, na_ref, nb_ref = refs[10:16]
    gas_ref, cbs_ref, bgs_ref, sgas_ref, sgbs_ref = refs[16:21]
    wbf_ref, stage_ref, sem_ref, xbf_ref = refs[21:25]
    bufs = (refs[25:28], refs[28:31])
    ha_ref, hb_ref = refs[31:33]
    j = pl.program_id(0)
    i = pl.program_id(1)
    nj = pl.num_programs(0)
    tm, d = xbf_ref.shape
    tn = ha_ref.shape[-1]
    unit_rows = stage_ref.shape[1]
    units_per_block = d // unit_rows
    n_units = N_PROJ_BLOCKS * units_per_block
    t = i - 1

    @pl.when(jnp.logical_and(j == 0, i == 0))
    def _():
        def unit_copy(u):
            k, h = divmod(u, units_per_block)
            return pltpu.make_async_copy(
                w_hbm.at[pl.ds(h * unit_rows, unit_rows), pl.ds(k * d, tn)], stage_ref.at[u % 2], sem_ref.at[u % 2])

        unit_copy(0).start()
        for u in range(n_units):
            if u + 1 < n_units:
                unit_copy(u + 1).start()
            unit_copy(u).wait()
            k, h = divmod(u, units_per_block)
            wbf_ref[0, k, h * unit_rows:(h + 1) * unit_rows, :] = stage_ref[u % 2].astype(BF16)
        for buf in bufs:
            for ref in buf:
                ref[...] = jnp.zeros_like(ref)

    @pl.when(jnp.logical_and(i < n_units, j + 1 < nj))
    def _():
        k = i // units_per_block
        h = i % units_per_block
        row0 = pl.multiple_of(h * unit_rows, unit_rows)
        wbf_ref[(j + 1) % 2, k, pl.ds(row0, unit_rows), :] = wnext_ref[...].astype(BF16)

    @pl.when(i == 0)
    def _():
        xbf_ref[...] = xs_ref[...]

    @pl.when(jnp.logical_and(i > 0, i < n_tiles))
    def _():
        xbf_ref[...] = xp_ref[...].astype(BF16)

    res_a, res_b = _residues(ka, HIST_A), _residues(kb, HIST_B)
    meta_end = n_s + n_meta

    def raw_rows(ref, res, hist):
        return ref[res.index(0), hist:hist + tm, :]

    def last_rows(ref, res, hist, n_taps):
        rho = (hist - (n_taps - 1)) % 8
        start = hist + tm - (n_taps - 1) - rho
        return ref[res.index(rho), start:start + n_taps - 1, :]

    def set_history(ref, res, hist, src, row0):
        for q, rho in enumerate(res):
            if hist - rho > 0:
                ref[q, 0:hist - rho, :] = src[q, row0:row0 + hist - rho, :]

    @pl.when(i == 1)
    def _():
        sa_ref, sb_ref, bg_ref = bufs[0]
        gas_ref[...] = raw_rows(sa_ref, res_a, HIST_A)
        cbs_ref[...] = raw_rows(sb_ref, res_b, HIST_B)
        bgs_ref[...] = bg_ref[...]
        sgas_ref[...] = sga_ref[...]
        sgbs_ref[...] = sgb_ref[...]
        for h_ref, src, res, hist in ((ha_ref, sa_ref, res_a, HIST_A), (hb_ref, sb_ref, res_b, HIST_B)):
            for q, rho in enumerate(res):
                h_ref[q] = src[q, meta_end:meta_end + hist, :]
                n_zero = hist - n_meta - rho
                if n_zero > 0:
                    h_ref[q, 0:n_zero, :] = jnp.zeros((n_zero, tn), F32)

    seq_pos = (t - 1) % tiles_per_seq
    for par in range(2):
        sa_ref, sb_ref, _ = bufs[par]
        sa_prev, sb_prev, _ = bufs[1 - par]
        is_t = jnp.logical_and(t >= 1, t % 2 == par)

        @pl.when(jnp.logical_and(is_t, seq_pos == 0))
        def _():
            set_history(sa_ref, res_a, HIST_A, ha_ref, 0)
            set_history(sb_ref, res_b, HIST_B, hb_ref, 0)

        @pl.when(jnp.logical_and(is_t, seq_pos != 0))
        def _():
            set_history(sa_ref, res_a, HIST_A, sa_prev, tm)
            set_history(sb_ref, res_b, HIST_B, sb_prev, tm)

    slot = j % 2
    chunks = [(c0, r0) for c0 in range(0, tn, LANES) for r0 in range(0, tm, CONV_RC)]

    def conv_taps(ref, res, w_ref, n_taps, hist, r0, c0):
        acc = None
        for k in range(n_taps):
            o = hist - (n_taps - 1) + k
            term = w_ref[k:k + 1, c0:c0 + LANES] * ref[res.index(o % 8), pl.ds(r0 + o - o % 8, CONV_RC), c0:c0 + LANES]
            acc = term if acc is None else acc + term
        return acc

    def conv_items(buf):
        sa_ref, sb_ref, bg_ref = buf

        def conv_a(c0, r0):
            acc = conv_taps(sa_ref, res_a, wa_ref, ka, HIST_A, r0, c0)
            ua_ref[r0:r0 + CONV_RC, c0:c0 + LANES] = acc + ba_ref[:, c0:c0 + LANES]

        def conv_b(c0, r0):
            accb = conv_taps(sb_ref, res_b, wb_ref, kb, HIST_B, r0, c0)
            pb_ref[r0:r0 + CONV_RC, c0:c0 + LANES] = (bg_ref[r0:r0 + CONV_RC, c0:c0 + LANES] * accb).astype(BF16)

        return [[functools.partial(conv_a, c0, r0), functools.partial(conv_b, c0, r0)] for c0, r0 in chunks]

    def store_shifted(ref, res, hist, m0, value):
        for q, rho in enumerate(res):
            ref[q, pl.ds(hist + m0 - rho, value.shape[0]), :] = value

    def run(todo, n):
        for _ in range(min(n, len(todo))):
            for item in todo.pop(0):
                item()

    def project(buf, todo):
        sa_ref, sb_ref, bg_ref = buf

        def proj(k, m0):
            return (jnp.dot(xbf_ref[m0:m0 + IN_HM, :], wbf_ref[slot, k], preferred_element_type=F32)
                    + b_refs[k][...])

        run(todo, 1)
        for m0 in range(0, tm, IN_HM):
            p0 = proj(0, m0)
            run(todo, 1)
            p1 = proj(1, m0)
            run(todo, 1)
            store_shifted(sa_ref, res_a, HIST_A, m0, p0 * _sigmoid(p1))
            p3 = proj(3, m0)
            run(todo, 1)
            p4 = proj(4, m0)
            run(todo, 1)
            store_shifted(sb_ref, res_b, HIST_B, m0, p3 * p4)
            bg_ref[m0:m0 + IN_HM, :] = proj(2, m0)
            run(todo, 1)
            sga_ref[m0:m0 + IN_HM, :] = _sigmoid(proj(5, m0)).astype(BF16)
            run(todo, 1)
            sgb_ref[m0:m0 + IN_HM, :] = _sigmoid(proj(6, m0)).astype(BF16)
            run(todo, 1)
        run(todo, len(todo))

    for par in range(2):
        @pl.when(jnp.logical_and(i < n_tiles, i % 2 == par))
        def _():
            project(bufs[par], conv_items(bufs[1 - par]))

    @pl.when(i == n_tiles)
    def _():
        todo = conv_items(bufs[(n_tiles - 1) % 2])
        run(todo, len(todo))

    for par in range(2):
        sa_ref, sb_ref, _ = bufs[par]

        @pl.when(jnp.logical_and(jnp.logical_and(t >= 1, t % 2 == par), seq_pos == tiles_per_seq - 1))
        def _():
            na_ref[...] = last_rows(sa_ref, res_a, HIST_A, ka)
            nb_ref[...] = last_rows(sb_ref, res_b, HIST_B, kb)


def _inproj_pipe(xp, xs_bf, w, b, wa, ba, wb, n_seq, seq, n_s, n_meta):
    rp, d = xp.shape
    ka, kb = wa.shape[0], wb.shape[0]
    tiles_per_seq = seq // IN_TM
    n_prompt_tiles = rp // IN_TM
    n_tiles = n_prompt_tiles + 1
    nj = d // IN_TN
    units_per_block = d // W_UNIT_ROWS
    n_units = N_PROJ_BLOCKS * units_per_block
    last = n_prompt_tiles - 1

    def wnext_map(j, i):
        u = jnp.where(j == nj - 1, n_units - 1, jnp.minimum(i, n_units - 1))
        col = jnp.minimum(j + 1, nj - 1)
        return (u % units_per_block, (u // units_per_block) * nj + col)

    def projected(i):
        return jnp.clip(i - 1, 0, last)

    def convolved(i):
        return jnp.clip(i - 2, 0, last)

    b_specs = [pl.BlockSpec((1, IN_TN), functools.partial(lambda j, i, k: (0, k * nj + j), k=k))
               for k in range(N_PROJ_BLOCKS)]
    ch_spec = lambda rows: pl.BlockSpec((rows, IN_TN), lambda j, i: (0, j))
    proj_spec = pl.BlockSpec((IN_TM, IN_TN), lambda j, i: (projected(i), j))
    conv_spec = pl.BlockSpec((IN_TM, IN_TN), lambda j, i: (convolved(i), j))
    state_spec = lambda rows: pl.BlockSpec((None, rows, IN_TN), lambda j, i: (convolved(i) // tiles_per_seq, 0, j))
    small_spec = pl.BlockSpec((IN_TM, IN_TN), lambda j, i: (0, j))
    sds = jax.ShapeDtypeStruct
    n_res_a, n_res_b = len(_residues(ka, HIST_A)), len(_residues(kb, HIST_B))
    assert (n_s + n_meta) % 8 == 0 and n_s + n_meta + max(HIST_A, HIST_B) <= IN_TM
    tile_bufs = [pltpu.VMEM((n_res_a, HIST_A + IN_TM, IN_TN), F32), pltpu.VMEM((n_res_b, HIST_B + IN_TM, IN_TN), F32),
                 pltpu.VMEM((IN_TM, IN_TN), F32)]
    return pl.pallas_call(
        functools.partial(_inproj_pipe_kernel, ka, kb, tiles_per_seq, n_tiles, n_s, n_meta),
        grid=(nj, n_tiles + 1),
        in_specs=[pl.BlockSpec((IN_TM, d), lambda j, i: (projected(i), 0)),
                  pl.BlockSpec((IN_TM, d), lambda j, i: (0, 0)),
                  pl.BlockSpec((W_UNIT_ROWS, IN_TN), wnext_map),
                  pl.BlockSpec(memory_space=pl.ANY)] + b_specs + [ch_spec(ka), ch_spec(1), ch_spec(kb)],
        out_specs=[conv_spec, conv_spec, proj_spec, proj_spec, state_spec(ka - 1), state_spec(kb - 1),
                   small_spec, small_spec, small_spec, small_spec, small_spec],
        out_shape=[sds((rp, d), F32), sds((rp, d), BF16), sds((rp, d), BF16), sds((rp, d), BF16),
                   sds((n_seq, ka - 1, d), F32), sds((n_seq, kb - 1, d), F32),
                   sds((IN_TM, d), F32), sds((IN_TM, d), F32), sds((IN_TM, d), F32),
                   sds((IN_TM, d), BF16), sds((IN_TM, d), BF16)],
        scratch_shapes=[pltpu.VMEM((2, N_PROJ_BLOCKS, d, IN_TN), BF16),
                        pltpu.VMEM((2, W_UNIT_ROWS, IN_TN), F32), pltpu.SemaphoreType.DMA((2,)),
                        pltpu.VMEM((IN_TM, d), BF16)] + tile_bufs + tile_bufs + [
                        pltpu.VMEM((n_res_a, HIST_A, IN_TN), F32), pltpu.VMEM((n_res_b, HIST_B, IN_TN), F32)],
        compiler_params=_cparams(2),
        name="inproj_conv",
    )(xp, xs_bf, w, w, *([b] * N_PROJ_BLOCKS), wa, ba, wb)


def _conv_sample_kernel(ka, kb, sta_ref, stb_ref, ga_ref, cb_ref, bg_ref, wa_ref, ba_ref, wb_ref,
                        ua_ref, pb_ref, nsa_ref, nsb_ref):
    ga = ga_ref[...]
    cb = cb_ref[...]
    acc = wa_ref[ka - 1:ka, :] * ga
    for k in range(ka - 1):
        acc = acc + wa_ref[k:k + 1, :] * sta_ref[:, k, :]
    ua_ref[...] = acc + ba_ref[...]
    accb = wb_ref[kb - 1:kb, :] * cb
    for k in range(kb - 1):
        accb = accb + wb_ref[k:k + 1, :] * stb_ref[:, k, :]
    pb_ref[...] = (bg_ref[...] * accb).astype(BF16)
    for k in range(ka - 2):
        nsa_ref[:, k, :] = sta_ref[:, k + 1, :]
    nsa_ref[:, ka - 2, :] = ga
    for k in range(kb - 2):
        nsb_ref[:, k, :] = stb_ref[:, k + 1, :]
    nsb_ref[:, kb - 2, :] = cb


def _conv_sample(state_a, state_b, ga, cb, bg, row_block, wa, ba, wb):
    _, n, _, d = state_a.shape
    ka, kb = wa.shape[0], wb.shape[0]
    row_spec = pl.BlockSpec((n, CONV_TC), lambda c: (row_block, c))
    out_spec = pl.BlockSpec((n, CONV_TC), lambda c: (0, c))
    state_spec = lambda rows: pl.BlockSpec((None, n, rows, CONV_TC), lambda c: (0, 0, 0, c))
    return pl.pallas_call(
        functools.partial(_conv_sample_kernel, ka, kb),
        grid=(d // CONV_TC,),
        in_specs=[state_spec(ka - 1), state_spec(kb - 1),
                  row_spec, row_spec, row_spec,
                  pl.BlockSpec((ka, CONV_TC), lambda c: (0, c)),
                  pl.BlockSpec((1, CONV_TC), lambda c: (0, c)),
                  pl.BlockSpec((kb, CONV_TC), lambda c: (0, c))],
        out_specs=[out_spec, out_spec, state_spec(ka - 1), state_spec(kb - 1)],
        out_shape=[jax.ShapeDtypeStruct((n, d), F32), jax.ShapeDtypeStruct((n, d), BF16),
                   jax.ShapeDtypeStruct(state_a.shape, F32), jax.ShapeDtypeStruct(state_b.shape, F32)],
        compiler_params=_cparams(1),
        name="conv_sample",
    )(state_a, state_b, ga, cb, bg, wa, ba, wb)


ROUTE_ID, ROUTE_RANK, ROUTE_W = 0, TOP_K, 2 * TOP_K


def _route_tile(lg, carry):
    tm = lg.shape[0]
    lane = lax.broadcasted_iota(jnp.int32, (tm, LANES), 1)
    neg_inf = jnp.float32(-jnp.inf)

    def first_max(v):
        m = jnp.max(v, axis=-1, keepdims=True)
        return m, jnp.min(jnp.where(v == m, lane, LANES), axis=-1, keepdims=True)

    g_mask = lane < N_GROUPS
    g_max, g_sel = first_max(jnp.where(g_mask, lg, neg_inf))
    g_w = 1.0 / jnp.sum(jnp.where(g_mask, jnp.exp(lg - g_max), 0.0), axis=-1, keepdims=True)
    lane0 = N_GROUPS + g_sel * EXPERTS_PER_GROUP
    e_lg = jnp.where(jnp.logical_and(lane >= lane0, lane < lane0 + EXPERTS_PER_GROUP), lg, neg_inf)
    m1, l1 = first_max(e_lg)
    m2, l2 = first_max(jnp.where(lane == l1, neg_inf, e_lg))
    r = jnp.exp(m2 - m1)
    c1 = g_w / (1.0 + r)
    c2 = g_w * r / (1.0 + r)

    a1 = lane == l1
    a2 = lane == l2
    hit = jnp.where(jnp.logical_or(a1, a2), 1.0, 0.0)
    row = lax.broadcasted_iota(jnp.int32, (tm, tm), 0)
    col = lax.broadcasted_iota(jnp.int32, (tm, tm), 1)
    before = jnp.where(col < row, 1.0, 0.0).astype(BF16)
    seen = jnp.dot(before, hit.astype(BF16), preferred_element_type=F32) + carry
    rank1 = jnp.sum(jnp.where(a1, seen, 0.0), axis=-1, keepdims=True)
    rank2 = jnp.sum(jnp.where(a2, seen, 0.0), axis=-1, keepdims=True)
    carry = carry + jnp.sum(hit, axis=0, keepdims=True)

    rec = jnp.zeros((tm, LANES), F32)
    fields = [(l1 - N_GROUPS).astype(F32), (l2 - N_GROUPS).astype(F32), rank1, rank2, c1, c2]
    for n, v in enumerate(fields):
        rec = jnp.where(lane == n, v, rec)
    return rec, carry


def _mixer_kernel(alpha, n_tiles, ua_ref, pb_ref, sga_ref, sgb_ref, x_ref, wa_hbm, wb_hbm, wo_hbm,
                  lnag_ref, lnab_ref, ln1g_ref, ln1b_ref, wrf_ref, br_ref, *refs):
    x1_ref, rt_ref, ri_ref, cnt_ref = refs[-11:-7]
    carry_ref, wa_ref, wb_ref, wo_ref, wr_ref, stage_ref, sem_ref = refs[-7:]
    i = pl.program_id(0)

    @pl.when(i == 0)
    def _():
        carry_ref[...] = jnp.zeros_like(carry_ref)
        rows = stage_ref.shape[1]
        jobs = [(src, dst, r0) for src, dst in ((wa_hbm, wa_ref), (wb_hbm, wb_ref), (wo_hbm, wo_ref))
                for r0 in range(0, src.shape[0], rows)]

        def chunk_copy(n):
            src, _, r0 = jobs[n]
            return pltpu.make_async_copy(src.at[pl.ds(r0, rows)], stage_ref.at[n % 2], sem_ref.at[n % 2])

        chunk_copy(0).start()
        for n, (_, dst, r0) in enumerate(jobs):
            if n + 1 < len(jobs):
                chunk_copy(n + 1).start()
            chunk_copy(n).wait()
            dst[r0:r0 + rows, :] = stage_ref[n % 2].astype(BF16)
        wr = wrf_ref[...]
        wr_hi = wr.astype(BF16)
        wr_ref[:, 0:LANES] = wr_hi
        wr_ref[:, LANES:2 * LANES] = (wr - wr_hi.astype(F32)).astype(BF16)

    def tile(rows, ua_t, pb_t, sga_t, sgb_t, x_t):
        un = _layer_norm(ua_t[...], lnag_ref[...], lnab_ref[...])
        act = (un * _sigmoid(un)).astype(BF16)
        ya = jnp.dot(act, wa_ref[...], preferred_element_type=F32)
        yb = jnp.dot(pb_t[...], wb_ref[...], preferred_element_type=F32)
        m = (sga_t[...].astype(F32) * ya + sgb_t[...].astype(F32) * yb).astype(BF16)
        mixed = jnp.dot(m, wo_ref[...], preferred_element_type=F32)
        x1 = _layer_norm(alpha * x_t[...] + mixed, ln1g_ref[...], ln1b_ref[...])
        x1_ref[0:rows, :] = x1
        hi = x1.astype(BF16)
        lo = (x1 - hi.astype(F32)).astype(BF16)
        a = jnp.dot(hi, wr_ref[...], preferred_element_type=F32)
        b = jnp.dot(lo, wr_ref[...], preferred_element_type=F32)
        lg = a[:, :LANES] + a[:, LANES:] + b[:, :LANES] + br_ref[...]
        rec, carry = _route_tile(lg, carry_ref[...])
        carry_ref[...] = carry
        rt_ref[0:rows, :] = rec
        ri_ref[:, 0:rows] = rec.T[0:2 * TOP_K, :].astype(jnp.int32)

    @pl.when(i < n_tiles)
    def _():
        tile(x_ref.shape[0], ua_ref, pb_ref, sga_ref, sgb_ref, x_ref)

    if len(refs) == 16:
        tail_refs = refs[:5]

        @pl.when(i == n_tiles)
        def _():
            tile(tail_refs[0].shape[0], *tail_refs)

    cnt_ref[...] = carry_ref[...]


def _mixer(tm, n_tiles, alpha, ua, pb, sga, sgb, x, wa, wb, wo, lnag, lnab, ln1g, ln1b, wr, br, tail=None):
    d = x.shape[1]
    last = n_tiles - 1
    in_spec = pl.BlockSpec((tm, d), lambda i: (jnp.minimum(i, last), 0))
    vec_spec = pl.BlockSpec((1, d), lambda i: (0, 0))
    lane_spec = pl.BlockSpec((1, LANES), lambda i: (0, 0))
    w_spec = pl.BlockSpec(memory_space=pl.ANY)
    in_specs = [in_spec, in_spec, in_spec, in_spec, in_spec, w_spec, w_spec, w_spec,
                vec_spec, vec_spec, vec_spec, vec_spec,
                pl.BlockSpec((d, LANES), lambda i: (0, 0)), lane_spec]
    args = [ua, pb, sga, sgb, x, wa, wb, wo, lnag, lnab, ln1g, ln1b, wr, br]
    n_rows, n_steps, n_tail = n_tiles * tm, n_tiles, 0
    if tail is not None:
        n_tail = tail[0].shape[0]
        assert n_tail <= tm and all(a.shape[0] >= n_tail for a in tail)
        in_specs += [pl.BlockSpec((n_tail, d), lambda i: (0, 0), pipeline_mode=pl.Buffered(1))] * len(tail)
        args += list(tail)
        n_rows, n_steps = n_rows + n_tail, n_steps + 1
    return pl.pallas_call(
        functools.partial(_mixer_kernel, alpha, n_tiles),
        grid=(n_steps,),
        in_specs=in_specs,
        out_specs=[pl.BlockSpec((tm, d), lambda i: (i, 0)), pl.BlockSpec((tm, LANES), lambda i: (i, 0)),
                   pl.BlockSpec((2 * TOP_K, tm), lambda i: (0, i)), lane_spec],
        out_shape=[jax.ShapeDtypeStruct((n_rows, d), F32), jax.ShapeDtypeStruct((n_rows, LANES), F32),
                   jax.ShapeDtypeStruct((2 * TOP_K, n_rows), jnp.int32), jax.ShapeDtypeStruct((1, LANES), F32)],
        scratch_shapes=[pltpu.VMEM((1, LANES), F32), pltpu.VMEM((d, d), BF16), pltpu.VMEM((d, d), BF16),
                        pltpu.VMEM((d, d), BF16), pltpu.VMEM((d, 2 * LANES), BF16),
                        pltpu.VMEM((2, MIX_WROWS, d), F32), pltpu.SemaphoreType.DMA((2,))],
        compiler_params=_cparams(1, 3 * d * d * 2 + 2 * MIX_WROWS * d * 4 + (2 * 18 + 12) * tm * d
                                 + 18 * n_tail * d + (2 << 20)),
        name="mixer",
    )(*args)


def _plan_kernel(n_tok, n_blocks, *refs):
    id_refs = refs[0:TOP_K]
    rank_refs = refs[TOP_K:2 * TOP_K]
    cnt_ref, be_ref, tok_ref, dst_ref, nu_ref, start_ref = refs[2 * TOP_K:]
    shift = MOE_BM.bit_length() - 1

    def per_expert(e, blk0):
        cnt = cnt_ref[0, N_GROUPS + e]
        nb = lax.shift_right_logical(cnt + (MOE_BM - 1), shift)
        start_ref[e] = blk0 * MOE_BM

        def fill(j, carry):
            be_ref[blk0 + j] = e
            return carry

        lax.fori_loop(0, nb, fill, 0)

        def pad(s, carry):
            tok_ref[s] = 0
            return carry

        lax.fori_loop(blk0 * MOE_BM + cnt, (blk0 + nb) * MOE_BM, pad, 0)
        return blk0 + nb

    n_used = lax.fori_loop(0, N_EXPERTS, per_expert, 0)
    nu_ref[0] = n_used

    def rest(b, carry):
        be_ref[b] = N_EXPERTS - 1

        def pad(s, c):
            tok_ref[b * MOE_BM + s] = 0
            return c

        lax.fori_loop(0, MOE_BM, pad, 0, unroll=8)
        return carry

    lax.fori_loop(n_used, n_blocks, rest, 0)

    def place(t, carry):
        for k in range(TOP_K):
            slot = start_ref[id_refs[k][t]] + rank_refs[k][t]
            dst_ref[k * n_tok + t] = slot
            tok_ref[slot] = t
        return carry

    lax.fori_loop(0, n_tok, place, 0, unroll=8)


def _plan(ids, ranks, cnt):
    n_tok = ids[0].shape[0]
    n_blocks = -(-n_tok * TOP_K // MOE_BM) + N_EXPERTS
    smem = pl.BlockSpec(memory_space=pltpu.SMEM)
    return pl.pallas_call(
        functools.partial(_plan_kernel, n_tok, n_blocks),
        in_specs=[smem] * (2 * TOP_K + 1),
        out_specs=[smem, smem, smem, smem],
        out_shape=[jax.ShapeDtypeStruct((n_blocks,), jnp.int32), jax.ShapeDtypeStruct((n_blocks * MOE_BM,), jnp.int32),
                   jax.ShapeDtypeStruct((TOP_K * n_tok,), jnp.int32), jax.ShapeDtypeStruct((1,), jnp.int32)],
        scratch_shapes=[pltpu.SMEM((N_EXPERTS,), jnp.int32)],
        name="plan",
    )(*ids, *ranks, cnt)


def _moe_kernel(be_ref, tok_ref, nused_ref, cnt_ref, x_hbm, wg_hbm, wu_hbm, wd_hbm, ys_ref,
                xbuf_ref, gsem_ref, wgf_ref, wuf_ref, wdf_ref, wsem_ref, wgu_ref, wdb_ref, ord_ref):
    b = pl.program_id(0)
    n_used = nused_ref[0]
    de2 = wgu_ref.shape[1]
    de = de2 // 2
    shift = MOE_BM.bit_length() - 1

    def weight_copies(e, slot):
        return [pltpu.make_async_copy(wg_hbm.at[e], wgf_ref.at[slot], wsem_ref.at[slot]),
                pltpu.make_async_copy(wu_hbm.at[e], wuf_ref.at[slot], wsem_ref.at[slot]),
                pltpu.make_async_copy(wd_hbm.at[e], wdf_ref.at[slot], wsem_ref.at[slot])]

    def row_copy(blk, slot, r):
        return pltpu.make_async_copy(x_hbm.at[pl.ds(tok_ref[blk * MOE_BM + r], 1)],
                                     xbuf_ref.at[slot, pl.ds(r, 1)], gsem_ref.at[slot])

    @pl.when(jnp.logical_and(b == 0, n_used > 0))
    def _():
        ord_ref[0] = 0
        for cp in weight_copies(be_ref[0], 0):
            cp.start(priority=WEIGHT_DMA_PRIORITY)
        for ahead in range(MOE_AHEAD):
            for r in range(MOE_BM):
                row_copy(jnp.minimum(ahead, n_used - 1), ahead, r).start()

    @pl.when(b < n_used)
    def _():
        e = be_ref[b]

        @pl.when(jnp.logical_or(b == 0, e != be_ref[jnp.maximum(b - 1, 0)]))
        def _():
            order = ord_ref[0]
            wslot = order % 2
            for cp in weight_copies(e, wslot):
                cp.wait()
            nxt = b + lax.shift_right_logical(cnt_ref[0, N_GROUPS + e] + (MOE_BM - 1), shift)

            @pl.when(nxt < n_used)
            def _():
                for cp in weight_copies(be_ref[jnp.minimum(nxt, be_ref.shape[0] - 1)], 1 - wslot):
                    cp.start(priority=WEIGHT_DMA_PRIORITY)

            wgu_ref[:, 0:de] = wgf_ref[wslot].astype(BF16)
            wgu_ref[:, de:de2] = wuf_ref[wslot].astype(BF16)
            wdb_ref[...] = wdf_ref[wslot].astype(BF16)
            ord_ref[0] = order + 1

        n_buf = MOE_AHEAD + 1
        slot = b % n_buf
        for r in range(MOE_BM):
            row_copy(b, slot, r).wait()
        nxt_blk = jnp.minimum(b + MOE_AHEAD, n_used - 1)
        nxt_slot = (b + MOE_AHEAD) % n_buf
        xb = xbuf_ref[slot].astype(BF16)
        n_chunks = de2 // MOE_NC
        per = MOE_BM // n_chunks
        gu = []
        for c in range(n_chunks):
            for r in range(c * per, (c + 1) * per):
                row_copy(nxt_blk, nxt_slot, r).start()
            gu.append(jnp.dot(xb, wgu_ref[:, c * MOE_NC:(c + 1) * MOE_NC], preferred_element_type=F32))
        half = n_chunks // 2
        y = None
        for c in range(half):
            g = gu[c]
            h = (g * _sigmoid(g) * gu[half + c]).astype(BF16)
            part = jnp.dot(h, wdb_ref[c * MOE_NC:(c + 1) * MOE_NC, :], preferred_element_type=F32)
            y = part if y is None else y + part
        ys_ref[...] = y

    @pl.when(b == n_used - 1)
    def _():
        for ahead in range(1, MOE_AHEAD + 1):
            for r in range(MOE_BM):
                row_copy(b, (b + ahead) % (MOE_AHEAD + 1), r).wait()

    @pl.when(b >= n_used)
    def _():
        ys_ref[...] = jnp.zeros_like(ys_ref)


def _moe(block_expert, slot_tok, n_used, cnt, x1, wg, wu, wd):
    n_blocks = block_expert.shape[0]
    n_slots = slot_tok.shape[0]
    _, d, de = wg.shape
    any_spec = pl.BlockSpec(memory_space=pl.ANY)
    grid_spec = pltpu.PrefetchScalarGridSpec(
        num_scalar_prefetch=4,
        grid=(n_blocks,),
        in_specs=[any_spec, any_spec, any_spec, any_spec],
        out_specs=pl.BlockSpec((MOE_BM, d), lambda b, *_: (b, 0)),
        scratch_shapes=[pltpu.VMEM((MOE_AHEAD + 1, MOE_BM, d), F32), pltpu.SemaphoreType.DMA((MOE_AHEAD + 1,)),
                        pltpu.VMEM((2, d, de), F32), pltpu.VMEM((2, d, de), F32), pltpu.VMEM((2, de, d), F32),
                        pltpu.SemaphoreType.DMA((2,)),
                        pltpu.VMEM((d, 2 * de), BF16), pltpu.VMEM((de, d), BF16), pltpu.SMEM((1,), jnp.int32)],
    )
    return pl.pallas_call(
        _moe_kernel,
        grid_spec=grid_spec,
        out_shape=jax.ShapeDtypeStruct((n_slots, d), F32),
        compiler_params=_cparams(1),
        name="moe_ffn",
    )(block_expert, slot_tok, n_used, cnt, x1, wg, wu, wd)


def _combine_kernel(alpha, tile_off, n_tok, dst_ref, ys_hbm, x1_ref, rt_ref, g_ref, b_ref, out_ref,
                    buf_ref, sem_ref):
    i = pl.program_id(0)
    n = pl.num_programs(0)
    tm = x1_ref.shape[0]

    def row_copy(tile, slot, r, k):
        src = dst_ref[k * n_tok + (tile + tile_off) * tm + r]
        return pltpu.make_async_copy(ys_hbm.at[pl.ds(src, 1)], buf_ref.at[slot, k, pl.ds(r, 1)], sem_ref.at[slot])

    n_buf = CMB_AHEAD + 1

    @pl.when(i == 0)
    def _():
        for ahead in range(CMB_AHEAD):
            for r in range(tm):
                for k in range(TOP_K):
                    row_copy(jnp.minimum(ahead, n - 1), ahead, r, k).start()

    slot = i % n_buf
    for r in range(tm):
        for k in range(TOP_K):
            row_copy(i, slot, r, k).wait()
    nxt = jnp.minimum(i + CMB_AHEAD, n - 1)
    nxt_slot = (i + CMB_AHEAD) % n_buf
    for r0 in range(0, tm, CMB_RC):
        for r in range(r0, r0 + CMB_RC):
            for k in range(TOP_K):
                row_copy(nxt, nxt_slot, r, k).start()
        rt = rt_ref[r0:r0 + CMB_RC, :]
        f = (rt[:, ROUTE_W:ROUTE_W + 1] * buf_ref[slot, 0, r0:r0 + CMB_RC, :]
             + rt[:, ROUTE_W + 1:ROUTE_W + 2] * buf_ref[slot, 1, r0:r0 + CMB_RC, :])
        out_ref[r0:r0 + CMB_RC, :] = _layer_norm(alpha * x1_ref[r0:r0 + CMB_RC, :] + f, g_ref[...], b_ref[...])

    @pl.when(i == n - 1)
    def _():
        for ahead in range(1, n_buf):
            for r in range(tm):
                for k in range(TOP_K):
                    row_copy(i, (i + ahead) % n_buf, r, k).wait()


def _combine(tm, n_tiles, tile_off, alpha, dst, ys, x1, rt, g, b):
    n_tok, d = x1.shape
    grid_spec = pltpu.PrefetchScalarGridSpec(
        num_scalar_prefetch=1,
        grid=(n_tiles,),
        in_specs=[pl.BlockSpec(memory_space=pl.ANY),
                  pl.BlockSpec((tm, d), lambda i, *_: (i + tile_off, 0)),
                  pl.BlockSpec((tm, LANES), lambda i, *_: (i + tile_off, 0)),
                  pl.BlockSpec((1, d), lambda i, *_: (0, 0)),
                  pl.BlockSpec((1, d), lambda i, *_: (0, 0))],
        out_specs=pl.BlockSpec((tm, d), lambda i, *_: (i, 0)),
        scratch_shapes=[pltpu.VMEM((CMB_AHEAD + 1, TOP_K, tm, d), F32), pltpu.SemaphoreType.DMA((CMB_AHEAD + 1,))],
    )
    return pl.pallas_call(
        functools.partial(_combine_kernel, alpha, tile_off, n_tok),
        grid_spec=grid_spec,
        out_shape=jax.ShapeDtypeStruct((n_tiles * tm, d), F32),
        compiler_params=_cparams(1),
        name="combine",
    )(dst, ys, x1, rt, g, b)


def kernel(x_prompt, x_sample, state_conv_a, state_conv_b, meta_tokens, w_in, b_in, conv_a_w, conv_a_b, ln_a_g, ln_a_b, w_a_out, conv_b_w, w_b_out, w_o, ln1_g, ln1_b, w_router_group, b_router_group, w_router_expert, b_router_expert, w_exp_gate, w_exp_up, w_exp_down, ln2_g, ln2_b):
    depth = w_in.shape[0]
    assert depth == 1, "single-layer step only"
    n_seq, seq, d = x_prompt.shape
    n_s = x_sample.shape[0]
    n_meta = meta_tokens.shape[0]
    ka, kb = conv_a_w.shape[1], conv_b_w.shape[1]
    assert x_sample.shape[1] == 1 and seq % IN_TM == 0 and IN_TM >= ka - 1
    assert n_meta <= HIST_A and ka - 1 <= HIST_A and kb - 1 <= HIST_B and kb - 1 <= n_meta
    assert n_s + n_meta <= IN_TM and n_s % CMB_TM == 0
    alpha = (2.0 * depth) ** 0.25
    rp = n_seq * seq

    xp = x_prompt.reshape(rp, d)
    xs = jnp.concatenate([x_sample.reshape(n_s, d), meta_tokens,
                          jnp.zeros((IN_TM - n_s - n_meta, d), F32)], axis=0).astype(BF16)
    (ua_p, pb_p, sga_p, sgb_p, new_a_p, new_b_p, ga_x, cb_x, bg_x, sga_x, sgb_x) = _inproj_pipe(
        xp, xs, w_in[0], b_in, conv_a_w[0], conv_a_b, conv_b_w[0], n_seq, seq, n_s, n_meta)
    ua_s, pb_s, new_a_s, new_b_s = _conv_sample(state_conv_a, state_conv_b, ga_x, cb_x, bg_x, 0,
                                                conv_a_w[0], conv_a_b, conv_b_w[0])

    t = rp + n_s
    wr_f = jnp.concatenate([w_router_group[0], w_router_expert[0].transpose(1, 0, 2).reshape(d, N_EXPERTS),
                            jnp.zeros((d, LANES - N_GROUPS - N_EXPERTS), F32)], axis=1)
    br = jnp.concatenate([b_router_group[0], b_router_expert[0].reshape(-1),
                          jnp.zeros((LANES - N_GROUPS - N_EXPERTS,), F32)])[None, :]
    x1, rt, ri, cnt = _mixer(MIX_TM, rp // MIX_TM, alpha, ua_p, pb_p, sga_p, sgb_p, xp, w_a_out[0], w_b_out[0], w_o[0],
                             ln_a_g, ln_a_b, ln1_g, ln1_b, wr_f, br,
                             tail=(ua_s, pb_s, sga_x, sgb_x, x_sample.reshape(n_s, d)))

    cnt_i = cnt.astype(jnp.int32)
    ids = [ri[ROUTE_ID + k] for k in range(TOP_K)]
    ranks = [ri[ROUTE_RANK + k] for k in range(TOP_K)]
    block_expert, slot_tok, dest, n_used = _plan(ids, ranks, cnt_i)
    ys = _moe(block_expert, slot_tok, n_used, cnt_i, x1, w_exp_gate[0], w_exp_up[0], w_exp_down[0])
    y_p = _combine(CMB_TM, rp // CMB_TM, 0, alpha, dest, ys, x1, rt, ln2_g, ln2_b)
    y_s = _combine(CMB_TM, n_s // CMB_TM, rp // CMB_TM, alpha, dest, ys, x1, rt, ln2_g, ln2_b)

    return (y_p.reshape(n_seq, seq, d), y_s.reshape(n_s, 1, d), new_a_p[None], new_b_p[None], new_a_s, new_b_s)
```

```python
import functools

import jax
import jax.numpy as jnp
from jax import lax
from jax.experimental import pallas as pl
from jax.experimental.pallas import tpu as pltpu

F32 = jnp.float32
BF16 = jnp.bfloat16

LN_EPS = 1e-5
N_GROUPS = 4
EXPERTS_PER_GROUP = 8
N_EXPERTS = N_GROUPS * EXPERTS_PER_GROUP
TOP_K = 2
N_PROJ_BLOCKS = 7

VMEM_LIMIT_BYTES = 56 * 1024 * 1024
LANES = 128

IN_TM = 512
IN_TN = 256
IN_HM = 256
W_UNIT_ROWS = 1024
CONV_TC = 256
CONV_RC = 64
HIST_A = 32
HIST_B = 8
MIX_TM = 256
MIX_WROWS = 128
MOE_BM = 128
MOE_NC = 256
MOE_AHEAD = 4
WEIGHT_DMA_PRIORITY = 1
CMB_TM = 128
CMB_RC = 32
CMB_AHEAD = 3


def _cparams(n_axes, vmem_bytes=None):
    limit = VMEM_LIMIT_BYTES if vmem_bytes is None else min(int(vmem_bytes), VMEM_LIMIT_BYTES)
    return pltpu.CompilerParams(dimension_semantics=("arbitrary",) * n_axes, vmem_limit_bytes=limit)


def _sigmoid(x):
    return 1.0 / (1.0 + jnp.exp(-x))


def _layer_norm(x, g, b):
    mu = jnp.mean(x, axis=-1, keepdims=True)
    xc = x - mu
    var = jnp.mean(xc * xc, axis=-1, keepdims=True)
    return xc * lax.rsqrt(var + LN_EPS) * g + b


def _conv_taps(src_ref, w_ref, n_taps, hist, r0, rc, c0):
    base = hist - (n_taps - 1)
    acc = None
    for rho in range(8):
        offs = [o for o in range(base, base + n_taps) if o % 8 == rho]
        if not offs:
            continue
        lo = offs[0]
        x = src_ref[pl.ds(r0 + lo, rc + offs[-1] - lo), c0:c0 + LANES]
        for o in offs:
            term = w_ref[o - base:o - base + 1, c0:c0 + LANES] * x[o - lo:o - lo + rc]
            acc = term if acc is None else acc + term
    return acc


def _inproj_conv_kernel(ka, kb, tiles_per_seq, n_s, n_meta, xp_ref, xs_ref, wnext_ref, w_hbm, *refs):
    b_refs = refs[0:7]
    wa_ref, ba_ref, wb_ref = refs[7:10]
    ua_ref, pb_ref, sga_ref, sgb_ref, na_ref, nb_ref = refs[10:16]
    gas_ref, cbs_ref, bgs_ref, sgas_ref, sgbs_ref = refs[16:21]
    wbf_ref, stage_ref, sem_ref, xbf_ref, sa_ref, sb_ref, bg_ref, ha_ref, hb_ref = refs[21:30]
    j = pl.program_id(0)
    i = pl.program_id(1)
    nj = pl.num_programs(0)
    tm, d = xbf_ref.shape
    tn = sa_ref.shape[1]
    unit_rows = stage_ref.shape[1]
    units_per_block = d // unit_rows
    n_units = N_PROJ_BLOCKS * units_per_block

    @pl.when(jnp.logical_and(j == 0, i == 0))
    def _():
        def unit_copy(u):
            k, h = divmod(u, units_per_block)
            return pltpu.make_async_copy(
                w_hbm.at[pl.ds(h * unit_rows, unit_rows), pl.ds(k * d, tn)], stage_ref.at[u % 2], sem_ref.at[u % 2])

        unit_copy(0).start()
        for u in range(n_units):
            if u + 1 < n_units:
                unit_copy(u + 1).start()
            unit_copy(u).wait()
            k, h = divmod(u, units_per_block)
            wbf_ref[0, k, h * unit_rows:(h + 1) * unit_rows, :] = stage_ref[u % 2].astype(BF16)

    @pl.when(jnp.logical_and(i < n_units, j + 1 < nj))
    def _():
        k = i // units_per_block
        h = i % units_per_block
        row0 = pl.multiple_of(h * unit_rows, unit_rows)
        wbf_ref[(j + 1) % 2, k, pl.ds(row0, unit_rows), :] = wnext_ref[...].astype(BF16)

    @pl.when(i == 0)
    def _():
        xbf_ref[...] = xs_ref[...]

    @pl.when(i > 0)
    def _():
        xbf_ref[...] = xp_ref[...].astype(BF16)

    seq_pos = (i - 1) % tiles_per_seq

    @pl.when(jnp.logical_and(i > 0, seq_pos == 0))
    def _():
        sa_ref[0:HIST_A, :] = ha_ref[...]
        sb_ref[0:HIST_B, :] = hb_ref[...]

    slot = j % 2
    halves = list(range(0, tm, IN_HM))

    def proj(k, m0):
        return (jnp.dot(xbf_ref[m0:m0 + IN_HM, :], wbf_ref[slot, k], preferred_element_type=F32)
                + b_refs[k][...])

    chunks = [(c0, r0) for c0 in range(0, tn, LANES) for r0 in range(0, tm, CONV_RC)]

    def conv_a(c0, r0):
        acc = _conv_taps(sa_ref, wa_ref, ka, HIST_A, r0, CONV_RC, c0)
        ua_ref[r0:r0 + CONV_RC, c0:c0 + LANES] = acc + ba_ref[:, c0:c0 + LANES]

    def conv_b(c0, r0):
        accb = _conv_taps(sb_ref, wb_ref, kb, HIST_B, r0, CONV_RC, c0)
        pb_ref[r0:r0 + CONV_RC, c0:c0 + LANES] = (bg_ref[r0:r0 + CONV_RC, c0:c0 + LANES] * accb).astype(BF16)

    todo_a = [functools.partial(conv_a, c0, r0) for c0, r0 in chunks]
    todo_b = [functools.partial(conv_b, c0, r0) for c0, r0 in chunks]

    def run(todo, n):
        for _ in range(min(n, len(todo))):
            todo.pop(0)()

    for m0 in halves:
        sa_ref[HIST_A + m0:HIST_A + m0 + IN_HM, :] = proj(0, m0) * _sigmoid(proj(1, m0))
    for m0 in halves:
        p3 = proj(3, m0)
        run(todo_a, 2)
        p4 = proj(4, m0)
        run(todo_a, 2)
        sb_ref[HIST_B + m0:HIST_B + m0 + IN_HM, :] = p3 * p4
    for m0 in halves:
        bg_ref[m0:m0 + IN_HM, :] = proj(2, m0)
        run(todo_a, 2)
    for m0 in halves:
        sga_ref[m0:m0 + IN_HM, :] = _sigmoid(proj(5, m0)).astype(BF16)
        run(todo_a, 1)
        run(todo_b, 4)
    for m0 in halves:
        sgb_ref[m0:m0 + IN_HM, :] = _sigmoid(proj(6, m0)).astype(BF16)
        run(todo_a, 1)
        run(todo_b, 4)
    run(todo_a, len(todo_a))
    run(todo_b, len(todo_b))

    @pl.when(jnp.logical_and(i > 0, seq_pos == tiles_per_seq - 1))
    def _():
        na_ref[...] = sa_ref[HIST_A + tm - (ka - 1):HIST_A + tm, :]
        nb_ref[...] = sb_ref[HIST_B + tm - (kb - 1):HIST_B + tm, :]

    @pl.when(i == 0)
    def _():
        gas_ref[...] = sa_ref[HIST_A:HIST_A + tm, :]
        cbs_ref[...] = sb_ref[HIST_B:HIST_B + tm, :]
        bgs_ref[...] = bg_ref[...]
        sgas_ref[...] = sga_ref[...]
        sgbs_ref[...] = sgb_ref[...]
        ma, mb = min(n_meta, HIST_A), min(n_meta, HIST_B)
        meta_end = n_s + n_meta
        if ma < HIST_A:
            ha_ref[0:HIST_A - ma, :] = jnp.zeros((HIST_A - ma, tn), F32)
        if mb < HIST_B:
            hb_ref[0:HIST_B - mb, :] = jnp.zeros((HIST_B - mb, tn), F32)
        ha_ref[HIST_A - ma:HIST_A, :] = sa_ref[HIST_A + meta_end - ma:HIST_A + meta_end, :]
        hb_ref[HIST_B - mb:HIST_B, :] = sb_ref[HIST_B + meta_end - mb:HIST_B + meta_end, :]

    sa_ref[0:HIST_A, :] = sa_ref[tm:tm + HIST_A, :]
    sb_ref[0:HIST_B, :] = sb_ref[tm:tm + HIST_B, :]


def _inproj_conv(xp, xs_bf, w, b, wa, ba, wb, n_seq, seq, n_s, n_meta):
    rp, d = xp.shape
    ka, kb = wa.shape[0], wb.shape[0]
    tiles_per_seq = seq // IN_TM
    n_prompt_tiles = rp // IN_TM
    nj = d // IN_TN
    units_per_block = d // W_UNIT_ROWS
    n_units = N_PROJ_BLOCKS * units_per_block

    def wnext_map(j, i):
        u = jnp.where(j == nj - 1, n_units - 1, jnp.minimum(i, n_units - 1))
        col = jnp.minimum(j + 1, nj - 1)
        return (u % units_per_block, (u // units_per_block) * nj + col)

    def prow(j, i):
        return (jnp.maximum(i - 1, 0), j)

    b_specs = [pl.BlockSpec((1, IN_TN), functools.partial(lambda j, i, k: (0, k * nj + j), k=k))
               for k in range(N_PROJ_BLOCKS)]
    ch_spec = lambda rows: pl.BlockSpec((rows, IN_TN), lambda j, i: (0, j))
    prompt_spec = pl.BlockSpec((IN_TM, IN_TN), prow)
    state_spec = lambda rows: pl.BlockSpec((None, rows, IN_TN),
                                           lambda j, i: (jnp.maximum(i - 1, 0) // tiles_per_seq, 0, j))
    small_spec = pl.BlockSpec((IN_TM, IN_TN), lambda j, i: (0, j))
    sds = jax.ShapeDtypeStruct
    return pl.pallas_call(
        functools.partial(_inproj_conv_kernel, ka, kb, tiles_per_seq, n_s, n_meta),
        grid=(nj, n_prompt_tiles + 1),
        in_specs=[pl.BlockSpec((IN_TM, d), lambda j, i: (jnp.maximum(i - 1, 0), 0)),
                  pl.BlockSpec((IN_TM, d), lambda j, i: (0, 0)),
                  pl.BlockSpec((W_UNIT_ROWS, IN_TN), wnext_map),
                  pl.BlockSpec(memory_space=pl.ANY)] + b_specs + [ch_spec(ka), ch_spec(1), ch_spec(kb)],
        out_specs=[prompt_spec, prompt_spec, prompt_spec, prompt_spec, state_spec(ka - 1), state_spec(kb - 1),
                   small_spec, small_spec, small_spec, small_spec, small_spec],
        out_shape=[sds((rp, d), F32), sds((rp, d), BF16), sds((rp, d), BF16), sds((rp, d), BF16),
                   sds((n_seq, ka - 1, d), F32), sds((n_seq, kb - 1, d), F32),
                   sds((IN_TM, d), F32), sds((IN_TM, d), F32), sds((IN_TM, d), F32),
                   sds((IN_TM, d), BF16), sds((IN_TM, d), BF16)],
        scratch_shapes=[pltpu.VMEM((2, N_PROJ_BLOCKS, d, IN_TN), BF16),
                        pltpu.VMEM((2, W_UNIT_ROWS, IN_TN), F32), pltpu.SemaphoreType.DMA((2,)),
                        pltpu.VMEM((IN_TM, d), BF16),
                        pltpu.VMEM((HIST_A + IN_TM, IN_TN), F32), pltpu.VMEM((HIST_B + IN_TM, IN_TN), F32),
                        pltpu.VMEM((IN_TM, IN_TN), F32),
                        pltpu.VMEM((HIST_A, IN_TN), F32), pltpu.VMEM((HIST_B, IN_TN), F32)],
        compiler_params=_cparams(2),
        name="inproj_conv",
    )(xp, xs_bf, w, w, *([b] * N_PROJ_BLOCKS), wa, ba, wb)


def _residues(n_taps, hist):
    return sorted({(hist - (n_taps - 1) + k) % 8 for k in range(n_taps)})


def _inproj_pipe_kernel(ka, kb, tiles_per_seq, n_tiles, n_s, n_meta, xp_ref, xs_ref, wnext_ref, w_hbm, *refs):
    b_refs = refs[0:7]
    wa_ref, ba_ref, wb_ref = refs[7:10]
    ua_ref, pb_ref, sga_ref, sgb_ref, na_ref, nb_ref = refs[10:16]
    gas_ref, cbs_ref, bgs_ref, sgas_ref, sgbs_ref = refs[16:21]
    wbf_ref, stage_ref, sem_ref, xbf_ref = refs[21:25]
    bufs = (refs[25:28], refs[28:31])
    ha_ref, hb_ref = refs[31:33]
    j = pl.program_id(0)
    i = pl.program_id(1)
    nj = pl.num_programs(0)
    tm, d = xbf_ref.shape
    tn = ha_ref.shape[-1]
    unit_rows = stage_ref.shape[1]
    units_per_block = d // unit_rows
    n_units = N_PROJ_BLOCKS * units_per_block
    t = i - 1

    @pl.when(jnp.logical_and(j == 0, i == 0))
    def _():
        def unit_copy(u):
            k, h = divmod(u, units_per_block)
            return pltpu.make_async_copy(
                w_hbm.at[pl.ds(h * unit_rows, unit_rows), pl.ds(k * d, tn)], stage_ref.at[u % 2], sem_ref.at[u % 2])

        unit_copy(0).start()
        for u in range(n_units):
            if u + 1 < n_units:
                unit_copy(u + 1).start()
            unit_copy(u).wait()
            k, h = divmod(u, units_per_block)
            wbf_ref[0, k, h * unit_rows:(h + 1) * unit_rows, :] = stage_ref[u % 2].astype(BF16)
        for buf in bufs:
            for ref in buf:
                ref[...] = jnp.zeros_like(ref)

    @pl.when(jnp.logical_and(i < n_units, j + 1 < nj))
    def _():
        k = i // units_per_block
        h = i % units_per_block
        row0 = pl.multiple_of(h * unit_rows, unit_rows)
        wbf_ref[(j + 1) % 2, k, pl.ds(row0, unit_rows), :] = wnext_ref[...].astype(BF16)

    @pl.when(i == 0)
    def _():
        xbf_ref[...] = xs_ref[...]

    @pl.when(jnp.logical_and(i > 0, i < n_tiles))
    def _():
        xbf_ref[...] = xp_ref[...].astype(BF16)

    res_a, res_b = _residues(ka, HIST_A), _residues(kb, HIST_B)
    meta_end = n_s + n_meta

    def raw_rows(ref, res, hist):
        return ref[res.index(0), hist:hist + tm, :]

    def last_rows(ref, res, hist, n_taps):
        rho = (hist - (n_taps - 1)) % 8
        start = hist + tm - (n_taps - 1) - rho
        return ref[res.index(rho), start:start + n_taps - 1, :]

    def set_history(ref, res, hist, src, row0):
        for q, rho in enumerate(res):
            if hist - rho > 0:
                ref[q, 0:hist - rho, :] = src[q, row0:row0 + hist - rho, :]

    @pl.when(i == 1)
    def _():
        sa_ref, sb_ref, bg_ref = bufs[0]
        gas_ref[...] = raw_rows(sa_ref, res_a, HIST_A)
        cbs_ref[...] = raw_rows(sb_ref, res_b, HIST_B)
        bgs_ref[...] = bg_ref[...]
        sgas_ref[...] = sga_ref[...]
        sgbs_ref[...] = sgb_ref[...]
        for h_ref, src, res, hist in ((ha_ref, sa_ref, res_a, HIST_A), (hb_ref, sb_ref, res_b, HIST_B)):
            for q, rho in enumerate(res):
                h_ref[q] = src[q, meta_end:meta_end + hist, :]
                n_zero = hist - n_meta - rho
                if n_zero > 0:
                    h_ref[q, 0:n_zero, :] = jnp.zeros((n_zero, tn), F32)

    seq_pos = (t - 1) % tiles_per_seq
    for par in range(2):
        sa_ref, sb_ref, _ = bufs[par]
        sa_prev, sb_prev, _ = bufs[1 - par]
        is_t = jnp.logical_and(t >= 1, t % 2 == par)

        @pl.when(jnp.logical_and(is_t, seq_pos == 0))
        def _():
            set_history(sa_ref, res_a, HIST_A, ha_ref, 0)
            set_history(sb_ref, res_b, HIST_B, hb_ref, 0)

        @pl.when(jnp.logical_and(is_t, seq_pos != 0))
        def _():
            set_history(sa_ref, res_a, HIST_A, sa_prev, tm)
            set_history(sb_ref, res_b, HIST_B, sb_prev, tm)

    slot = j % 2
    chunks = [(c0, r0) for c0 in range(0, tn, LANES) for r0 in range(0, tm, CONV_RC)]

    def conv_taps(ref, res, w_ref, n_taps, hist, r0, c0):
        acc = None
        for k in range(n_taps):
            o = hist - (n_taps - 1) + k
            term = w_ref[k:k + 1, c0:c0 + LANES] * ref[res.index(o % 8), pl.ds(r0 + o - o % 8, CONV_RC), c0:c0 + LANES]
            acc = term if acc is None else acc + term
        return acc

    def conv_items(buf):
        sa_ref, sb_ref, bg_ref = buf

        def conv_a(c0, r0):
            acc = conv_taps(sa_ref, res_a, wa_ref, ka, HIST_A, r0, c0)
            ua_ref[r0:r0 + CONV_RC, c0:c0 + LANES] = acc + ba_ref[:, c0:c0 + LANES]

        def conv_b(c0, r0):
            accb = conv_taps(sb_ref, res_b, wb_ref, kb, HIST_B, r0, c0)
            pb_ref[r0:r0 + CONV_RC, c0:c0 + LANES] = (bg_ref[r0:r0 + CONV_RC, c0:c0 + LANES] * accb).astype(BF16)

        return [[functools.partial(conv_a, c0, r0), functools.partial(conv_b, c0, r0)] for c0, r0 in chunks]

    def store_shifted(ref, res, hist, m0, value):
        for q, rho in enumerate(res):
            ref[q, pl.ds(hist + m0 - rho, value.shape[0]), :] = value

    def run(todo, n):
        for _ in range(min(n, len(todo))):
            for item in todo.pop(0):
                item()

    def project(buf, todo):
        sa_ref, sb_ref, bg_ref = buf

        def proj(k, m0):
            return (jnp.dot(xbf_ref[m0:m0 + IN_HM, :], wbf_ref[slot, k], preferred_element_type=F32)
                    + b_refs[k][...])

        run(todo, 1)
        for m0 in range(0, tm, IN_HM):
            p0 = proj(0, m0)
            run(todo, 1)
            p1 = proj(1, m0)
            run(todo, 1)
            store_shifted(sa_ref, res_a, HIST_A, m0, p0 * _sigmoid(p1))
            p3 = proj(3, m0)
            run(todo, 1)
            p4 = proj(4, m0)
            run(todo, 1)
            store_shifted(sb_ref, res_b, HIST_B, m0, p3 * p4)
            bg_ref[m0:m0 + IN_HM, :] = proj(2, m0)
            run(todo, 1)
            sga_ref[m0:m0 + IN_HM, :] = _sigmoid(proj(5, m0)).astype(BF16)
            run(todo, 1)
            sgb_ref[m0:m0 + IN_HM, :] = _sigmoid(proj(6, m0)).astype(BF16)
            run(todo, 1)
        run(todo, len(todo))

    for par in range(2):
        @pl.when(jnp.logical_and(i < n_tiles, i % 2 == par))
        def _():
            project(bufs[par], conv_items(bufs[1 - par]))

    @pl.when(i == n_tiles)
    def _():
        todo = conv_items(bufs[(n_tiles - 1) % 2])
        run(todo, len(todo))

    for par in range(2):
        sa_ref, sb_ref, _ = bufs[par]

        @pl.when(jnp.logical_and(jnp.logical_and(t >= 1, t % 2 == par), seq_pos == tiles_per_seq - 1))
        def _():
            na_ref[...] = last_rows(sa_ref, res_a, HIST_A, ka)
            nb_ref[...] = last_rows(sb_ref, res_b, HIST_B, kb)


def _inproj_pipe(xp, xs_bf, w, b, wa, ba, wb, n_seq, seq, n_s, n_meta):
    rp, d = xp.shape
    ka, kb = wa.shape[0], wb.shape[0]
    tiles_per_seq = seq // IN_TM
    n_prompt_tiles = rp // IN_TM
    n_tiles = n_prompt_tiles + 1
    nj = d // IN_TN
    units_per_block = d // W_UNIT_ROWS
    n_units = N_PROJ_BLOCKS * units_per_block
    last = n_prompt_tiles - 1

    def wnext_map(j, i):
        u = jnp.where(j == nj - 1, n_units - 1, jnp.minimum(i, n_units - 1))
        col = jnp.minimum(j + 1, nj - 1)
        return (u % units_per_block, (u // units_per_block) * nj + col)

    def projected(i):
        return jnp.clip(i - 1, 0, last)

    def convolved(i):
        return jnp.clip(i - 2, 0, last)

    b_specs = [pl.BlockSpec((1, IN_TN), functools.partial(lambda j, i, k: (0, k * nj + j), k=k))
               for k in range(N_PROJ_BLOCKS)]
    ch_spec = lambda rows: pl.BlockSpec((rows, IN_TN), lambda j, i: (0, j))
    proj_spec = pl.BlockSpec((IN_TM, IN_TN), lambda j, i: (projected(i), j))
    conv_spec = pl.BlockSpec((IN_TM, IN_TN), lambda j, i: (convolved(i), j))
    state_spec = lambda rows: pl.BlockSpec((None, rows, IN_TN), lambda j, i: (convolved(i) // tiles_per_seq, 0, j))
    small_spec = pl.BlockSpec((IN_TM, IN_TN), lambda j, i: (0, j))
    sds = jax.ShapeDtypeStruct
    n_res_a, n_res_b = len(_residues(ka, HIST_A)), len(_residues(kb, HIST_B))
    assert (n_s + n_meta) % 8 == 0 and n_s + n_meta + max(HIST_A, HIST_B) <= IN_TM
    tile_bufs = [pltpu.VMEM((n_res_a, HIST_A + IN_TM, IN_TN), F32), pltpu.VMEM((n_res_b, HIST_B + IN_TM, IN_TN), F32),
                 pltpu.VMEM((IN_TM, IN_TN), F32)]
    return pl.pallas_call(
        functools.partial(_inproj_pipe_kernel, ka, kb, tiles_per_seq, n_tiles, n_s, n_meta),
        grid=(nj, n_tiles + 1),
        in_specs=[pl.BlockSpec((IN_TM, d), lambda j, i: (projected(i), 0)),
                  pl.BlockSpec((IN_TM, d), lambda j, i: (0, 0)),
                  pl.BlockSpec((W_UNIT_ROWS, IN_TN), wnext_map),
                  pl.BlockSpec(memory_space=pl.ANY)] + b_specs + [ch_spec(ka), ch_spec(1), ch_spec(kb)],
        out_specs=[conv_spec, conv_spec, proj_spec, proj_spec, state_spec(ka - 1), state_spec(kb - 1),
                   small_spec, small_spec, small_spec, small_spec, small_spec],
        out_shape=[sds((rp, d), F32), sds((rp, d), BF16), sds((rp, d), BF16), sds((rp, d), BF16),
                   sds((n_seq, ka - 1, d), F32), sds((n_seq, kb - 1, d), F32),
                   sds((IN_TM, d), F32), sds((IN_TM, d), F32), sds((IN_TM, d), F32),
                   sds((IN_TM, d), BF16), sds((IN_TM, d), BF16)],
        scratch_shapes=[pltpu.VMEM((2, N_PROJ_BLOCKS, d, IN_TN), BF16),
                        pltpu.VMEM((2, W_UNIT_ROWS, IN_TN), F32), pltpu.SemaphoreType.DMA((2,)),
                        pltpu.VMEM((IN_TM, d), BF16)] + tile_bufs + tile_bufs + [
                        pltpu.VMEM((n_res_a, HIST_A, IN_TN), F32), pltpu.VMEM((n_res_b, HIST_B, IN_TN), F32)],
        compiler_params=_cparams(2),
        name="inproj_conv",
    )(xp, xs_bf, w, w, *([b] * N_PROJ_BLOCKS), wa, ba, wb)


def _conv_sample_kernel(ka, kb, sta_ref, stb_ref, ga_ref, cb_ref, bg_ref, wa_ref, ba_ref, wb_ref,
                        ua_ref, pb_ref, nsa_ref, nsb_ref):
    ga = ga_ref[...]
    cb = cb_ref[...]
    acc = wa_ref[ka - 1:ka, :] * ga
    for k in range(ka - 1):
        acc = acc + wa_ref[k:k + 1, :] * sta_ref[k]
    ua_ref[...] = acc + ba_ref[...]
    accb = wb_ref[kb - 1:kb, :] * cb
    for k in range(kb - 1):
        accb = accb + wb_ref[k:k + 1, :] * stb_ref[k]
    pb_ref[...] = (bg_ref[...] * accb).astype(BF16)
    for k in range(ka - 2):
        nsa_ref[k] = sta_ref[k + 1]
    nsa_ref[ka - 2] = ga
    for k in range(kb - 2):
        nsb_ref[k] = stb_ref[k + 1]
    nsb_ref[kb - 2] = cb


def _conv_sample(state_a, state_b, ga, cb, bg, row_block, wa, ba, wb):
    _, n, d = state_a.shape
    ka, kb = wa.shape[0], wb.shape[0]
    row_spec = pl.BlockSpec((n, CONV_TC), lambda c: (row_block, c))
    out_spec = pl.BlockSpec((n, CONV_TC), lambda c: (0, c))
    state_spec = lambda rows: pl.BlockSpec((rows, n, CONV_TC), lambda c: (0, 0, c))
    return pl.pallas_call(
        functools.partial(_conv_sample_kernel, ka, kb),
        grid=(d // CONV_TC,),
        in_specs=[state_spec(ka - 1), state_spec(kb - 1),
                  row_spec, row_spec, row_spec,
                  pl.BlockSpec((ka, CONV_TC), lambda c: (0, c)),
                  pl.BlockSpec((1, CONV_TC), lambda c: (0, c)),
                  pl.BlockSpec((kb, CONV_TC), lambda c: (0, c))],
        out_specs=[out_spec, out_spec, state_spec(ka - 1), state_spec(kb - 1)],
        out_shape=[jax.ShapeDtypeStruct((n, d), F32), jax.ShapeDtypeStruct((n, d), BF16),
                   jax.ShapeDtypeStruct(state_a.shape, F32), jax.ShapeDtypeStruct(state_b.shape, F32)],
        compiler_params=_cparams(1),
        name="conv_sample",
    )(state_a, state_b, ga, cb, bg, wa, ba, wb)


ROUTE_ID, ROUTE_RANK, ROUTE_W = 0, TOP_K, 2 * TOP_K


def _route_tile(lg, carry):
    tm = lg.shape[0]
    lane = lax.broadcasted_iota(jnp.int32, (tm, LANES), 1)
    neg_inf = jnp.float32(-jnp.inf)

    def first_max(v):
        m = jnp.max(v, axis=-1, keepdims=True)
        return m, jnp.min(jnp.where(v == m, lane, LANES), axis=-1, keepdims=True)

    g_mask = lane < N_GROUPS
    g_max, g_sel = first_max(jnp.where(g_mask, lg, neg_inf))
    g_w = 1.0 / jnp.sum(jnp.where(g_mask, jnp.exp(lg - g_max), 0.0), axis=-1, keepdims=True)
    lane0 = N_GROUPS + g_sel * EXPERTS_PER_GROUP
    e_lg = jnp.where(jnp.logical_and(lane >= lane0, lane < lane0 + EXPERTS_PER_GROUP), lg, neg_inf)
    m1, l1 = first_max(e_lg)
    m2, l2 = first_max(jnp.where(lane == l1, neg_inf, e_lg))
    r = jnp.exp(m2 - m1)
    c1 = g_w / (1.0 + r)
    c2 = g_w * r / (1.0 + r)

    a1 = lane == l1
    a2 = lane == l2
    hit = jnp.where(jnp.logical_or(a1, a2), 1.0, 0.0)
    row = lax.broadcasted_iota(jnp.int32, (tm, tm), 0)
    col = lax.broadcasted_iota(jnp.int32, (tm, tm), 1)
    before = jnp.where(col < row, 1.0, 0.0).astype(BF16)
    seen = jnp.dot(before, hit.astype(BF16), preferred_element_type=F32) + carry
    rank1 = jnp.sum(jnp.where(a1, seen, 0.0), axis=-1, keepdims=True)
    rank2 = jnp.sum(jnp.where(a2, seen, 0.0), axis=-1, keepdims=True)
    carry = carry + jnp.sum(hit, axis=0, keepdims=True)

    rec = jnp.zeros((tm, LANES), F32)
    fields = [(l1 - N_GROUPS).astype(F32), (l2 - N_GROUPS).astype(F32), rank1, rank2, c1, c2]
    for n, v in enumerate(fields):
        rec = jnp.where(lane == n, v, rec)
    return rec, carry


def _mixer_kernel(alpha, n_tiles, ua_ref, pb_ref, sga_ref, sgb_ref, x_ref, wa_hbm, wb_hbm, wo_hbm,
                  lnag_ref, lnab_ref, ln1g_ref, ln1b_ref, wrf_ref, br_ref, *refs):
    x1_ref, rt_ref, ri_ref, cnt_ref = refs[-11:-7]
    carry_ref, wa_ref, wb_ref, wo_ref, wr_ref, stage_ref, sem_ref = refs[-7:]
    i = pl.program_id(0)

    @pl.when(i == 0)
    def _():
        carry_ref[...] = jnp.zeros_like(carry_ref)
        rows = stage_ref.shape[1]
        jobs = [(src, dst, r0) for src, dst in ((wa_hbm, wa_ref), (wb_hbm, wb_ref), (wo_hbm, wo_ref))
                for r0 in range(0, src.shape[0], rows)]

        def chunk_copy(n):
            src, _, r0 = jobs[n]
            return pltpu.make_async_copy(src.at[pl.ds(r0, rows)], stage_ref.at[n % 2], sem_ref.at[n % 2])

        chunk_copy(0).start()
        for n, (_, dst, r0) in enumerate(jobs):
            if n + 1 < len(jobs):
                chunk_copy(n + 1).start()
            chunk_copy(n).wait()
            dst[r0:r0 + rows, :] = stage_ref[n % 2].astype(BF16)
        wr = wrf_ref[...]
        wr_hi = wr.astype(BF16)
        wr_ref[:, 0:LANES] = wr_hi
        wr_ref[:, LANES:2 * LANES] = (wr - wr_hi.astype(F32)).astype(BF16)

    def tile(rows, ua_t, pb_t, sga_t, sgb_t, x_t):
        un = _layer_norm(ua_t[...], lnag_ref[...], lnab_ref[...])
        act = (un * _sigmoid(un)).astype(BF16)
        ya = jnp.dot(act, wa_ref[...], preferred_element_type=F32)
        yb = jnp.dot(pb_t[...], wb_ref[...], preferred_element_type=F32)
        m = (sga_t[...].astype(F32) * ya + sgb_t[...].astype(F32) * yb).astype(BF16)
        mixed = jnp.dot(m, wo_ref[...], preferred_element_type=F32)
        x1 = _layer_norm(alpha * x_t[...] + mixed, ln1g_ref[...], ln1b_ref[...])
        x1_ref[0:rows, :] = x1
        hi = x1.astype(BF16)
        lo = (x1 - hi.astype(F32)).astype(BF16)
        a = jnp.dot(hi, wr_ref[...], preferred_element_type=F32)
        b = jnp.dot(lo, wr_ref[...], preferred_element_type=F32)
        lg = a[:, :LANES] + a[:, LANES:] + b[:, :LANES] + br_ref[...]
        rec, carry = _route_tile(lg, carry_ref[...])
        carry_ref[...] = carry
        rt_ref[0:rows, :] = rec
        ri_ref[:, 0:rows] = rec.T[0:2 * TOP_K, :].astype(jnp.int32)

    @pl.when(i < n_tiles)
    def _():
        tile(x_ref.shape[0], ua_ref, pb_ref, sga_ref, sgb_ref, x_ref)

    if len(refs) == 16:
        tail_refs = refs[:5]

        @pl.when(i == n_tiles)
        def _():
            tile(tail_refs[0].shape[0], *tail_refs)

    cnt_ref[...] = carry_ref[...]


def _mixer(tm, n_tiles, alpha, ua, pb, sga, sgb, x, wa, wb, wo, lnag, lnab, ln1g, ln1b, wr, br, tail=None):
    d = x.shape[1]
    last = n_tiles - 1
    in_spec = pl.BlockSpec((tm, d), lambda i: (jnp.minimum(i, last), 0))
    vec_spec = pl.BlockSpec((1, d), lambda i: (0, 0))
    lane_spec = pl.BlockSpec((1, LANES), lambda i: (0, 0))
    w_spec = pl.BlockSpec(memory_space=pl.ANY)
    in_specs = [in_spec, in_spec, in_spec, in_spec, in_spec, w_spec, w_spec, w_spec,
                vec_spec, vec_spec, vec_spec, vec_spec,
                pl.BlockSpec((d, LANES), lambda i: (0, 0)), lane_spec]
    args = [ua, pb, sga, sgb, x, wa, wb, wo, lnag, lnab, ln1g, ln1b, wr, br]
    n_rows, n_steps, n_tail = n_tiles * tm, n_tiles, 0
    if tail is not None:
        n_tail = tail[0].shape[0]
        assert n_tail <= tm and all(a.shape[0] >= n_tail for a in tail)
        in_specs += [pl.BlockSpec((n_tail, d), lambda i: (0, 0), pipeline_mode=pl.Buffered(1))] * len(tail)
        args += list(tail)
        n_rows, n_steps = n_rows + n_tail, n_steps + 1
    return pl.pallas_call(
        functools.partial(_mixer_kernel, alpha, n_tiles),
        grid=(n_steps,),
        in_specs=in_specs,
        out_specs=[pl.BlockSpec((tm, d), lambda i: (i, 0)), pl.BlockSpec((tm, LANES), lambda i: (i, 0)),
                   pl.BlockSpec((2 * TOP_K, tm), lambda i: (0, i)), lane_spec],
        out_shape=[jax.ShapeDtypeStruct((n_rows, d), F32), jax.ShapeDtypeStruct((n_rows, LANES), F32),
                   jax.ShapeDtypeStruct((2 * TOP_K, n_rows), jnp.int32), jax.ShapeDtypeStruct((1, LANES), F32)],
        scratch_shapes=[pltpu.VMEM((1, LANES), F32), pltpu.VMEM((d, d), BF16), pltpu.VMEM((d, d), BF16),
                        pltpu.VMEM((d, d), BF16), pltpu.VMEM((d, 2 * LANES), BF16),
                        pltpu.VMEM((2, MIX_WROWS, d), F32), pltpu.SemaphoreType.DMA((2,))],
        compiler_params=_cparams(1, 3 * d * d * 2 + 2 * MIX_WROWS * d * 4 + (2 * 18 + 12) * tm * d
                                 + 18 * n_tail * d + (2 << 20)),
        name="mixer",
    )(*args)


def _plan_kernel(n_tok, n_blocks, *refs):
    id_refs = refs[0:TOP_K]
    rank_refs = refs[TOP_K:2 * TOP_K]
    cnt_ref, be_ref, tok_ref, dst_ref, nu_ref, start_ref = refs[2 * TOP_K:]
    shift = MOE_BM.bit_length() - 1

    def per_expert(e, blk0):
        cnt = cnt_ref[0, N_GROUPS + e]
        nb = lax.shift_right_logical(cnt + (MOE_BM - 1), shift)
        start_ref[e] = blk0 * MOE_BM

        def fill(j, carry):
            be_ref[blk0 + j] = e
            return carry

        lax.fori_loop(0, nb, fill, 0)

        def pad(s, carry):
            tok_ref[s] = 0
            return carry

        lax.fori_loop(blk0 * MOE_BM + cnt, (blk0 + nb) * MOE_BM, pad, 0)
        return blk0 + nb

    n_used = lax.fori_loop(0, N_EXPERTS, per_expert, 0)
    nu_ref[0] = n_used

    def rest(b, carry):
        be_ref[b] = N_EXPERTS - 1

        def pad(s, c):
            tok_ref[b * MOE_BM + s] = 0
            return c

        lax.fori_loop(0, MOE_BM, pad, 0, unroll=8)
        return carry

    lax.fori_loop(n_used, n_blocks, rest, 0)

    def place(t, carry):
        for k in range(TOP_K):
            slot = start_ref[id_refs[k][t]] + rank_refs[k][t]
            dst_ref[k * n_tok + t] = slot
            tok_ref[slot] = t
        return carry

    lax.fori_loop(0, n_tok, place, 0, unroll=8)


def _plan(ids, ranks, cnt):
    n_tok = ids[0].shape[0]
    n_blocks = -(-n_tok * TOP_K // MOE_BM) + N_EXPERTS
    smem = pl.BlockSpec(memory_space=pltpu.SMEM)
    return pl.pallas_call(
        functools.partial(_plan_kernel, n_tok, n_blocks),
        in_specs=[smem] * (2 * TOP_K + 1),
        out_specs=[smem, smem, smem, smem],
        out_shape=[jax.ShapeDtypeStruct((n_blocks,), jnp.int32), jax.ShapeDtypeStruct((n_blocks * MOE_BM,), jnp.int32),
                   jax.ShapeDtypeStruct((TOP_K * n_tok,), jnp.int32), jax.ShapeDtypeStruct((1,), jnp.int32)],
        scratch_shapes=[pltpu.SMEM((N_EXPERTS,), jnp.int32)],
        name="plan",
    )(*ids, *ranks, cnt)


def _moe_kernel(be_ref, tok_ref, nused_ref, cnt_ref, x_hbm, wg_hbm, wu_hbm, wd_hbm, ys_ref,
                xbuf_ref, gsem_ref, wgf_ref, wuf_ref, wdf_ref, wsem_ref, wgu_ref, wdb_ref, ord_ref):
    b = pl.program_id(0)
    n_used = nused_ref[0]
    de2 = wgu_ref.shape[1]
    de = de2 // 2
    shift = MOE_BM.bit_length() - 1

    def weight_copies(e, slot):
        return [pltpu.make_async_copy(wg_hbm.at[e], wgf_ref.at[slot], wsem_ref.at[slot]),
                pltpu.make_async_copy(wu_hbm.at[e], wuf_ref.at[slot], wsem_ref.at[slot]),
                pltpu.make_async_copy(wd_hbm.at[e], wdf_ref.at[slot], wsem_ref.at[slot])]

    def row_copy(blk, slot, r):
        return pltpu.make_async_copy(x_hbm.at[pl.ds(tok_ref[blk * MOE_BM + r], 1)],
                                     xbuf_ref.at[slot, pl.ds(r, 1)], gsem_ref.at[slot])

    @pl.when(jnp.logical_and(b == 0, n_used > 0))
    def _():
        ord_ref[0] = 0
        for cp in weight_copies(be_ref[0], 0):
            cp.start(priority=WEIGHT_DMA_PRIORITY)
        for ahead in range(MOE_AHEAD):
            for r in range(MOE_BM):
                row_copy(jnp.minimum(ahead, n_used - 1), ahead, r).start()

    @pl.when(b < n_used)
    def _():
        e = be_ref[b]

        @pl.when(jnp.logical_or(b == 0, e != be_ref[jnp.maximum(b - 1, 0)]))
        def _():
            order = ord_ref[0]
            wslot = order % 2
            for cp in weight_copies(e, wslot):
                cp.wait()
            nxt = b + lax.shift_right_logical(cnt_ref[0, N_GROUPS + e] + (MOE_BM - 1), shift)

            @pl.when(nxt < n_used)
            def _():
                for cp in weight_copies(be_ref[jnp.minimum(nxt, be_ref.shape[0] - 1)], 1 - wslot):
                    cp.start(priority=WEIGHT_DMA_PRIORITY)

            wgu_ref[:, 0:de] = wgf_ref[wslot].astype(BF16)
            wgu_ref[:, de:de2] = wuf_ref[wslot].astype(BF16)
            wdb_ref[...] = wdf_ref[wslot].astype(BF16)
            ord_ref[0] = order + 1

        n_buf = MOE_AHEAD + 1
        slot = b % n_buf
        for r in range(MOE_BM):
            row_copy(b, slot, r).wait()
        nxt_blk = jnp.minimum(b + MOE_AHEAD, n_used - 1)
        nxt_slot = (b + MOE_AHEAD) % n_buf
        xb = xbuf_ref[slot].astype(BF16)
        n_chunks = de2 // MOE_NC
        per = MOE_BM // n_chunks
        gu = []
        for c in range(n_chunks):
            for r in range(c * per, (c + 1) * per):
                row_copy(nxt_blk, nxt_slot, r).start()
            gu.append(jnp.dot(xb, wgu_ref[:, c * MOE_NC:(c + 1) * MOE_NC], preferred_element_type=F32))
        half = n_chunks // 2
        y = None
        for c in range(half):
            g = gu[c]
            h = (g * _sigmoid(g) * gu[half + c]).astype(BF16)
            part = jnp.dot(h, wdb_ref[c * MOE_NC:(c + 1) * MOE_NC, :], preferred_element_type=F32)
            y = part if y is None else y + part
        ys_ref[...] = y

    @pl.when(b == n_used - 1)
    def _():
        for ahead in range(1, MOE_AHEAD + 1):
            for r in range(MOE_BM):
                row_copy(b, (b + ahead) % (MOE_AHEAD + 1), r).wait()

    @pl.when(b >= n_used)
    def _():
        ys_ref[...] = jnp.zeros_like(ys_ref)


def _moe(block_expert, slot_tok, n_used, cnt, x1, wg, wu, wd):
    n_blocks = block_expert.shape[0]
    n_slots = slot_tok.shape[0]
    _, d, de = wg.shape
    any_spec = pl.BlockSpec(memory_space=pl.ANY)
    grid_spec = pltpu.PrefetchScalarGridSpec(
        num_scalar_prefetch=4,
        grid=(n_blocks,),
        in_specs=[any_spec, any_spec, any_spec, any_spec],
        out_specs=pl.BlockSpec((MOE_BM, d), lambda b, *_: (b, 0)),
        scratch_shapes=[pltpu.VMEM((MOE_AHEAD + 1, MOE_BM, d), F32), pltpu.SemaphoreType.DMA((MOE_AHEAD + 1,)),
                        pltpu.VMEM((2, d, de), F32), pltpu.VMEM((2, d, de), F32), pltpu.VMEM((2, de, d), F32),
                        pltpu.SemaphoreType.DMA((2,)),
                        pltpu.VMEM((d, 2 * de), BF16), pltpu.VMEM((de, d), BF16), pltpu.SMEM((1,), jnp.int32)],
    )
    return pl.pallas_call(
        _moe_kernel,
        grid_spec=grid_spec,
        out_shape=jax.ShapeDtypeStruct((n_slots, d), F32),
        compiler_params=_cparams(1),
        name="moe_ffn",
    )(block_expert, slot_tok, n_used, cnt, x1, wg, wu, wd)


def _combine_kernel(alpha, tile_off, n_tok, dst_ref, ys_hbm, x1_ref, rt_ref, g_ref, b_ref, out_ref,
                    buf_ref, sem_ref):
    i = pl.program_id(0)
    n = pl.num_programs(0)
    tm = x1_ref.shape[0]

    def row_copy(tile, slot, r, k):
        src = dst_ref[k * n_tok + (tile + tile_off) * tm + r]
        return pltpu.make_async_copy(ys_hbm.at[pl.ds(src, 1)], buf_ref.at[slot, k, pl.ds(r, 1)], sem_ref.at[slot])

    n_buf = CMB_AHEAD + 1

    @pl.when(i == 0)
    def _():
        for ahead in range(CMB_AHEAD):
            for r in range(tm):
                for k in range(TOP_K):
                    row_copy(jnp.minimum(ahead, n - 1), ahead, r, k).start()

    slot = i % n_buf
    for r in range(tm):
        for k in range(TOP_K):
            row_copy(i, slot, r, k).wait()
    nxt = jnp.minimum(i + CMB_AHEAD, n - 1)
    nxt_slot = (i + CMB_AHEAD) % n_buf
    for r0 in range(0, tm, CMB_RC):
        for r in range(r0, r0 + CMB_RC):
            for k in range(TOP_K):
                row_copy(nxt, nxt_slot, r, k).start()
        rt = rt_ref[r0:r0 + CMB_RC, :]
        f = (rt[:, ROUTE_W:ROUTE_W + 1] * buf_ref[slot, 0, r0:r0 + CMB_RC, :]
             + rt[:, ROUTE_W + 1:ROUTE_W + 2] * buf_ref[slot, 1, r0:r0 + CMB_RC, :])
        out_ref[r0:r0 + CMB_RC, :] = _layer_norm(alpha * x1_ref[r0:r0 + CMB_RC, :] + f, g_ref[...], b_ref[...])

    @pl.when(i == n - 1)
    def _():
        for ahead in range(1, n_buf):
            for r in range(tm):
                for k in range(TOP_K):
                    row_copy(i, (i + ahead) % n_buf, r, k).wait()


def _combine(tm, n_tiles, tile_off, alpha, dst, ys, x1, rt, g, b):
    n_tok, d = x1.shape
    grid_spec = pltpu.PrefetchScalarGridSpec(
        num_scalar_prefetch=1,
        grid=(n_tiles,),
        in_specs=[pl.BlockSpec(memory_space=pl.ANY),
                  pl.BlockSpec((tm, d), lambda i, *_: (i + tile_off, 0)),
                  pl.BlockSpec((tm, LANES), lambda i, *_: (i + tile_off, 0)),
                  pl.BlockSpec((1, d), lambda i, *_: (0, 0)),
                  pl.BlockSpec((1, d), lambda i, *_: (0, 0))],
        out_specs=pl.BlockSpec((tm, d), lambda i, *_: (i, 0)),
        scratch_shapes=[pltpu.VMEM((CMB_AHEAD + 1, TOP_K, tm, d), F32), pltpu.SemaphoreType.DMA((CMB_AHEAD + 1,))],
    )
    return pl.pallas_call(
        functools.partial(_combine_kernel, alpha, tile_off, n_tok),
        grid_spec=grid_spec,
        out_shape=jax.ShapeDtypeStruct((n_tiles * tm, d), F32),
        compiler_params=_cparams(1),
        name="combine",
    )(dst, ys, x1, rt, g, b)


def kernel(x_prompt, x_sample, state_conv_a, state_conv_b, meta_tokens, w_in, b_in, conv_a_w, conv_a_b, ln_a_g, ln_a_b, w_a_out, conv_b_w, w_b_out, w_o, ln1_g, ln1_b, w_router_group, b_router_group, w_router_expert, b_router_expert, w_exp_gate, w_exp_up, w_exp_down, ln2_g, ln2_b):
    depth = w_in.shape[0]
    assert depth == 1, "single-layer step only"
    n_seq, seq, d = x_prompt.shape
    n_s = x_sample.shape[0]
    n_meta = meta_tokens.shape[0]
    ka, kb = conv_a_w.shape[1], conv_b_w.shape[1]
    assert x_sample.shape[1] == 1 and seq % IN_TM == 0 and IN_TM >= ka - 1
    assert n_meta <= HIST_A and ka - 1 <= HIST_A and kb - 1 <= HIST_B and kb - 1 <= n_meta
    assert n_s + n_meta <= IN_TM and n_s % CMB_TM == 0
    alpha = (2.0 * depth) ** 0.25
    rp = n_seq * seq

    xp = x_prompt.reshape(rp, d)
    xs = jnp.concatenate([x_sample.reshape(n_s, d), meta_tokens,
                          jnp.zeros((IN_TM - n_s - n_meta, d), F32)], axis=0).astype(BF16)
    (ua_p, pb_p, sga_p, sgb_p, new_a_p, new_b_p, ga_x, cb_x, bg_x, sga_x, sgb_x) = _inproj_pipe(
        xp, xs, w_in[0], b_in, conv_a_w[0], conv_a_b, conv_b_w[0], n_seq, seq, n_s, n_meta)
    ua_s, pb_s, new_a_s, new_b_s = _conv_sample(
        jnp.transpose(state_conv_a[0], (1, 0, 2)), jnp.transpose(state_conv_b[0], (1, 0, 2)),
        ga_x, cb_x, bg_x, 0, conv_a_w[0], conv_a_b, conv_b_w[0])
    new_a_s = jnp.transpose(new_a_s, (1, 0, 2))[None]
    new_b_s = jnp.transpose(new_b_s, (1, 0, 2))[None]

    t = rp + n_s
    wr_f = jnp.concatenate([w_router_group[0], w_router_expert[0].transpose(1, 0, 2).reshape(d, N_EXPERTS),
                            jnp.zeros((d, LANES - N_GROUPS - N_EXPERTS), F32)], axis=1)
    br = jnp.concatenate([b_router_group[0], b_router_expert[0].reshape(-1),
                          jnp.zeros((LANES - N_GROUPS - N_EXPERTS,), F32)])[None, :]
    x1, rt, ri, cnt = _mixer(MIX_TM, rp // MIX_TM, alpha, ua_p, pb_p, sga_p, sgb_p, xp, w_a_out[0], w_b_out[0], w_o[0],
                             ln_a_g, ln_a_b, ln1_g, ln1_b, wr_f, br,
                             tail=(ua_s, pb_s, sga_x, sgb_x, x_sample.reshape(n_s, d)))

    cnt_i = cnt.astype(jnp.int32)
    ids = [ri[ROUTE_ID + k] for k in range(TOP_K)]
    ranks = [ri[ROUTE_RANK + k] for k in range(TOP_K)]
    block_expert, slot_tok, dest, n_used = _plan(ids, ranks, cnt_i)
    ys = _moe(block_expert, slot_tok, n_used, cnt_i, x1, w_exp_gate[0], w_exp_up[0], w_exp_down[0])
    y_p = _combine(CMB_TM, rp // CMB_TM, 0, alpha, dest, ys, x1, rt, ln2_g, ln2_b)
    y_s = _combine(CMB_TM, n_s // CMB_TM, rp // CMB_TM, alpha, dest, ys, x1, rt, ln2_g, ln2_b)

    return (y_p.reshape(n_seq, seq, d), y_s.reshape(n_s, 1, d), new_a_p[None], new_b_p[None], new_a_s, new_b_s)
```

```python
import functools

import jax
import jax.numpy as jnp
from jax import lax
from jax.experimental import pallas as pl
from jax.experimental.pallas import tpu as pltpu

F32 = jnp.float32
BF16 = jnp.bfloat16

LN_EPS = 1e-5
N_GROUPS = 4
EXPERTS_PER_GROUP = 8
N_EXPERTS = N_GROUPS * EXPERTS_PER_GROUP
TOP_K = 2
N_PROJ_BLOCKS = 7

VMEM_LIMIT_BYTES = 56 * 1024 * 1024
LANES = 128

IN_TM = 512
IN_TN = 256
IN_HM = 512
W_UNIT_ROWS = 1024
CONV_TC = 256
CONV_RC = 64
HIST_A = 32
HIST_B = 8
MIX_TM = 256
MIX_WROWS = 128
MOE_BM = 128
MOE_NC = 256
MOE_AHEAD = 4
MOE_WSLOTS = 2
WEIGHT_DMA_PRIORITY = 1
CMB_TM = 128
CMB_RC = 32
CMB_AHEAD = 3


def _cparams(n_axes, vmem_bytes=None):
    limit = VMEM_LIMIT_BYTES if vmem_bytes is None else min(int(vmem_bytes), VMEM_LIMIT_BYTES)
    return pltpu.CompilerParams(dimension_semantics=("arbitrary",) * n_axes, vmem_limit_bytes=limit)


def _sigmoid(x):
    return 1.0 / (1.0 + jnp.exp(-x))


def _layer_norm(x, g, b):
    mu = jnp.mean(x, axis=-1, keepdims=True)
    xc = x - mu
    var = jnp.mean(xc * xc, axis=-1, keepdims=True)
    return xc * lax.rsqrt(var + LN_EPS) * g + b


def _residues(n_taps, hist):
    return sorted({(hist - (n_taps - 1) + k) % 8 for k in range(n_taps)})


def _inproj_pipe_kernel(ka, kb, tiles_per_seq, n_tiles, n_s, n_meta, xp_ref, xs_ref, wnext_ref, w_hbm, *refs):
    b_refs = refs[0:7]
    wa_ref, ba_ref, wb_ref = refs[7:10]
    ua_ref, pb_ref, sga_ref, sgb_ref, na_ref, nb_ref = refs[10:16]
    gas_ref, cbs_ref, bgs_ref, sgas_ref, sgbs_ref = refs[16:21]
    wbf_ref, stage_ref, sem_ref, xbf_ref = refs[21:25]
    bufs = (refs[25:28], refs[28:31])
    ha_ref, hb_ref = refs[31:33]
    j = pl.program_id(0)
    i = pl.program_id(1)
    nj = pl.num_programs(0)
    tm, d = xbf_ref.shape
    tn = ha_ref.shape[-1]
    unit_rows = stage_ref.shape[1]
    units_per_block = d // unit_rows
    n_units = N_PROJ_BLOCKS * units_per_block
    t = i - 1

    @pl.when(jnp.logical_and(j == 0, i == 0))
    def _():
        def unit_copy(u):
            k, h = divmod(u, units_per_block)
            return pltpu.make_async_copy(
                w_hbm.at[pl.ds(h * unit_rows, unit_rows), pl.ds(k * d, tn)], stage_ref.at[u % 2], sem_ref.at[u % 2])

        unit_copy(0).start()
        for u in range(n_units):
            if u + 1 < n_units:
                unit_copy(u + 1).start()
            unit_copy(u).wait()
            k, h = divmod(u, units_per_block)
            wbf_ref[0, k, h * unit_rows:(h + 1) * unit_rows, :] = stage_ref[u % 2].astype(BF16)
        for buf in bufs:
            for ref in buf:
                ref[...] = jnp.zeros_like(ref)

    @pl.when(jnp.logical_and(i < n_units, j + 1 < nj))
    def _():
        k = i // units_per_block
        h = i % units_per_block
        row0 = pl.multiple_of(h * unit_rows, unit_rows)
        wbf_ref[(j + 1) % 2, k, pl.ds(row0, unit_rows), :] = wnext_ref[...].astype(BF16)

    @pl.when(i == 0)
    def _():
        xbf_ref[...] = xs_ref[...]

    @pl.when(jnp.logical_and(i > 0, i < n_tiles))
    def _():
        xbf_ref[...] = xp_ref[...].astype(BF16)

    res_a, res_b = _residues(ka, HIST_A), _residues(kb, HIST_B)
    meta_end = n_s + n_meta

    def raw_rows(ref, res, hist):
        return ref[res.index(0), hist:hist + tm, :]

    def last_rows(ref, res, hist, n_taps):
        rho = (hist - (n_taps - 1)) % 8
        start = hist + tm - (n_taps - 1) - rho
        return ref[res.index(rho), start:start + n_taps - 1, :]

    def set_history(ref, res, hist, src, row0):
        for q, rho in enumerate(res):
            if hist - rho > 0:
                ref[q, 0:hist - rho, :] = src[q, row0:row0 + hist - rho, :]

    @pl.when(i == 1)
    def _():
        sa_ref, sb_ref, bg_ref = bufs[0]
        gas_ref[...] = raw_rows(sa_ref, res_a, HIST_A)
        cbs_ref[...] = raw_rows(sb_ref, res_b, HIST_B)
        bgs_ref[...] = bg_ref[...]
        sgas_ref[...] = sga_ref[...]
        sgbs_ref[...] = sgb_ref[...]
        for h_ref, src, res, hist in ((ha_ref, sa_ref, res_a, HIST_A), (hb_ref, sb_ref, res_b, HIST_B)):
            for q, rho in enumerate(res):
                h_ref[q] = src[q, meta_end:meta_end + hist, :]
                n_zero = hist - n_meta - rho
                if n_zero > 0:
                    h_ref[q, 0:n_zero, :] = jnp.zeros((n_zero, tn), F32)

    seq_pos = (t - 1) % tiles_per_seq
    for par in range(2):
        sa_ref, sb_ref, _ = bufs[par]
        sa_prev, sb_prev, _ = bufs[1 - par]
        is_t = jnp.logical_and(t >= 1, t % 2 == par)

        @pl.when(jnp.logical_and(is_t, seq_pos == 0))
        def _():
            set_history(sa_ref, res_a, HIST_A, ha_ref, 0)
            set_history(sb_ref, res_b, HIST_B, hb_ref, 0)

        @pl.when(jnp.logical_and(is_t, seq_pos != 0))
        def _():
            set_history(sa_ref, res_a, HIST_A, sa_prev, tm)
            set_history(sb_ref, res_b, HIST_B, sb_prev, tm)

    slot = j % 2
    chunks = [(c0, r0) for c0 in range(0, tn, LANES) for r0 in range(0, tm, CONV_RC)]

    def conv_taps(ref, res, w_ref, n_taps, hist, r0, c0):
        acc = None
        for k in range(n_taps):
            o = hist - (n_taps - 1) + k
            term = w_ref[k:k + 1, c0:c0 + LANES] * ref[res.index(o % 8), pl.ds(r0 + o - o % 8, CONV_RC), c0:c0 + LANES]
            acc = term if acc is None else acc + term
        return acc

    def conv_items(buf):
        sa_ref, sb_ref, bg_ref = buf

        def conv_a(c0, r0):
            acc = conv_taps(sa_ref, res_a, wa_ref, ka, HIST_A, r0, c0)
            ua_ref[r0:r0 + CONV_RC, c0:c0 + LANES] = acc + ba_ref[:, c0:c0 + LANES]

        def conv_b(c0, r0):
            accb = conv_taps(sb_ref, res_b, wb_ref, kb, HIST_B, r0, c0)
            pb_ref[r0:r0 + CONV_RC, c0:c0 + LANES] = (bg_ref[r0:r0 + CONV_RC, c0:c0 + LANES] * accb).astype(BF16)

        return [[functools.partial(conv_a, c0, r0), functools.partial(conv_b, c0, r0)] for c0, r0 in chunks]

    def store_shifted(ref, res, hist, m0, value):
        for q, rho in enumerate(res):
            ref[q, pl.ds(hist + m0 - rho, value.shape[0]), :] = value

    def run(todo, n):
        for _ in range(min(n, len(todo))):
            for item in todo.pop(0):
                item()

    def project(buf, todo):
        sa_ref, sb_ref, bg_ref = buf

        def proj(k, m0):
            return (jnp.dot(xbf_ref[m0:m0 + IN_HM, :], wbf_ref[slot, k], preferred_element_type=F32)
                    + b_refs[k][...])

        run(todo, 1)
        for m0 in range(0, tm, IN_HM):
            p0 = proj(0, m0)
            run(todo, 1)
            p1 = proj(1, m0)
            run(todo, 1)
            store_shifted(sa_ref, res_a, HIST_A, m0, p0 * _sigmoid(p1))
            p3 = proj(3, m0)
            run(todo, 1)
            p4 = proj(4, m0)
            run(todo, 1)
            store_shifted(sb_ref, res_b, HIST_B, m0, p3 * p4)
            bg_ref[m0:m0 + IN_HM, :] = proj(2, m0)
            run(todo, 1)
            sga_ref[m0:m0 + IN_HM, :] = _sigmoid(proj(5, m0)).astype(BF16)
            run(todo, 1)
            sgb_ref[m0:m0 + IN_HM, :] = _sigmoid(proj(6, m0)).astype(BF16)
            run(todo, 1)
        run(todo, len(todo))

    for par in range(2):
        @pl.when(jnp.logical_and(i < n_tiles, i % 2 == par))
        def _():
            project(bufs[par], conv_items(bufs[1 - par]))

    @pl.when(i == n_tiles)
    def _():
        todo = conv_items(bufs[(n_tiles - 1) % 2])
        run(todo, len(todo))

    for par in range(2):
        sa_ref, sb_ref, _ = bufs[par]

        @pl.when(jnp.logical_and(jnp.logical_and(t >= 1, t % 2 == par), seq_pos == tiles_per_seq - 1))
        def _():
            na_ref[...] = last_rows(sa_ref, res_a, HIST_A, ka)
            nb_ref[...] = last_rows(sb_ref, res_b, HIST_B, kb)


def _inproj_pipe(xp, xs_bf, w, b, wa, ba, wb, n_seq, seq, n_s, n_meta):
    rp, d = xp.shape
    ka, kb = wa.shape[0], wb.shape[0]
    tiles_per_seq = seq // IN_TM
    n_prompt_tiles = rp // IN_TM
    n_tiles = n_prompt_tiles + 1
    nj = d // IN_TN
    units_per_block = d // W_UNIT_ROWS
    n_units = N_PROJ_BLOCKS * units_per_block
    last = n_prompt_tiles - 1

    def wnext_map(j, i):
        u = jnp.where(j == nj - 1, n_units - 1, jnp.minimum(i, n_units - 1))
        col = jnp.minimum(j + 1, nj - 1)
        return (u % units_per_block, (u // units_per_block) * nj + col)

    def projected(i):
        return jnp.clip(i - 1, 0, last)

    def convolved(i):
        return jnp.clip(i - 2, 0, last)

    b_specs = [pl.BlockSpec((1, IN_TN), functools.partial(lambda j, i, k: (0, k * nj + j), k=k))
               for k in range(N_PROJ_BLOCKS)]
    ch_spec = lambda rows: pl.BlockSpec((rows, IN_TN), lambda j, i: (0, j))
    proj_spec = pl.BlockSpec((IN_TM, IN_TN), lambda j, i: (projected(i), j))
    conv_spec = pl.BlockSpec((IN_TM, IN_TN), lambda j, i: (convolved(i), j))
    state_spec = lambda rows: pl.BlockSpec((None, rows, IN_TN), lambda j, i: (convolved(i) // tiles_per_seq, 0, j))
    small_spec = pl.BlockSpec((IN_TM, IN_TN), lambda j, i: (0, j))
    sds = jax.ShapeDtypeStruct
    n_res_a, n_res_b = len(_residues(ka, HIST_A)), len(_residues(kb, HIST_B))
    assert (n_s + n_meta) % 8 == 0 and n_s + n_meta + max(HIST_A, HIST_B) <= IN_TM
    tile_bufs = [pltpu.VMEM((n_res_a, HIST_A + IN_TM, IN_TN), F32), pltpu.VMEM((n_res_b, HIST_B + IN_TM, IN_TN), F32),
                 pltpu.VMEM((IN_TM, IN_TN), F32)]
    return pl.pallas_call(
        functools.partial(_inproj_pipe_kernel, ka, kb, tiles_per_seq, n_tiles, n_s, n_meta),
        grid=(nj, n_tiles + 1),
        in_specs=[pl.BlockSpec((IN_TM, d), lambda j, i: (projected(i), 0)),
                  pl.BlockSpec((IN_TM, d), lambda j, i: (0, 0)),
                  pl.BlockSpec((W_UNIT_ROWS, IN_TN), wnext_map),
                  pl.BlockSpec(memory_space=pl.ANY)] + b_specs + [ch_spec(ka), ch_spec(1), ch_spec(kb)],
        out_specs=[conv_spec, conv_spec, proj_spec, proj_spec, state_spec(ka - 1), state_spec(kb - 1),
                   small_spec, small_spec, small_spec, small_spec, small_spec],
        out_shape=[sds((rp, d), F32), sds((rp, d), BF16), sds((rp, d), BF16), sds((rp, d), BF16),
                   sds((n_seq, ka - 1, d), F32), sds((n_seq, kb - 1, d), F32),
                   sds((IN_TM, d), F32), sds((IN_TM, d), F32), sds((IN_TM, d), F32),
                   sds((IN_TM, d), BF16), sds((IN_TM, d), BF16)],
        scratch_shapes=[pltpu.VMEM((2, N_PROJ_BLOCKS, d, IN_TN), BF16),
                        pltpu.VMEM((2, W_UNIT_ROWS, IN_TN), F32), pltpu.SemaphoreType.DMA((2,)),
                        pltpu.VMEM((IN_TM, d), BF16)] + tile_bufs + tile_bufs + [
                        pltpu.VMEM((n_res_a, HIST_A, IN_TN), F32), pltpu.VMEM((n_res_b, HIST_B, IN_TN), F32)],
        compiler_params=_cparams(2),
        name="inproj_conv",
    )(xp, xs_bf, w, w, *([b] * N_PROJ_BLOCKS), wa, ba, wb)


def _conv_sample_kernel(ka, kb, sta_ref, stb_ref, ga_ref, cb_ref, bg_ref, wa_ref, ba_ref, wb_ref,
                        ua_ref, pb_ref, nsa_ref, nsb_ref):
    ga = ga_ref[...]
    cb = cb_ref[...]
    acc = wa_ref[ka - 1:ka, :] * ga
    for k in range(ka - 1):
        acc = acc + wa_ref[k:k + 1, :] * sta_ref[k]
    ua_ref[...] = acc + ba_ref[...]
    accb = wb_ref[kb - 1:kb, :] * cb
    for k in range(kb - 1):
        accb = accb + wb_ref[k:k + 1, :] * stb_ref[k]
    pb_ref[...] = (bg_ref[...] * accb).astype(BF16)
    for k in range(ka - 2):
        nsa_ref[k] = sta_ref[k + 1]
    nsa_ref[ka - 2] = ga
    for k in range(kb - 2):
        nsb_ref[k] = stb_ref[k + 1]
    nsb_ref[kb - 2] = cb


def _conv_sample(state_a, state_b, ga, cb, bg, row_block, wa, ba, wb):
    _, n, d = state_a.shape
    ka, kb = wa.shape[0], wb.shape[0]
    row_spec = pl.BlockSpec((n, CONV_TC), lambda c: (row_block, c))
    out_spec = pl.BlockSpec((n, CONV_TC), lambda c: (0, c))
    state_spec = lambda rows: pl.BlockSpec((rows, n, CONV_TC), lambda c: (0, 0, c))
    return pl.pallas_call(
        functools.partial(_conv_sample_kernel, ka, kb),
        grid=(d // CONV_TC,),
        in_specs=[state_spec(ka - 1), state_spec(kb - 1),
                  row_spec, row_spec, row_spec,
                  pl.BlockSpec((ka, CONV_TC), lambda c: (0, c)),
                  pl.BlockSpec((1, CONV_TC), lambda c: (0, c)),
                  pl.BlockSpec((kb, CONV_TC), lambda c: (0, c))],
        out_specs=[out_spec, out_spec, state_spec(ka - 1), state_spec(kb - 1)],
        out_shape=[jax.ShapeDtypeStruct((n, d), F32), jax.ShapeDtypeStruct((n, d), BF16),
                   jax.ShapeDtypeStruct(state_a.shape, F32), jax.ShapeDtypeStruct(state_b.shape, F32)],
        compiler_params=_cparams(1),
        name="conv_sample",
    )(state_a, state_b, ga, cb, bg, wa, ba, wb)


ROUTE_ID, ROUTE_RANK, ROUTE_W = 0, TOP_K, 2 * TOP_K


def _route_tile(lg, carry):
    tm = lg.shape[0]
    lane = lax.broadcasted_iota(jnp.int32, (tm, LANES), 1)
    neg_inf = jnp.float32(-jnp.inf)

    def first_max(v):
        m = jnp.max(v, axis=-1, keepdims=True)
        return m, jnp.min(jnp.where(v == m, lane, LANES), axis=-1, keepdims=True)

    g_mask = lane < N_GROUPS
    g_max, g_sel = first_max(jnp.where(g_mask, lg, neg_inf))
    g_w = 1.0 / jnp.sum(jnp.where(g_mask, jnp.exp(lg - g_max), 0.0), axis=-1, keepdims=True)
    lane0 = N_GROUPS + g_sel * EXPERTS_PER_GROUP
    e_lg = jnp.where(jnp.logical_and(lane >= lane0, lane < lane0 + EXPERTS_PER_GROUP), lg, neg_inf)
    m1, l1 = first_max(e_lg)
    m2, l2 = first_max(jnp.where(lane == l1, neg_inf, e_lg))
    r = jnp.exp(m2 - m1)
    c1 = g_w / (1.0 + r)
    c2 = g_w * r / (1.0 + r)

    a1 = lane == l1
    a2 = lane == l2
    hit = jnp.where(jnp.logical_or(a1, a2), 1.0, 0.0)
    row = lax.broadcasted_iota(jnp.int32, (tm, tm), 0)
    col = lax.broadcasted_iota(jnp.int32, (tm, tm), 1)
    before = jnp.where(col < row, 1.0, 0.0).astype(BF16)
    seen = jnp.dot(before, hit.astype(BF16), preferred_element_type=F32) + carry
    rank1 = jnp.sum(jnp.where(a1, seen, 0.0), axis=-1, keepdims=True)
    rank2 = jnp.sum(jnp.where(a2, seen, 0.0), axis=-1, keepdims=True)
    carry = carry + jnp.sum(hit, axis=0, keepdims=True)

    rec = jnp.zeros((tm, LANES), F32)
    fields = [(l1 - N_GROUPS).astype(F32), (l2 - N_GROUPS).astype(F32), rank1, rank2, c1, c2]
    for n, v in enumerate(fields):
        rec = jnp.where(lane == n, v, rec)
    return rec, carry


def _mixer_kernel(alpha, n_tiles, ua_ref, pb_ref, sga_ref, sgb_ref, x_ref, wa_hbm, wb_hbm, wo_hbm,
                  lnag_ref, lnab_ref, ln1g_ref, ln1b_ref, wrf_ref, br_ref, *refs):
    x1_ref, rt_ref, ri_ref, cnt_ref = refs[-11:-7]
    carry_ref, wa_ref, wb_ref, wo_ref, wr_ref, stage_ref, sem_ref = refs[-7:]
    i = pl.program_id(0)

    @pl.when(i == 0)
    def _():
        carry_ref[...] = jnp.zeros_like(carry_ref)
        rows = stage_ref.shape[1]
        jobs = [(src, dst, r0) for src, dst in ((wa_hbm, wa_ref), (wb_hbm, wb_ref), (wo_hbm, wo_ref))
                for r0 in range(0, src.shape[0], rows)]

        def chunk_copy(n):
            src, _, r0 = jobs[n]
            return pltpu.make_async_copy(src.at[pl.ds(r0, rows)], stage_ref.at[n % 2], sem_ref.at[n % 2])

        chunk_copy(0).start()
        for n, (_, dst, r0) in enumerate(jobs):
            if n + 1 < len(jobs):
                chunk_copy(n + 1).start()
            chunk_copy(n).wait()
            dst[r0:r0 + rows, :] = stage_ref[n % 2].astype(BF16)
        wr = wrf_ref[...]
        wr_hi = wr.astype(BF16)
        wr_ref[:, 0:LANES] = wr_hi
        wr_ref[:, LANES:2 * LANES] = (wr - wr_hi.astype(F32)).astype(BF16)

    def tile(rows, ua_t, pb_t, sga_t, sgb_t, x_t):
        un = _layer_norm(ua_t[...], lnag_ref[...], lnab_ref[...])
        act = (un * _sigmoid(un)).astype(BF16)
        ya = jnp.dot(act, wa_ref[...], preferred_element_type=F32)
        yb = jnp.dot(pb_t[...], wb_ref[...], preferred_element_type=F32)
        m = (sga_t[...].astype(F32) * ya + sgb_t[...].astype(F32) * yb).astype(BF16)
        mixed = jnp.dot(m, wo_ref[...], preferred_element_type=F32)
        x1 = _layer_norm(alpha * x_t[...] + mixed, ln1g_ref[...], ln1b_ref[...])
        x1_ref[0:rows, :] = x1
        hi = x1.astype(BF16)
        lo = (x1 - hi.astype(F32)).astype(BF16)
        a = jnp.dot(hi, wr_ref[...], preferred_element_type=F32)
        b = jnp.dot(lo, wr_ref[...], preferred_element_type=F32)
        lg = a[:, :LANES] + a[:, LANES:] + b[:, :LANES] + br_ref[...]
        rec, carry = _route_tile(lg, carry_ref[...])
        carry_ref[...] = carry
        rt_ref[0:rows, :] = rec
        ri_ref[:, 0:rows] = rec.T[0:2 * TOP_K, :].astype(jnp.int32)

    @pl.when(i < n_tiles)
    def _():
        tile(x_ref.shape[0], ua_ref, pb_ref, sga_ref, sgb_ref, x_ref)

    if len(refs) == 16:
        tail_refs = refs[:5]

        @pl.when(i == n_tiles)
        def _():
            tile(tail_refs[0].shape[0], *tail_refs)

    cnt_ref[...] = carry_ref[...]


def _mixer(tm, n_tiles, alpha, ua, pb, sga, sgb, x, wa, wb, wo, lnag, lnab, ln1g, ln1b, wr, br, tail=None):
    d = x.shape[1]
    last = n_tiles - 1
    in_spec = pl.BlockSpec((tm, d), lambda i: (jnp.minimum(i, last), 0))
    vec_spec = pl.BlockSpec((1, d), lambda i: (0, 0))
    lane_spec = pl.BlockSpec((1, LANES), lambda i: (0, 0))
    w_spec = pl.BlockSpec(memory_space=pl.ANY)
    in_specs = [in_spec, in_spec, in_spec, in_spec, in_spec, w_spec, w_spec, w_spec,
                vec_spec, vec_spec, vec_spec, vec_spec,
                pl.BlockSpec((d, LANES), lambda i: (0, 0)), lane_spec]
    args = [ua, pb, sga, sgb, x, wa, wb, wo, lnag, lnab, ln1g, ln1b, wr, br]
    n_rows, n_steps, n_tail = n_tiles * tm, n_tiles, 0
    if tail is not None:
        n_tail = tail[0].shape[0]
        assert n_tail <= tm and all(a.shape[0] >= n_tail for a in tail)
        in_specs += [pl.BlockSpec((n_tail, d), lambda i: (0, 0), pipeline_mode=pl.Buffered(1))] * len(tail)
        args += list(tail)
        n_rows, n_steps = n_rows + n_tail, n_steps + 1
    return pl.pallas_call(
        functools.partial(_mixer_kernel, alpha, n_tiles),
        grid=(n_steps,),
        in_specs=in_specs,
        out_specs=[pl.BlockSpec((tm, d), lambda i: (i, 0)), pl.BlockSpec((tm, LANES), lambda i: (i, 0)),
                   pl.BlockSpec((2 * TOP_K, tm), lambda i: (0, i)), lane_spec],
        out_shape=[jax.ShapeDtypeStruct((n_rows, d), F32), jax.ShapeDtypeStruct((n_rows, LANES), F32),
                   jax.ShapeDtypeStruct((2 * TOP_K, n_rows), jnp.int32), jax.ShapeDtypeStruct((1, LANES), F32)],
        scratch_shapes=[pltpu.VMEM((1, LANES), F32), pltpu.VMEM((d, d), BF16), pltpu.VMEM((d, d), BF16),
                        pltpu.VMEM((d, d), BF16), pltpu.VMEM((d, 2 * LANES), BF16),
                        pltpu.VMEM((2, MIX_WROWS, d), F32), pltpu.SemaphoreType.DMA((2,))],
        compiler_params=_cparams(1, 3 * d * d * 2 + 2 * MIX_WROWS * d * 4 + (2 * 18 + 12) * tm * d
                                 + 18 * n_tail * d + (2 << 20)),
        name="mixer",
    )(*args)


def _plan_kernel(n_tok, n_blocks, *refs):
    id_refs = refs[0:TOP_K]
    rank_refs = refs[TOP_K:2 * TOP_K]
    cnt_ref, be_ref, tok_ref, dst_ref, nu_ref, start_ref = refs[2 * TOP_K:]
    shift = MOE_BM.bit_length() - 1

    def per_expert(e, blk0):
        cnt = cnt_ref[0, N_GROUPS + e]
        nb = lax.shift_right_logical(cnt + (MOE_BM - 1), shift)
        start_ref[e] = blk0 * MOE_BM

        def fill(j, carry):
            be_ref[blk0 + j] = e
            return carry

        lax.fori_loop(0, nb, fill, 0)

        def pad(s, carry):
            tok_ref[s] = 0
            return carry

        lax.fori_loop(blk0 * MOE_BM + cnt, (blk0 + nb) * MOE_BM, pad, 0)
        return blk0 + nb

    n_used = lax.fori_loop(0, N_EXPERTS, per_expert, 0)
    nu_ref[0] = n_used

    def rest(b, carry):
        be_ref[b] = N_EXPERTS - 1

        def pad(s, c):
            tok_ref[b * MOE_BM + s] = 0
            return c

        lax.fori_loop(0, MOE_BM, pad, 0, unroll=8)
        return carry

    lax.fori_loop(n_used, n_blocks, rest, 0)

    def place(t, carry):
        for k in range(TOP_K):
            slot = start_ref[id_refs[k][t]] + rank_refs[k][t]
            dst_ref[k * n_tok + t] = slot
            tok_ref[slot] = t
        return carry

    lax.fori_loop(0, n_tok, place, 0, unroll=8)


def _plan(ids, ranks, cnt):
    n_tok = ids[0].shape[0]
    n_blocks = -(-n_tok * TOP_K // MOE_BM) + N_EXPERTS
    smem = pl.BlockSpec(memory_space=pltpu.SMEM)
    return pl.pallas_call(
        functools.partial(_plan_kernel, n_tok, n_blocks),
        in_specs=[smem] * (2 * TOP_K + 1),
        out_specs=[smem, smem, smem, smem],
        out_shape=[jax.ShapeDtypeStruct((n_blocks,), jnp.int32), jax.ShapeDtypeStruct((n_blocks * MOE_BM,), jnp.int32),
                   jax.ShapeDtypeStruct((TOP_K * n_tok,), jnp.int32), jax.ShapeDtypeStruct((1,), jnp.int32)],
        scratch_shapes=[pltpu.SMEM((N_EXPERTS,), jnp.int32)],
        name="plan",
    )(*ids, *ranks, cnt)


def _moe_kernel(be_ref, tok_ref, nused_ref, cnt_ref, x_hbm, wg_hbm, wu_hbm, wd_hbm, ys_ref,
                xbuf_ref, gsem_ref, wgf_ref, wuf_ref, wdf_ref, wsem_ref, wgu_ref, wdb_ref, ord_ref):
    b = pl.program_id(0)
    n_used = nused_ref[0]
    de2 = wgu_ref.shape[1]
    de = de2 // 2
    shift = MOE_BM.bit_length() - 1

    def weight_copies(e, slot):
        return [pltpu.make_async_copy(wg_hbm.at[e], wgf_ref.at[slot], wsem_ref.at[slot]),
                pltpu.make_async_copy(wu_hbm.at[e], wuf_ref.at[slot], wsem_ref.at[slot]),
                pltpu.make_async_copy(wd_hbm.at[e], wdf_ref.at[slot], wsem_ref.at[slot])]

    def row_copy(blk, slot, r):
        return pltpu.make_async_copy(x_hbm.at[pl.ds(tok_ref[blk * MOE_BM + r], 1)],
                                     xbuf_ref.at[slot, pl.ds(r, 1)], gsem_ref.at[slot])

    n_wslots = wgf_ref.shape[0]
    last_blk = be_ref.shape[0] - 1

    def blocks_of(e):
        return lax.shift_right_logical(cnt_ref[0, N_GROUPS + e] + (MOE_BM - 1), shift)

    @pl.when(jnp.logical_and(b == 0, n_used > 0))
    def _():
        ord_ref[0] = 0
        blk = jnp.int32(0)
        for ahead in range(n_wslots - 1):
            e_ahead = be_ref[jnp.minimum(blk, last_blk)]

            @pl.when(blk < n_used)
            def _():
                for cp in weight_copies(e_ahead, ahead):
                    cp.start(priority=WEIGHT_DMA_PRIORITY)

            blk = blk + blocks_of(e_ahead)
        for ahead in range(MOE_AHEAD):
            for r in range(MOE_BM):
                row_copy(jnp.minimum(ahead, n_used - 1), ahead, r).start()

    @pl.when(b < n_used)
    def _():
        e = be_ref[b]

        @pl.when(jnp.logical_or(b == 0, e != be_ref[jnp.maximum(b - 1, 0)]))
        def _():
            order = ord_ref[0]
            wslot = order % n_wslots
            for cp in weight_copies(e, wslot):
                cp.wait()
            blk = b
            for _ in range(n_wslots - 1):
                blk = blk + blocks_of(be_ref[jnp.minimum(blk, last_blk)])
                blk = jnp.minimum(blk, n_used)

            @pl.when(blk < n_used)
            def _():
                for cp in weight_copies(be_ref[jnp.minimum(blk, last_blk)], (order + n_wslots - 1) % n_wslots):
                    cp.start(priority=WEIGHT_DMA_PRIORITY)

            wgu_ref[:, 0:de] = wgf_ref[wslot].astype(BF16)
            wgu_ref[:, de:de2] = wuf_ref[wslot].astype(BF16)
            wdb_ref[...] = wdf_ref[wslot].astype(BF16)
            ord_ref[0] = order + 1

        n_buf = MOE_AHEAD + 1
        slot = b % n_buf
        for r in range(MOE_BM):
            row_copy(b, slot, r).wait()
        nxt_blk = jnp.minimum(b + MOE_AHEAD, n_used - 1)
        nxt_slot = (b + MOE_AHEAD) % n_buf
        xb = xbuf_ref[slot].astype(BF16)
        n_chunks = de2 // MOE_NC
        per = MOE_BM // n_chunks
        gu = []
        for c in range(n_chunks):
            for r in range(c * per, (c + 1) * per):
                row_copy(nxt_blk, nxt_slot, r).start()
            gu.append(jnp.dot(xb, wgu_ref[:, c * MOE_NC:(c + 1) * MOE_NC], preferred_element_type=F32))
        half = n_chunks // 2
        y = None
        for c in range(half):
            g = gu[c]
            h = (g * _sigmoid(g) * gu[half + c]).astype(BF16)
            part = jnp.dot(h, wdb_ref[c * MOE_NC:(c + 1) * MOE_NC, :], preferred_element_type=F32)
            y = part if y is None else y + part
        ys_ref[...] = y

    @pl.when(b == n_used - 1)
    def _():
        for ahead in range(1, MOE_AHEAD + 1):
            for r in range(MOE_BM):
                row_copy(b, (b + ahead) % (MOE_AHEAD + 1), r).wait()

    @pl.when(b >= n_used)
    def _():
        ys_ref[...] = jnp.zeros_like(ys_ref)


def _moe(block_expert, slot_tok, n_used, cnt, x1, wg, wu, wd):
    n_blocks = block_expert.shape[0]
    n_slots = slot_tok.shape[0]
    _, d, de = wg.shape
    any_spec = pl.BlockSpec(memory_space=pl.ANY)
    grid_spec = pltpu.PrefetchScalarGridSpec(
        num_scalar_prefetch=4,
        grid=(n_blocks,),
        in_specs=[any_spec, any_spec, any_spec, any_spec],
        out_specs=pl.BlockSpec((MOE_BM, d), lambda b, *_: (b, 0)),
        scratch_shapes=[pltpu.VMEM((MOE_AHEAD + 1, MOE_BM, d), F32), pltpu.SemaphoreType.DMA((MOE_AHEAD + 1,)),
                        pltpu.VMEM((MOE_WSLOTS, d, de), F32), pltpu.VMEM((MOE_WSLOTS, d, de), F32),
                        pltpu.VMEM((MOE_WSLOTS, de, d), F32), pltpu.SemaphoreType.DMA((MOE_WSLOTS,)),
                        pltpu.VMEM((d, 2 * de), BF16), pltpu.VMEM((de, d), BF16), pltpu.SMEM((1,), jnp.int32)],
    )
    return pl.pallas_call(
        _moe_kernel,
        grid_spec=grid_spec,
        out_shape=jax.ShapeDtypeStruct((n_slots, d), F32),
        compiler_params=_cparams(1),
        name="moe_ffn",
    )(block_expert, slot_tok, n_used, cnt, x1, wg, wu, wd)


def _combine_kernel(alpha, tile_off, n_tok, dst_ref, ys_hbm, x1_ref, rt_ref, g_ref, b_ref, out_ref,
                    buf_ref, sem_ref):
    i = pl.program_id(0)
    n = pl.num_programs(0)
    tm = x1_ref.shape[0]

    def row_copy(tile, slot, r, k):
        src = dst_ref[k * n_tok + (tile + tile_off) * tm + r]
        return pltpu.make_async_copy(ys_hbm.at[pl.ds(src, 1)], buf_ref.at[slot, k, pl.ds(r, 1)], sem_ref.at[slot])

    n_buf = CMB_AHEAD + 1

    @pl.when(i == 0)
    def _():
        for ahead in range(CMB_AHEAD):
            for r in range(tm):
                for k in range(TOP_K):
                    row_copy(jnp.minimum(ahead, n - 1), ahead, r, k).start()

    slot = i % n_buf
    for r in range(tm):
        for k in range(TOP_K):
            row_copy(i, slot, r, k).wait()
    nxt = jnp.minimum(i + CMB_AHEAD, n - 1)
    nxt_slot = (i + CMB_AHEAD) % n_buf
    for r0 in range(0, tm, CMB_RC):
        for r in range(r0, r0 + CMB_RC):
            for k in range(TOP_K):
                row_copy(nxt, nxt_slot, r, k).start()
        rt = rt_ref[r0:r0 + CMB_RC, :]
        f = (rt[:, ROUTE_W:ROUTE_W + 1] * buf_ref[slot, 0, r0:r0 + CMB_RC, :]
             + rt[:, ROUTE_W + 1:ROUTE_W + 2] * buf_ref[slot, 1, r0:r0 + CMB_RC, :])
        out_ref[r0:r0 + CMB_RC, :] = _layer_norm(alpha * x1_ref[r0:r0 + CMB_RC, :] + f, g_ref[...], b_ref[...])

    @pl.when(i == n - 1)
    def _():
        for ahead in range(1, n_buf):
            for r in range(tm):
                for k in range(TOP_K):
                    row_copy(i, (i + ahead) % n_buf, r, k).wait()


def _combine(tm, n_tiles, tile_off, alpha, dst, ys, x1, rt, g, b):
    n_tok, d = x1.shape
    grid_spec = pltpu.PrefetchScalarGridSpec(
        num_scalar_prefetch=1,
        grid=(n_tiles,),
        in_specs=[pl.BlockSpec(memory_space=pl.ANY),
                  pl.BlockSpec((tm, d), lambda i, *_: (i + tile_off, 0)),
                  pl.BlockSpec((tm, LANES), lambda i, *_: (i + tile_off, 0)),
                  pl.BlockSpec((1, d), lambda i, *_: (0, 0)),
                  pl.BlockSpec((1, d), lambda i, *_: (0, 0))],
        out_specs=pl.BlockSpec((tm, d), lambda i, *_: (i, 0)),
        scratch_shapes=[pltpu.VMEM((CMB_AHEAD + 1, TOP_K, tm, d), F32), pltpu.SemaphoreType.DMA((CMB_AHEAD + 1,))],
    )
    return pl.pallas_call(
        functools.partial(_combine_kernel, alpha, tile_off, n_tok),
        grid_spec=grid_spec,
        out_shape=jax.ShapeDtypeStruct((n_tiles * tm, d), F32),
        compiler_params=_cparams(1),
        name="combine",
    )(dst, ys, x1, rt, g, b)


def kernel(x_prompt, x_sample, state_conv_a, state_conv_b, meta_tokens, w_in, b_in, conv_a_w, conv_a_b, ln_a_g, ln_a_b, w_a_out, conv_b_w, w_b_out, w_o, ln1_g, ln1_b, w_router_group, b_router_group, w_router_expert, b_router_expert, w_exp_gate, w_exp_up, w_exp_down, ln2_g, ln2_b):
    depth = w_in.shape[0]
    assert depth == 1, "single-layer step only"
    n_seq, seq, d = x_prompt.shape
    n_s = x_sample.shape[0]
    n_meta = meta_tokens.shape[0]
    ka, kb = conv_a_w.shape[1], conv_b_w.shape[1]
    assert x_sample.shape[1] == 1 and seq % IN_TM == 0 and IN_TM >= ka - 1
    assert n_meta <= HIST_A and ka - 1 <= HIST_A and kb - 1 <= HIST_B and kb - 1 <= n_meta
    assert n_s + n_meta <= IN_TM and n_s % CMB_TM == 0
    alpha = (2.0 * depth) ** 0.25
    rp = n_seq * seq

    xp = x_prompt.reshape(rp, d)
    xs = jnp.concatenate([x_sample.reshape(n_s, d), meta_tokens,
                          jnp.zeros((IN_TM - n_s - n_meta, d), F32)], axis=0).astype(BF16)
    (ua_p, pb_p, sga_p, sgb_p, new_a_p, new_b_p, ga_x, cb_x, bg_x, sga_x, sgb_x) = _inproj_pipe(
        xp, xs, w_in[0], b_in, conv_a_w[0], conv_a_b, conv_b_w[0], n_seq, seq, n_s, n_meta)
    ua_s, pb_s, new_a_s, new_b_s = _conv_sample(
        jnp.transpose(state_conv_a[0], (1, 0, 2)), jnp.transpose(state_conv_b[0], (1, 0, 2)),
        ga_x, cb_x, bg_x, 0, conv_a_w[0], conv_a_b, conv_b_w[0])
    new_a_s = jnp.transpose(new_a_s, (1, 0, 2))[None]
    new_b_s = jnp.transpose(new_b_s, (1, 0, 2))[None]

    wr_f = jnp.concatenate([w_router_group[0], w_router_expert[0].transpose(1, 0, 2).reshape(d, N_EXPERTS),
                            jnp.zeros((d, LANES - N_GROUPS - N_EXPERTS), F32)], axis=1)
    br = jnp.concatenate([b_router_group[0], b_router_expert[0].reshape(-1),
                          jnp.zeros((LANES - N_GROUPS - N_EXPERTS,), F32)])[None, :]
    x1, rt, ri, cnt = _mixer(MIX_TM, rp // MIX_TM, alpha, ua_p, pb_p, sga_p, sgb_p, xp, w_a_out[0], w_b_out[0], w_o[0],
                             ln_a_g, ln_a_b, ln1_g, ln1_b, wr_f, br,
                             tail=(ua_s, pb_s, sga_x, sgb_x, x_sample.reshape(n_s, d)))

    cnt_i = cnt.astype(jnp.int32)
    ids = [ri[ROUTE_ID + k] for k in range(TOP_K)]
    ranks = [ri[ROUTE_RANK + k] for k in range(TOP_K)]
    block_expert, slot_tok, dest, n_used = _plan(ids, ranks, cnt_i)
    ys = _moe(block_expert, slot_tok, n_used, cnt_i, x1, w_exp_gate[0], w_exp_up[0], w_exp_down[0])
    y_p = _combine(CMB_TM, rp // CMB_TM, 0, alpha, dest, ys, x1, rt, ln2_g, ln2_b)
    y_s = _combine(CMB_TM, n_s // CMB_TM, rp // CMB_TM, alpha, dest, ys, x1, rt, ln2_g, ln2_b)

    return (y_p.reshape(n_seq, seq, d), y_s.reshape(n_s, 1, d), new_a_p[None], new_b_p[None], new_a_s, new_b_s)
```

```python
import functools

import jax
import jax.numpy as jnp
from jax import lax
from jax.experimental import pallas as pl
from jax.experimental.pallas import tpu as pltpu

F32 = jnp.float32
BF16 = jnp.bfloat16

LN_EPS = 1e-5
N_GROUPS = 4
EXPERTS_PER_GROUP = 8
N_EXPERTS = N_GROUPS * EXPERTS_PER_GROUP
TOP_K = 2
N_PROJ_BLOCKS = 7

VMEM_LIMIT_BYTES = 56 * 1024 * 1024
LANES = 128

IN_TM = 512
IN_TN = 256
IN_HM = 512
W_UNIT_ROWS = 1024
CONV_TC = 256
CONV_RC = 64
HIST_A = 32
HIST_B = 8
MIX_TM = 256
MIX_WROWS = 64
MIX_WSTAGE = 4
MOE_BM = 128
MOE_NC = 256
MOE_AHEAD = 4
MOE_WSLOTS = 2
WEIGHT_DMA_PRIORITY = 1
CMB_TM = 128
CMB_RC = 32
CMB_AHEAD = 3


def _cparams(n_axes, vmem_bytes=None):
    limit = VMEM_LIMIT_BYTES if vmem_bytes is None else min(int(vmem_bytes), VMEM_LIMIT_BYTES)
    return pltpu.CompilerParams(dimension_semantics=("arbitrary",) * n_axes, vmem_limit_bytes=limit)


def _sigmoid(x):
    return 1.0 / (1.0 + jnp.exp(-x))


def _layer_norm(x, g, b):
    mu = jnp.mean(x, axis=-1, keepdims=True)
    xc = x - mu
    var = jnp.mean(xc * xc, axis=-1, keepdims=True)
    return xc * lax.rsqrt(var + LN_EPS) * g + b


def _residues(n_taps, hist):
    return sorted({(hist - (n_taps - 1) + k) % 8 for k in range(n_taps)})


def _inproj_pipe_kernel(ka, kb, tiles_per_seq, n_tiles, n_s, n_meta, xp_ref, xs_ref, wnext_ref, w_hbm, *refs):
    b_refs = refs[0:7]
    wa_ref, ba_ref, wb_ref = refs[7:10]
    ua_ref, pb_ref, sga_ref, sgb_ref, na_ref, nb_ref = refs[10:16]
    gas_ref, cbs_ref, bgs_ref, sgas_ref, sgbs_ref = refs[16:21]
    wbf_ref, stage_ref, sem_ref, xbf_ref = refs[21:25]
    bufs = (refs[25:28], refs[28:31])
    ha_ref, hb_ref = refs[31:33]
    j = pl.program_id(0)
    i = pl.program_id(1)
    nj = pl.num_programs(0)
    tm, d = xbf_ref.shape
    tn = ha_ref.shape[-1]
    unit_rows = stage_ref.shape[1]
    units_per_block = d // unit_rows
    n_units = N_PROJ_BLOCKS * units_per_block
    t = i - 1

    @pl.when(jnp.logical_and(j == 0, i == 0))
    def _():
        def unit_copy(u):
            k, h = divmod(u, units_per_block)
            return pltpu.make_async_copy(
                w_hbm.at[pl.ds(h * unit_rows, unit_rows), pl.ds(k * d, tn)], stage_ref.at[u % 2], sem_ref.at[u % 2])

        unit_copy(0).start()
        for u in range(n_units):
            if u + 1 < n_units:
                unit_copy(u + 1).start()
            unit_copy(u).wait()
            k, h = divmod(u, units_per_block)
            wbf_ref[0, k, h * unit_rows:(h + 1) * unit_rows, :] = stage_ref[u % 2].astype(BF16)
        for buf in bufs:
            for ref in buf:
                ref[...] = jnp.zeros_like(ref)

    @pl.when(jnp.logical_and(i < n_units, j + 1 < nj))
    def _():
        k = i // units_per_block
        h = i % units_per_block
        row0 = pl.multiple_of(h * unit_rows, unit_rows)
        wbf_ref[(j + 1) % 2, k, pl.ds(row0, unit_rows), :] = wnext_ref[...].astype(BF16)

    @pl.when(i == 0)
    def _():
        xbf_ref[...] = xs_ref[...]

    @pl.when(jnp.logical_and(i > 0, i < n_tiles))
    def _():
        xbf_ref[...] = xp_ref[...].astype(BF16)

    res_a, res_b = _residues(ka, HIST_A), _residues(kb, HIST_B)
    meta_end = n_s + n_meta

    def raw_rows(ref, res, hist):
        return ref[res.index(0), hist:hist + tm, :]

    def last_rows(ref, res, hist, n_taps):
        rho = (hist - (n_taps - 1)) % 8
        start = hist + tm - (n_taps - 1) - rho
        return ref[res.index(rho), start:start + n_taps - 1, :]

    def set_history(ref, res, hist, src, row0):
        for q, rho in enumerate(res):
            if hist - rho > 0:
                ref[q, 0:hist - rho, :] = src[q, row0:row0 + hist - rho, :]

    @pl.when(i == 1)
    def _():
        sa_ref, sb_ref, bg_ref = bufs[0]
        gas_ref[...] = raw_rows(sa_ref, res_a, HIST_A)
        cbs_ref[...] = raw_rows(sb_ref, res_b, HIST_B)
        bgs_ref[...] = bg_ref[...]
        sgas_ref[...] = sga_ref[...]
        sgbs_ref[...] = sgb_ref[...]
        for h_ref, src, res, hist in ((ha_ref, sa_ref, res_a, HIST_A), (hb_ref, sb_ref, res_b, HIST_B)):
            for q, rho in enumerate(res):
                h_ref[q] = src[q, meta_end:meta_end + hist, :]
                n_zero = hist - n_meta - rho
                if n_zero > 0:
                    h_ref[q, 0:n_zero, :] = jnp.zeros((n_zero, tn), F32)

    seq_pos = (t - 1) % tiles_per_seq
    for par in range(2):
        sa_ref, sb_ref, _ = bufs[par]
        sa_prev, sb_prev, _ = bufs[1 - par]
        is_t = jnp.logical_and(t >= 1, t % 2 == par)

        @pl.when(jnp.logical_and(is_t, seq_pos == 0))
        def _():
            set_history(sa_ref, res_a, HIST_A, ha_ref, 0)
            set_history(sb_ref, res_b, HIST_B, hb_ref, 0)

        @pl.when(jnp.logical_and(is_t, seq_pos != 0))
        def _():
            set_history(sa_ref, res_a, HIST_A, sa_prev, tm)
            set_history(sb_ref, res_b, HIST_B, sb_prev, tm)

    slot = j % 2
    chunks = [(c0, r0) for c0 in range(0, tn, LANES) for r0 in range(0, tm, CONV_RC)]

    def conv_taps(ref, res, w_ref, n_taps, hist, r0, c0):
        acc = None
        for k in range(n_taps):
            o = hist - (n_taps - 1) + k
            term = w_ref[k:k + 1, c0:c0 + LANES] * ref[res.index(o % 8), pl.ds(r0 + o - o % 8, CONV_RC), c0:c0 + LANES]
            acc = term if acc is None else acc + term
        return acc

    def conv_items(buf):
        sa_ref, sb_ref, bg_ref = buf

        def conv_a(c0, r0):
            acc = conv_taps(sa_ref, res_a, wa_ref, ka, HIST_A, r0, c0)
            ua_ref[r0:r0 + CONV_RC, c0:c0 + LANES] = acc + ba_ref[:, c0:c0 + LANES]

        def conv_b(c0, r0):
            accb = conv_taps(sb_ref, res_b, wb_ref, kb, HIST_B, r0, c0)
            pb_ref[r0:r0 + CONV_RC, c0:c0 + LANES] = (bg_ref[r0:r0 + CONV_RC, c0:c0 + LANES] * accb).astype(BF16)

        return [[functools.partial(conv_a, c0, r0), functools.partial(conv_b, c0, r0)] for c0, r0 in chunks]

    def store_shifted(ref, res, hist, m0, value):
        for q, rho in enumerate(res):
            ref[q, pl.ds(hist + m0 - rho, value.shape[0]), :] = value

    def run(todo, n):
        for _ in range(min(n, len(todo))):
            for item in todo.pop(0):
                item()

    def project(buf, todo):
        sa_ref, sb_ref, bg_ref = buf

        def proj(k, m0):
            return (jnp.dot(xbf_ref[m0:m0 + IN_HM, :], wbf_ref[slot, k], preferred_element_type=F32)
                    + b_refs[k][...])

        run(todo, 1)
        for m0 in range(0, tm, IN_HM):
            p0 = proj(0, m0)
            run(todo, 1)
            p1 = proj(1, m0)
            run(todo, 1)
            store_shifted(sa_ref, res_a, HIST_A, m0, p0 * _sigmoid(p1))
            p3 = proj(3, m0)
            run(todo, 1)
            p4 = proj(4, m0)
            run(todo, 1)
            store_shifted(sb_ref, res_b, HIST_B, m0, p3 * p4)
            bg_ref[m0:m0 + IN_HM, :] = proj(2, m0)
            run(todo, 1)
            sga_ref[m0:m0 + IN_HM, :] = _sigmoid(proj(5, m0)).astype(BF16)
            run(todo, 1)
            sgb_ref[m0:m0 + IN_HM, :] = _sigmoid(proj(6, m0)).astype(BF16)
            run(todo, 1)
        run(todo, len(todo))

    for par in range(2):
        @pl.when(jnp.logical_and(i < n_tiles, i % 2 == par))
        def _():
            project(bufs[par], conv_items(bufs[1 - par]))

    @pl.when(i == n_tiles)
    def _():
        todo = conv_items(bufs[(n_tiles - 1) % 2])
        run(todo, len(todo))

    for par in range(2):
        sa_ref, sb_ref, _ = bufs[par]

        @pl.when(jnp.logical_and(jnp.logical_and(t >= 1, t % 2 == par), seq_pos == tiles_per_seq - 1))
        def _():
            na_ref[...] = last_rows(sa_ref, res_a, HIST_A, ka)
            nb_ref[...] = last_rows(sb_ref, res_b, HIST_B, kb)


def _inproj_pipe(xp, xs_bf, w, b, wa, ba, wb, n_seq, seq, n_s, n_meta):
    rp, d = xp.shape
    ka, kb = wa.shape[0], wb.shape[0]
    tiles_per_seq = seq // IN_TM
    n_prompt_tiles = rp // IN_TM
    n_tiles = n_prompt_tiles + 1
    nj = d // IN_TN
    units_per_block = d // W_UNIT_ROWS
    n_units = N_PROJ_BLOCKS * units_per_block
    last = n_prompt_tiles - 1

    def wnext_map(j, i):
        u = jnp.where(j == nj - 1, n_units - 1, jnp.minimum(i, n_units - 1))
        col = jnp.minimum(j + 1, nj - 1)
        return (u % units_per_block, (u // units_per_block) * nj + col)

    def projected(i):
        return jnp.clip(i - 1, 0, last)

    def convolved(i):
        return jnp.clip(i - 2, 0, last)

    b_specs = [pl.BlockSpec((1, IN_TN), functools.partial(lambda j, i, k: (0, k * nj + j), k=k))
               for k in range(N_PROJ_BLOCKS)]
    ch_spec = lambda rows: pl.BlockSpec((rows, IN_TN), lambda j, i: (0, j))
    proj_spec = pl.BlockSpec((IN_TM, IN_TN), lambda j, i: (projected(i), j))
    conv_spec = pl.BlockSpec((IN_TM, IN_TN), lambda j, i: (convolved(i), j))
    state_spec = lambda rows: pl.BlockSpec((None, rows, IN_TN), lambda j, i: (convolved(i) // tiles_per_seq, 0, j))
    small_spec = pl.BlockSpec((IN_TM, IN_TN), lambda j, i: (0, j))
    sds = jax.ShapeDtypeStruct
    n_res_a, n_res_b = len(_residues(ka, HIST_A)), len(_residues(kb, HIST_B))
    assert (n_s + n_meta) % 8 == 0 and n_s + n_meta + max(HIST_A, HIST_B) <= IN_TM
    tile_bufs = [pltpu.VMEM((n_res_a, HIST_A + IN_TM, IN_TN), F32), pltpu.VMEM((n_res_b, HIST_B + IN_TM, IN_TN), F32),
                 pltpu.VMEM((IN_TM, IN_TN), F32)]
    return pl.pallas_call(
        functools.partial(_inproj_pipe_kernel, ka, kb, tiles_per_seq, n_tiles, n_s, n_meta),
        grid=(nj, n_tiles + 1),
        in_specs=[pl.BlockSpec((IN_TM, d), lambda j, i: (projected(i), 0)),
                  pl.BlockSpec((IN_TM, d), lambda j, i: (0, 0)),
                  pl.BlockSpec((W_UNIT_ROWS, IN_TN), wnext_map),
                  pl.BlockSpec(memory_space=pl.ANY)] + b_specs + [ch_spec(ka), ch_spec(1), ch_spec(kb)],
        out_specs=[conv_spec, conv_spec, proj_spec, proj_spec, state_spec(ka - 1), state_spec(kb - 1),
                   small_spec, small_spec, small_spec, small_spec, small_spec],
        out_shape=[sds((rp, d), F32), sds((rp, d), BF16), sds((rp, d), BF16), sds((rp, d), BF16),
                   sds((n_seq, ka - 1, d), F32), sds((n_seq, kb - 1, d), F32),
                   sds((IN_TM, d), F32), sds((IN_TM, d), F32), sds((IN_TM, d), F32),
                   sds((IN_TM, d), BF16), sds((IN_TM, d), BF16)],
        scratch_shapes=[pltpu.VMEM((2, N_PROJ_BLOCKS, d, IN_TN), BF16),
                        pltpu.VMEM((2, W_UNIT_ROWS, IN_TN), F32), pltpu.SemaphoreType.DMA((2,)),
                        pltpu.VMEM((IN_TM, d), BF16)] + tile_bufs + tile_bufs + [
                        pltpu.VMEM((n_res_a, HIST_A, IN_TN), F32), pltpu.VMEM((n_res_b, HIST_B, IN_TN), F32)],
        compiler_params=_cparams(2),
        name="inproj_conv",
    )(xp, xs_bf, w, w, *([b] * N_PROJ_BLOCKS), wa, ba, wb)


def _conv_sample_kernel(ka, kb, sta_ref, stb_ref, ga_ref, cb_ref, bg_ref, wa_ref, ba_ref, wb_ref,
                        ua_ref, pb_ref, nsa_ref, nsb_ref):
    ga = ga_ref[...]
    cb = cb_ref[...]
    acc = wa_ref[ka - 1:ka, :] * ga
    for k in range(ka - 1):
        acc = acc + wa_ref[k:k + 1, :] * sta_ref[k]
    ua_ref[...] = acc + ba_ref[...]
    accb = wb_ref[kb - 1:kb, :] * cb
    for k in range(kb - 1):
        accb = accb + wb_ref[k:k + 1, :] * stb_ref[k]
    pb_ref[...] = (bg_ref[...] * accb).astype(BF16)
    for k in range(ka - 2):
        nsa_ref[k] = sta_ref[k + 1]
    nsa_ref[ka - 2] = ga
    for k in range(kb - 2):
        nsb_ref[k] = stb_ref[k + 1]
    nsb_ref[kb - 2] = cb


def _conv_sample(state_a, state_b, ga, cb, bg, row_block, wa, ba, wb):
    _, n, d = state_a.shape
    ka, kb = wa.shape[0], wb.shape[0]
    row_spec = pl.BlockSpec((n, CONV_TC), lambda c: (row_block, c))
    out_spec = pl.BlockSpec((n, CONV_TC), lambda c: (0, c))
    state_spec = lambda rows: pl.BlockSpec((rows, n, CONV_TC), lambda c: (0, 0, c))
    return pl.pallas_call(
        functools.partial(_conv_sample_kernel, ka, kb),
        grid=(d // CONV_TC,),
        in_specs=[state_spec(ka - 1), state_spec(kb - 1),
                  row_spec, row_spec, row_spec,
                  pl.BlockSpec((ka, CONV_TC), lambda c: (0, c)),
                  pl.BlockSpec((1, CONV_TC), lambda c: (0, c)),
                  pl.BlockSpec((kb, CONV_TC), lambda c: (0, c))],
        out_specs=[out_spec, out_spec, state_spec(ka - 1), state_spec(kb - 1)],
        out_shape=[jax.ShapeDtypeStruct((n, d), F32), jax.ShapeDtypeStruct((n, d), BF16),
                   jax.ShapeDtypeStruct(state_a.shape, F32), jax.ShapeDtypeStruct(state_b.shape, F32)],
        compiler_params=_cparams(1),
        name="conv_sample",
    )(state_a, state_b, ga, cb, bg, wa, ba, wb)


ROUTE_ID, ROUTE_RANK, ROUTE_W = 0, TOP_K, 2 * TOP_K


def _route_tile(lg, carry):
    tm = lg.shape[0]
    lane = lax.broadcasted_iota(jnp.int32, (tm, LANES), 1)
    neg_inf = jnp.float32(-jnp.inf)

    def first_max(v):
        m = jnp.max(v, axis=-1, keepdims=True)
        return m, jnp.min(jnp.where(v == m, lane, LANES), axis=-1, keepdims=True)

    g_mask = lane < N_GROUPS
    g_max, g_sel = first_max(jnp.where(g_mask, lg, neg_inf))
    g_w = 1.0 / jnp.sum(jnp.where(g_mask, jnp.exp(lg - g_max), 0.0), axis=-1, keepdims=True)
    lane0 = N_GROUPS + g_sel * EXPERTS_PER_GROUP
    e_lg = jnp.where(jnp.logical_and(lane >= lane0, lane < lane0 + EXPERTS_PER_GROUP), lg, neg_inf)
    m1, l1 = first_max(e_lg)
    m2, l2 = first_max(jnp.where(lane == l1, neg_inf, e_lg))
    r = jnp.exp(m2 - m1)
    c1 = g_w / (1.0 + r)
    c2 = g_w * r / (1.0 + r)

    a1 = lane == l1
    a2 = lane == l2
    hit = jnp.where(jnp.logical_or(a1, a2), 1.0, 0.0)
    row = lax.broadcasted_iota(jnp.int32, (tm, tm), 0)
    col = lax.broadcasted_iota(jnp.int32, (tm, tm), 1)
    before = jnp.where(col < row, 1.0, 0.0).astype(BF16)
    seen = jnp.dot(before, hit.astype(BF16), preferred_element_type=F32) + carry
    rank1 = jnp.sum(jnp.where(a1, seen, 0.0), axis=-1, keepdims=True)
    rank2 = jnp.sum(jnp.where(a2, seen, 0.0), axis=-1, keepdims=True)
    carry = carry + jnp.sum(hit, axis=0, keepdims=True)

    rec = jnp.zeros((tm, LANES), F32)
    fields = [(l1 - N_GROUPS).astype(F32), (l2 - N_GROUPS).astype(F32), rank1, rank2, c1, c2]
    for n, v in enumerate(fields):
        rec = jnp.where(lane == n, v, rec)
    return rec, carry


def _mixer_kernel(alpha, n_tiles, ua_ref, pb_ref, sga_ref, sgb_ref, x_ref, wa_hbm, wb_hbm, wo_hbm,
                  lnag_ref, lnab_ref, ln1g_ref, ln1b_ref, wrf_ref, br_ref, *refs):
    x1_ref, rt_ref, ri_ref, cnt_ref = refs[-11:-7]
    carry_ref, wa_ref, wb_ref, wo_ref, wr_ref, stage_ref, sem_ref = refs[-7:]
    i = pl.program_id(0)

    @pl.when(i == 0)
    def _():
        carry_ref[...] = jnp.zeros_like(carry_ref)
        n_stage, rows = stage_ref.shape[0], stage_ref.shape[1]
        jobs = [(src, dst, r0) for src, dst in ((wa_hbm, wa_ref), (wb_hbm, wb_ref), (wo_hbm, wo_ref))
                for r0 in range(0, src.shape[0], rows)]

        def chunk_copy(n):
            src, _, r0 = jobs[n]
            return pltpu.make_async_copy(src.at[pl.ds(r0, rows)], stage_ref.at[n % n_stage], sem_ref.at[n % n_stage])

        for n in range(n_stage - 1):
            chunk_copy(n).start()
        for n, (_, dst, r0) in enumerate(jobs):
            if n + n_stage - 1 < len(jobs):
                chunk_copy(n + n_stage - 1).start()
            chunk_copy(n).wait()
            dst[r0:r0 + rows, :] = stage_ref[n % n_stage].astype(BF16)
        wr = wrf_ref[...]
        wr_hi = wr.astype(BF16)
        wr_ref[:, 0:LANES] = wr_hi
        wr_ref[:, LANES:2 * LANES] = (wr - wr_hi.astype(F32)).astype(BF16)

    def tile(rows, ua_t, pb_t, sga_t, sgb_t, x_t):
        yb = jnp.dot(pb_t[...], wb_ref[...], preferred_element_type=F32)
        un = _layer_norm(ua_t[...], lnag_ref[...], lnab_ref[...])
        act = (un * _sigmoid(un)).astype(BF16)
        ya = jnp.dot(act, wa_ref[...], preferred_element_type=F32)
        m = (sga_t[...].astype(F32) * ya + sgb_t[...].astype(F32) * yb).astype(BF16)
        mixed = jnp.dot(m, wo_ref[...], preferred_element_type=F32)
        x1 = _layer_norm(alpha * x_t[...] + mixed, ln1g_ref[...], ln1b_ref[...])
        x1_ref[0:rows, :] = x1
        hi = x1.astype(BF16)
        lo = (x1 - hi.astype(F32)).astype(BF16)
        a = jnp.dot(hi, wr_ref[...], preferred_element_type=F32)
        b = jnp.dot(lo, wr_ref[...], preferred_element_type=F32)
        lg = a[:, :LANES] + a[:, LANES:] + b[:, :LANES] + br_ref[...]
        rec, carry = _route_tile(lg, carry_ref[...])
        carry_ref[...] = carry
        rt_ref[0:rows, :] = rec
        ri_ref[:, 0:rows] = rec.T[0:2 * TOP_K, :].astype(jnp.int32)

    @pl.when(i < n_tiles)
    def _():
        tile(x_ref.shape[0], ua_ref, pb_ref, sga_ref, sgb_ref, x_ref)

    if len(refs) == 16:
        tail_refs = refs[:5]

        @pl.when(i == n_tiles)
        def _():
            tile(tail_refs[0].shape[0], *tail_refs)

    cnt_ref[...] = carry_ref[...]


def _mixer(tm, n_tiles, alpha, ua, pb, sga, sgb, x, wa, wb, wo, lnag, lnab, ln1g, ln1b, wr, br, tail=None):
    d = x.shape[1]
    last = n_tiles - 1
    in_spec = pl.BlockSpec((tm, d), lambda i: (jnp.minimum(i, last), 0))
    vec_spec = pl.BlockSpec((1, d), lambda i: (0, 0))
    lane_spec = pl.BlockSpec((1, LANES), lambda i: (0, 0))
    w_spec = pl.BlockSpec(memory_space=pl.ANY)
    in_specs = [in_spec, in_spec, in_spec, in_spec, in_spec, w_spec, w_spec, w_spec,
                vec_spec, vec_spec, vec_spec, vec_spec,
                pl.BlockSpec((d, LANES), lambda i: (0, 0)), lane_spec]
    args = [ua, pb, sga, sgb, x, wa, wb, wo, lnag, lnab, ln1g, ln1b, wr, br]
    n_rows, n_steps, n_tail = n_tiles * tm, n_tiles, 0
    if tail is not None:
        n_tail = tail[0].shape[0]
        assert n_tail <= tm and all(a.shape[0] >= n_tail for a in tail)
        in_specs += [pl.BlockSpec((n_tail, d), lambda i: (0, 0), pipeline_mode=pl.Buffered(1))] * len(tail)
        args += list(tail)
        n_rows, n_steps = n_rows + n_tail, n_steps + 1
    return pl.pallas_call(
        functools.partial(_mixer_kernel, alpha, n_tiles),
        grid=(n_steps,),
        in_specs=in_specs,
        out_specs=[pl.BlockSpec((tm, d), lambda i: (i, 0)), pl.BlockSpec((tm, LANES), lambda i: (i, 0)),
                   pl.BlockSpec((2 * TOP_K, tm), lambda i: (0, i)), lane_spec],
        out_shape=[jax.ShapeDtypeStruct((n_rows, d), F32), jax.ShapeDtypeStruct((n_rows, LANES), F32),
                   jax.ShapeDtypeStruct((2 * TOP_K, n_rows), jnp.int32), jax.ShapeDtypeStruct((1, LANES), F32)],
        scratch_shapes=[pltpu.VMEM((1, LANES), F32), pltpu.VMEM((d, d), BF16), pltpu.VMEM((d, d), BF16),
                        pltpu.VMEM((d, d), BF16), pltpu.VMEM((d, 2 * LANES), BF16),
                        pltpu.VMEM((MIX_WSTAGE, MIX_WROWS, d), F32), pltpu.SemaphoreType.DMA((MIX_WSTAGE,))],
        compiler_params=_cparams(1, 3 * d * d * 2 + MIX_WSTAGE * MIX_WROWS * d * 4 + (2 * 18 + 12) * tm * d
                                 + 18 * n_tail * d + (2 << 20)),
        name="mixer",
    )(*args)


def _plan_kernel(n_tok, n_blocks, *refs):
    id_refs = refs[0:TOP_K]
    rank_refs = refs[TOP_K:2 * TOP_K]
    cnt_ref, be_ref, tok_ref, dst_ref, nu_ref, start_ref = refs[2 * TOP_K:]
    shift = MOE_BM.bit_length() - 1

    def per_expert(e, blk0):
        cnt = cnt_ref[0, N_GROUPS + e]
        nb = lax.shift_right_logical(cnt + (MOE_BM - 1), shift)
        start_ref[e] = blk0 * MOE_BM

        def fill(j, carry):
            be_ref[blk0 + j] = e
            return carry

        lax.fori_loop(0, nb, fill, 0)

        def pad(s, carry):
            tok_ref[s] = 0
            return carry

        lax.fori_loop(blk0 * MOE_BM + cnt, (blk0 + nb) * MOE_BM, pad, 0)
        return blk0 + nb

    n_used = lax.fori_loop(0, N_EXPERTS, per_expert, 0)
    nu_ref[0] = n_used

    def rest(b, carry):
        be_ref[b] = N_EXPERTS - 1

        def pad(s, c):
            tok_ref[b * MOE_BM + s] = 0
            return c

        lax.fori_loop(0, MOE_BM, pad, 0, unroll=8)
        return carry

    lax.fori_loop(n_used, n_blocks, rest, 0)

    def place(t, carry):
        for k in range(TOP_K):
            slot = start_ref[id_refs[k][t]] + rank_refs[k][t]
            dst_ref[k * n_tok + t] = slot
            tok_ref[slot] = t
        return carry

    lax.fori_loop(0, n_tok, place, 0, unroll=8)


def _plan(ids, ranks, cnt):
    n_tok = ids[0].shape[0]
    n_blocks = -(-n_tok * TOP_K // MOE_BM) + N_EXPERTS
    smem = pl.BlockSpec(memory_space=pltpu.SMEM)
    return pl.pallas_call(
        functools.partial(_plan_kernel, n_tok, n_blocks),
        in_specs=[smem] * (2 * TOP_K + 1),
        out_specs=[smem, smem, smem, smem],
        out_shape=[jax.ShapeDtypeStruct((n_blocks,), jnp.int32), jax.ShapeDtypeStruct((n_blocks * MOE_BM,), jnp.int32),
                   jax.ShapeDtypeStruct((TOP_K * n_tok,), jnp.int32), jax.ShapeDtypeStruct((1,), jnp.int32)],
        scratch_shapes=[pltpu.SMEM((N_EXPERTS,), jnp.int32)],
        name="plan",
    )(*ids, *ranks, cnt)


def _moe_kernel(be_ref, tok_ref, nused_ref, cnt_ref, x_hbm, wg_hbm, wu_hbm, wd_hbm, ys_ref,
                xbuf_ref, gsem_ref, wgf_ref, wuf_ref, wdf_ref, wsem_ref, wgu_ref, wdb_ref, ord_ref):
    b = pl.program_id(0)
    n_used = nused_ref[0]
    de2 = wgu_ref.shape[1]
    de = de2 // 2
    shift = MOE_BM.bit_length() - 1

    def weight_copies(e, slot):
        return [pltpu.make_async_copy(wg_hbm.at[e], wgf_ref.at[slot], wsem_ref.at[slot]),
                pltpu.make_async_copy(wu_hbm.at[e], wuf_ref.at[slot], wsem_ref.at[slot]),
                pltpu.make_async_copy(wd_hbm.at[e], wdf_ref.at[slot], wsem_ref.at[slot])]

    def row_copy(blk, slot, r):
        return pltpu.make_async_copy(x_hbm.at[pl.ds(tok_ref[blk * MOE_BM + r], 1)],
                                     xbuf_ref.at[slot, pl.ds(r, 1)], gsem_ref.at[slot])

    n_wslots = wgf_ref.shape[0]
    last_blk = be_ref.shape[0] - 1

    def blocks_of(e):
        return lax.shift_right_logical(cnt_ref[0, N_GROUPS + e] + (MOE_BM - 1), shift)

    @pl.when(jnp.logical_and(b == 0, n_used > 0))
    def _():
        ord_ref[0] = 0
        blk = jnp.int32(0)
        for ahead in range(n_wslots - 1):
            e_ahead = be_ref[jnp.minimum(blk, last_blk)]

            @pl.when(blk < n_used)
            def _():
                for cp in weight_copies(e_ahead, ahead):
                    cp.start(priority=WEIGHT_DMA_PRIORITY)

            blk = blk + blocks_of(e_ahead)
        for ahead in range(MOE_AHEAD):
            for r in range(MOE_BM):
                row_copy(jnp.minimum(ahead, n_used - 1), ahead, r).start()

    @pl.when(b < n_used)
    def _():
        e = be_ref[b]

        @pl.when(jnp.logical_or(b == 0, e != be_ref[jnp.maximum(b - 1, 0)]))
        def _():
            order = ord_ref[0]
            wslot = order % n_wslots
            for cp in weight_copies(e, wslot):
                cp.wait()
            blk = b
            for _ in range(n_wslots - 1):
                blk = blk + blocks_of(be_ref[jnp.minimum(blk, last_blk)])
                blk = jnp.minimum(blk, n_used)

            @pl.when(blk < n_used)
            def _():
                for cp in weight_copies(be_ref[jnp.minimum(blk, last_blk)], (order + n_wslots - 1) % n_wslots):
                    cp.start(priority=WEIGHT_DMA_PRIORITY)

            wgu_ref[:, 0:de] = wgf_ref[wslot].astype(BF16)
            wgu_ref[:, de:de2] = wuf_ref[wslot].astype(BF16)
            wdb_ref[...] = wdf_ref[wslot].astype(BF16)
            ord_ref[0] = order + 1

        n_buf = MOE_AHEAD + 1
        slot = b % n_buf
        for r in range(MOE_BM):
            row_copy(b, slot, r).wait()
        nxt_blk = jnp.minimum(b + MOE_AHEAD, n_used - 1)
        nxt_slot = (b + MOE_AHEAD) % n_buf
        xb = xbuf_ref[slot].astype(BF16)
        n_chunks = de2 // MOE_NC
        per = MOE_BM // n_chunks
        gu = []
        for c in range(n_chunks):
            for r in range(c * per, (c + 1) * per):
                row_copy(nxt_blk, nxt_slot, r).start()
            gu.append(jnp.dot(xb, wgu_ref[:, c * MOE_NC:(c + 1) * MOE_NC], preferred_element_type=F32))
        half = n_chunks // 2
        y = None
        for c in range(half):
            g = gu[c]
            h = (g * _sigmoid(g) * gu[half + c]).astype(BF16)
            part = jnp.dot(h, wdb_ref[c * MOE_NC:(c + 1) * MOE_NC, :], preferred_element_type=F32)
            y = part if y is None else y + part
        ys_ref[...] = y

    @pl.when(b == n_used - 1)
    def _():
        for ahead in range(1, MOE_AHEAD + 1):
            for r in range(MOE_BM):
                row_copy(b, (b + ahead) % (MOE_AHEAD + 1), r).wait()

    @pl.when(b >= n_used)
    def _():
        ys_ref[...] = jnp.zeros_like(ys_ref)


def _moe(block_expert, slot_tok, n_used, cnt, x1, wg, wu, wd):
    n_blocks = block_expert.shape[0]
    n_slots = slot_tok.shape[0]
    _, d, de = wg.shape
    any_spec = pl.BlockSpec(memory_space=pl.ANY)
    grid_spec = pltpu.PrefetchScalarGridSpec(
        num_scalar_prefetch=4,
        grid=(n_blocks,),
        in_specs=[any_spec, any_spec, any_spec, any_spec],
        out_specs=pl.BlockSpec((MOE_BM, d), lambda b, *_: (b, 0)),
        scratch_shapes=[pltpu.VMEM((MOE_AHEAD + 1, MOE_BM, d), F32), pltpu.SemaphoreType.DMA((MOE_AHEAD + 1,)),
                        pltpu.VMEM((MOE_WSLOTS, d, de), F32), pltpu.VMEM((MOE_WSLOTS, d, de), F32),
                        pltpu.VMEM((MOE_WSLOTS, de, d), F32), pltpu.SemaphoreType.DMA((MOE_WSLOTS,)),
                        pltpu.VMEM((d, 2 * de), BF16), pltpu.VMEM((de, d), BF16), pltpu.SMEM((1,), jnp.int32)],
    )
    return pl.pallas_call(
        _moe_kernel,
        grid_spec=grid_spec,
        out_shape=jax.ShapeDtypeStruct((n_slots, d), F32),
        compiler_params=_cparams(1),
        name="moe_ffn",
    )(block_expert, slot_tok, n_used, cnt, x1, wg, wu, wd)


def _combine_kernel(alpha, tile_off, n_tok, dst_ref, ys_hbm, x1_ref, rt_ref, g_ref, b_ref, out_ref,
                    buf_ref, sem_ref):
    i = pl.program_id(0)
    n = pl.num_programs(0)
    tm = x1_ref.shape[0]

    def row_copy(tile, slot, r, k):
        src = dst_ref[k * n_tok + (tile + tile_off) * tm + r]
        return pltpu.make_async_copy(ys_hbm.at[pl.ds(src, 1)], buf_ref.at[slot, k, pl.ds(r, 1)], sem_ref.at[slot])

    n_buf = CMB_AHEAD + 1

    @pl.when(i == 0)
    def _():
        for ahead in range(CMB_AHEAD):
            for r in range(tm):
                for k in range(TOP_K):
                    row_copy(jnp.minimum(ahead, n - 1), ahead, r, k).start()

    slot = i % n_buf
    for r in range(tm):
        for k in range(TOP_K):
            row_copy(i, slot, r, k).wait()
    nxt = jnp.minimum(i + CMB_AHEAD, n - 1)
    nxt_slot = (i + CMB_AHEAD) % n_buf
    for r0 in range(0, tm, CMB_RC):
        for r in range(r0, r0 + CMB_RC):
            for k in range(TOP_K):
                row_copy(nxt, nxt_slot, r, k).start()
        rt = rt_ref[r0:r0 + CMB_RC, :]
        f = (rt[:, ROUTE_W:ROUTE_W + 1] * buf_ref[slot, 0, r0:r0 + CMB_RC, :]
             + rt[:, ROUTE_W + 1:ROUTE_W + 2] * buf_ref[slot, 1, r0:r0 + CMB_RC, :])
        out_ref[r0:r0 + CMB_RC, :] = _layer_norm(alpha * x1_ref[r0:r0 + CMB_RC, :] + f, g_ref[...], b_ref[...])

    @pl.when(i == n - 1)
    def _():
        for ahead in range(1, n_buf):
            for r in range(tm):
                for k in range(TOP_K):
                    row_copy(i, (i + ahead) % n_buf, r, k).wait()


def _combine(tm, n_tiles, tile_off, alpha, dst, ys, x1, rt, g, b):
    n_tok, d = x1.shape
    grid_spec = pltpu.PrefetchScalarGridSpec(
        num_scalar_prefetch=1,
        grid=(n_tiles,),
        in_specs=[pl.BlockSpec(memory_space=pl.ANY),
                  pl.BlockSpec((tm, d), lambda i, *_: (i + tile_off, 0)),
                  pl.BlockSpec((tm, LANES), lambda i, *_: (i + tile_off, 0)),
                  pl.BlockSpec((1, d), lambda i, *_: (0, 0)),
                  pl.BlockSpec((1, d), lambda i, *_: (0, 0))],
        out_specs=pl.BlockSpec((tm, d), lambda i, *_: (i, 0)),
        scratch_shapes=[pltpu.VMEM((CMB_AHEAD + 1, TOP_K, tm, d), F32), pltpu.SemaphoreType.DMA((CMB_AHEAD + 1,))],
    )
    return pl.pallas_call(
        functools.partial(_combine_kernel, alpha, tile_off, n_tok),
        grid_spec=grid_spec,
        out_shape=jax.ShapeDtypeStruct((n_tiles * tm, d), F32),
        compiler_params=_cparams(1),
        name="combine",
    )(dst, ys, x1, rt, g, b)


def kernel(x_prompt, x_sample, state_conv_a, state_conv_b, meta_tokens, w_in, b_in, conv_a_w, conv_a_b, ln_a_g, ln_a_b, w_a_out, conv_b_w, w_b_out, w_o, ln1_g, ln1_b, w_router_group, b_router_group, w_router_expert, b_router_expert, w_exp_gate, w_exp_up, w_exp_down, ln2_g, ln2_b):
    depth = w_in.shape[0]
    assert depth == 1, "single-layer step only"
    n_seq, seq, d = x_prompt.shape
    n_s = x_sample.shape[0]
    n_meta = meta_tokens.shape[0]
    ka, kb = conv_a_w.shape[1], conv_b_w.shape[1]
    assert x_sample.shape[1] == 1 and seq % IN_TM == 0 and IN_TM >= ka - 1
    assert n_meta <= HIST_A and ka - 1 <= HIST_A and kb - 1 <= HIST_B and kb - 1 <= n_meta
    assert n_s + n_meta <= IN_TM and n_s % CMB_TM == 0
    alpha = (2.0 * depth) ** 0.25
    rp = n_seq * seq

    xp = x_prompt.reshape(rp, d)
    xs = jnp.concatenate([x_sample.reshape(n_s, d), meta_tokens,
                          jnp.zeros((IN_TM - n_s - n_meta, d), F32)], axis=0).astype(BF16)
    (ua_p, pb_p, sga_p, sgb_p, new_a_p, new_b_p, ga_x, cb_x, bg_x, sga_x, sgb_x) = _inproj_pipe(
        xp, xs, w_in[0], b_in, conv_a_w[0], conv_a_b, conv_b_w[0], n_seq, seq, n_s, n_meta)
    ua_s, pb_s, new_a_s, new_b_s = _conv_sample(
        jnp.transpose(state_conv_a[0], (1, 0, 2)), jnp.transpose(state_conv_b[0], (1, 0, 2)),
        ga_x, cb_x, bg_x, 0, conv_a_w[0], conv_a_b, conv_b_w[0])
    new_a_s = jnp.transpose(new_a_s, (1, 0, 2))[None]
    new_b_s = jnp.transpose(new_b_s, (1, 0, 2))[None]

    wr_f = jnp.concatenate([w_router_group[0], w_router_expert[0].transpose(1, 0, 2).reshape(d, N_EXPERTS),
                            jnp.zeros((d, LANES - N_GROUPS - N_EXPERTS), F32)], axis=1)
    br = jnp.concatenate([b_router_group[0], b_router_expert[0].reshape(-1),
                          jnp.zeros((LANES - N_GROUPS - N_EXPERTS,), F32)])[None, :]
    x1, rt, ri, cnt = _mixer(MIX_TM, rp // MIX_TM, alpha, ua_p, pb_p, sga_p, sgb_p, xp, w_a_out[0], w_b_out[0], w_o[0],
                             ln_a_g, ln_a_b, ln1_g, ln1_b, wr_f, br,
                             tail=(ua_s, pb_s, sga_x, sgb_x, x_sample.reshape(n_s, d)))

    cnt_i = cnt.astype(jnp.int32)
    ids = [ri[ROUTE_ID + k] for k in range(TOP_K)]
    ranks = [ri[ROUTE_RANK + k] for k in range(TOP_K)]
    block_expert, slot_tok, dest, n_used = _plan(ids, ranks, cnt_i)
    ys = _moe(block_expert, slot_tok, n_used, cnt_i, x1, w_exp_gate[0], w_exp_up[0], w_exp_down[0])
    y_p = _combine(CMB_TM, rp // CMB_TM, 0, alpha, dest, ys, x1, rt, ln2_g, ln2_b)
    y_s = _combine(CMB_TM, n_s // CMB_TM, rp // CMB_TM, alpha, dest, ys, x1, rt, ln2_g, ln2_b)

    return (y_p.reshape(n_seq, seq, d), y_s.reshape(n_s, 1, d), new_a_p[None], new_b_p[None], new_a_s, new_b_s)
```

```python
import functools

import jax
import jax.numpy as jnp
from jax import lax
from jax.experimental import pallas as pl
from jax.experimental.pallas import tpu as pltpu

F32 = jnp.float32
BF16 = jnp.bfloat16

LN_EPS = 1e-5
N_GROUPS = 4
EXPERTS_PER_GROUP = 8
N_EXPERTS = N_GROUPS * EXPERTS_PER_GROUP
TOP_K = 2
N_PROJ_BLOCKS = 7

VMEM_LIMIT_BYTES = 56 * 1024 * 1024
LANES = 128

IN_TM = 512
IN_TN = 256
IN_HM = 512
W_UNIT_ROWS = 1024
CONV_TC = 256
CONV_RC = 32
HIST_A = 32
HIST_B = 8
MIX_TM = 256
MIX_WROWS = 64
MIX_WSTAGE = 4
MOE_BM = 128
MOE_NC = 256
MOE_AHEAD = 4
MOE_WSLOTS = 2
WEIGHT_DMA_PRIORITY = 1
CMB_TM = 128
CMB_RC = 32
CMB_AHEAD = 3


def _cparams(n_axes, vmem_bytes=None):
    limit = VMEM_LIMIT_BYTES if vmem_bytes is None else min(int(vmem_bytes), VMEM_LIMIT_BYTES)
    return pltpu.CompilerParams(dimension_semantics=("arbitrary",) * n_axes, vmem_limit_bytes=limit)


def _sigmoid(x):
    return 1.0 / (1.0 + jnp.exp(-x))


def _layer_norm(x, g, b):
    mu = jnp.mean(x, axis=-1, keepdims=True)
    xc = x - mu
    var = jnp.mean(xc * xc, axis=-1, keepdims=True)
    return xc * lax.rsqrt(var + LN_EPS) * g + b


def _residues(n_taps, hist):
    return sorted({(hist - (n_taps - 1) + k) % 8 for k in range(n_taps)})


def _inproj_pipe_kernel(ka, kb, tiles_per_seq, n_tiles, n_s, n_meta, xp_ref, xs_ref, wnext_ref, w_hbm, *refs):
    b_refs = refs[0:7]
    wa_ref, ba_ref, wb_ref = refs[7:10]
    ua_ref, pb_ref, sga_ref, sgb_ref, na_ref, nb_ref = refs[10:16]
    gas_ref, cbs_ref, bgs_ref, sgas_ref, sgbs_ref = refs[16:21]
    wbf_ref, stage_ref, sem_ref, xbf_ref = refs[21:25]
    bufs = (refs[25:28], refs[28:31])
    ha_ref, hb_ref = refs[31:33]
    j = pl.program_id(0)
    i = pl.program_id(1)
    nj = pl.num_programs(0)
    tm, d = xbf_ref.shape
    tn = ha_ref.shape[-1]
    unit_rows = stage_ref.shape[1]
    units_per_block = d // unit_rows
    n_units = N_PROJ_BLOCKS * units_per_block
    t = i - 1

    @pl.when(jnp.logical_and(j == 0, i == 0))
    def _():
        def unit_copy(u):
            k, h = divmod(u, units_per_block)
            return pltpu.make_async_copy(
                w_hbm.at[pl.ds(h * unit_rows, unit_rows), pl.ds(k * d, tn)], stage_ref.at[u % 2], sem_ref.at[u % 2])

        unit_copy(0).start()
        for u in range(n_units):
            if u + 1 < n_units:
                unit_copy(u + 1).start()
            unit_copy(u).wait()
            k, h = divmod(u, units_per_block)
            wbf_ref[0, k, h * unit_rows:(h + 1) * unit_rows, :] = stage_ref[u % 2].astype(BF16)
        for buf in bufs:
            for ref in buf:
                ref[...] = jnp.zeros_like(ref)

    @pl.when(jnp.logical_and(i < n_units, j + 1 < nj))
    def _():
        k = i // units_per_block
        h = i % units_per_block
        row0 = pl.multiple_of(h * unit_rows, unit_rows)
        wbf_ref[(j + 1) % 2, k, pl.ds(row0, unit_rows), :] = wnext_ref[...].astype(BF16)

    @pl.when(i == 0)
    def _():
        xbf_ref[...] = xs_ref[...]

    @pl.when(jnp.logical_and(i > 0, i < n_tiles))
    def _():
        xbf_ref[...] = xp_ref[...].astype(BF16)

    res_a, res_b = _residues(ka, HIST_A), _residues(kb, HIST_B)
    meta_end = n_s + n_meta

    def raw_rows(ref, res, hist):
        return ref[res.index(0), hist:hist + tm, :]

    def last_rows(ref, res, hist, n_taps):
        rho = (hist - (n_taps - 1)) % 8
        start = hist + tm - (n_taps - 1) - rho
        return ref[res.index(rho), start:start + n_taps - 1, :]

    def set_history(ref, res, hist, src, row0):
        for q, rho in enumerate(res):
            if hist - rho > 0:
                ref[q, 0:hist - rho, :] = src[q, row0:row0 + hist - rho, :]

    @pl.when(i == 1)
    def _():
        sa_ref, sb_ref, bg_ref = bufs[0]
        gas_ref[...] = raw_rows(sa_ref, res_a, HIST_A)
        cbs_ref[...] = raw_rows(sb_ref, res_b, HIST_B)
        bgs_ref[...] = bg_ref[...]
        sgas_ref[...] = sga_ref[...]
        sgbs_ref[...] = sgb_ref[...]
        for h_ref, src, res, hist in ((ha_ref, sa_ref, res_a, HIST_A), (hb_ref, sb_ref, res_b, HIST_B)):
            for q, rho in enumerate(res):
                h_ref[q] = src[q, meta_end:meta_end + hist, :]
                n_zero = hist - n_meta - rho
                if n_zero > 0:
                    h_ref[q, 0:n_zero, :] = jnp.zeros((n_zero, tn), F32)

    seq_pos = (t - 1) % tiles_per_seq
    for par in range(2):
        sa_ref, sb_ref, _ = bufs[par]
        sa_prev, sb_prev, _ = bufs[1 - par]
        is_t = jnp.logical_and(t >= 1, t % 2 == par)

        @pl.when(jnp.logical_and(is_t, seq_pos == 0))
        def _():
            set_history(sa_ref, res_a, HIST_A, ha_ref, 0)
            set_history(sb_ref, res_b, HIST_B, hb_ref, 0)

        @pl.when(jnp.logical_and(is_t, seq_pos != 0))
        def _():
            set_history(sa_ref, res_a, HIST_A, sa_prev, tm)
            set_history(sb_ref, res_b, HIST_B, sb_prev, tm)

    slot = j % 2
    chunks = [(c0, r0) for c0 in range(0, tn, LANES) for r0 in range(0, tm, CONV_RC)]

    def conv_taps(ref, res, w_ref, n_taps, hist, r0, c0):
        acc = None
        for k in range(n_taps):
            o = hist - (n_taps - 1) + k
            term = w_ref[k:k + 1, c0:c0 + LANES] * ref[res.index(o % 8), pl.ds(r0 + o - o % 8, CONV_RC), c0:c0 + LANES]
            acc = term if acc is None else acc + term
        return acc

    def conv_items(buf):
        sa_ref, sb_ref, bg_ref = buf

        def conv_a(c0, r0):
            acc = conv_taps(sa_ref, res_a, wa_ref, ka, HIST_A, r0, c0)
            ua_ref[r0:r0 + CONV_RC, c0:c0 + LANES] = acc + ba_ref[:, c0:c0 + LANES]

        def conv_b(c0, r0):
            accb = conv_taps(sb_ref, res_b, wb_ref, kb, HIST_B, r0, c0)
            pb_ref[r0:r0 + CONV_RC, c0:c0 + LANES] = (bg_ref[r0:r0 + CONV_RC, c0:c0 + LANES] * accb).astype(BF16)

        return [[functools.partial(conv_a, c0, r0), functools.partial(conv_b, c0, r0)] for c0, r0 in chunks]

    def store_shifted(ref, res, hist, m0, value):
        for q, rho in enumerate(res):
            ref[q, pl.ds(hist + m0 - rho, value.shape[0]), :] = value

    def run(todo, n):
        for _ in range(min(n, len(todo))):
            for item in todo.pop(0):
                item()

    def project(buf, todo):
        sa_ref, sb_ref, bg_ref = buf

        def proj(k, m0):
            return (jnp.dot(xbf_ref[m0:m0 + IN_HM, :], wbf_ref[slot, k], preferred_element_type=F32)
                    + b_refs[k][...])

        run(todo, 1)
        for m0 in range(0, tm, IN_HM):
            p0 = proj(0, m0)
            run(todo, 1)
            p1 = proj(1, m0)
            run(todo, 1)
            store_shifted(sa_ref, res_a, HIST_A, m0, p0 * _sigmoid(p1))
            p3 = proj(3, m0)
            run(todo, 1)
            p4 = proj(4, m0)
            run(todo, 1)
            store_shifted(sb_ref, res_b, HIST_B, m0, p3 * p4)
            bg_ref[m0:m0 + IN_HM, :] = proj(2, m0)
            run(todo, 1)
            sga_ref[m0:m0 + IN_HM, :] = _sigmoid(proj(5, m0)).astype(BF16)
            run(todo, 1)
            sgb_ref[m0:m0 + IN_HM, :] = _sigmoid(proj(6, m0)).astype(BF16)
            run(todo, 1)
        run(todo, len(todo))

    for par in range(2):
        @pl.when(jnp.logical_and(i < n_tiles, i % 2 == par))
        def _():
            project(bufs[par], conv_items(bufs[1 - par]))

    @pl.when(i == n_tiles)
    def _():
        todo = conv_items(bufs[(n_tiles - 1) % 2])
        run(todo, len(todo))

    for par in range(2):
        sa_ref, sb_ref, _ = bufs[par]

        @pl.when(jnp.logical_and(jnp.logical_and(t >= 1, t % 2 == par), seq_pos == tiles_per_seq - 1))
        def _():
            na_ref[...] = last_rows(sa_ref, res_a, HIST_A, ka)
            nb_ref[...] = last_rows(sb_ref, res_b, HIST_B, kb)


def _inproj_pipe(xp, xs_bf, w, b, wa, ba, wb, n_seq, seq, n_s, n_meta):
    rp, d = xp.shape
    ka, kb = wa.shape[0], wb.shape[0]
    tiles_per_seq = seq // IN_TM
    n_prompt_tiles = rp // IN_TM
    n_tiles = n_prompt_tiles + 1
    nj = d // IN_TN
    units_per_block = d // W_UNIT_ROWS
    n_units = N_PROJ_BLOCKS * units_per_block
    last = n_prompt_tiles - 1

    def wnext_map(j, i):
        u = jnp.where(j == nj - 1, n_units - 1, jnp.minimum(i, n_units - 1))
        col = jnp.minimum(j + 1, nj - 1)
        return (u % units_per_block, (u // units_per_block) * nj + col)

    def projected(i):
        return jnp.clip(i - 1, 0, last)

    def convolved(i):
        return jnp.clip(i - 2, 0, last)

    b_specs = [pl.BlockSpec((1, IN_TN), functools.partial(lambda j, i, k: (0, k * nj + j), k=k))
               for k in range(N_PROJ_BLOCKS)]
    ch_spec = lambda rows: pl.BlockSpec((rows, IN_TN), lambda j, i: (0, j))
    proj_spec = pl.BlockSpec((IN_TM, IN_TN), lambda j, i: (projected(i), j))
    conv_spec = pl.BlockSpec((IN_TM, IN_TN), lambda j, i: (convolved(i), j))
    state_spec = lambda rows: pl.BlockSpec((None, rows, IN_TN), lambda j, i: (convolved(i) // tiles_per_seq, 0, j))
    small_spec = pl.BlockSpec((IN_TM, IN_TN), lambda j, i: (0, j))
    sds = jax.ShapeDtypeStruct
    n_res_a, n_res_b = len(_residues(ka, HIST_A)), len(_residues(kb, HIST_B))
    assert (n_s + n_meta) % 8 == 0 and n_s + n_meta + max(HIST_A, HIST_B) <= IN_TM
    tile_bufs = [pltpu.VMEM((n_res_a, HIST_A + IN_TM, IN_TN), F32), pltpu.VMEM((n_res_b, HIST_B + IN_TM, IN_TN), F32),
                 pltpu.VMEM((IN_TM, IN_TN), F32)]
    return pl.pallas_call(
        functools.partial(_inproj_pipe_kernel, ka, kb, tiles_per_seq, n_tiles, n_s, n_meta),
        grid=(nj, n_tiles + 1),
        in_specs=[pl.BlockSpec((IN_TM, d), lambda j, i: (projected(i), 0)),
                  pl.BlockSpec((IN_TM, d), lambda j, i: (0, 0)),
                  pl.BlockSpec((W_UNIT_ROWS, IN_TN), wnext_map),
                  pl.BlockSpec(memory_space=pl.ANY)] + b_specs + [ch_spec(ka), ch_spec(1), ch_spec(kb)],
        out_specs=[conv_spec, conv_spec, proj_spec, proj_spec, state_spec(ka - 1), state_spec(kb - 1),
                   small_spec, small_spec, small_spec, small_spec, small_spec],
        out_shape=[sds((rp, d), F32), sds((rp, d), BF16), sds((rp, d), BF16), sds((rp, d), BF16),
                   sds((n_seq, ka - 1, d), F32), sds((n_seq, kb - 1, d), F32),
                   sds((IN_TM, d), F32), sds((IN_TM, d), F32), sds((IN_TM, d), F32),
                   sds((IN_TM, d), BF16), sds((IN_TM, d), BF16)],
        scratch_shapes=[pltpu.VMEM((2, N_PROJ_BLOCKS, d, IN_TN), BF16),
                        pltpu.VMEM((2, W_UNIT_ROWS, IN_TN), F32), pltpu.SemaphoreType.DMA((2,)),
                        pltpu.VMEM((IN_TM, d), BF16)] + tile_bufs + tile_bufs + [
                        pltpu.VMEM((n_res_a, HIST_A, IN_TN), F32), pltpu.VMEM((n_res_b, HIST_B, IN_TN), F32)],
        compiler_params=_cparams(2),
        name="inproj_conv",
    )(xp, xs_bf, w, w, *([b] * N_PROJ_BLOCKS), wa, ba, wb)


def _conv_sample_kernel(ka, kb, sta_ref, stb_ref, ga_ref, cb_ref, bg_ref, wa_ref, ba_ref, wb_ref,
                        ua_ref, pb_ref, nsa_ref, nsb_ref):
    ga = ga_ref[...]
    cb = cb_ref[...]
    acc = wa_ref[ka - 1:ka, :] * ga
    for k in range(ka - 1):
        acc = acc + wa_ref[k:k + 1, :] * sta_ref[k]
    ua_ref[...] = acc + ba_ref[...]
    accb = wb_ref[kb - 1:kb, :] * cb
    for k in range(kb - 1):
        accb = accb + wb_ref[k:k + 1, :] * stb_ref[k]
    pb_ref[...] = (bg_ref[...] * accb).astype(BF16)
    for k in range(ka - 2):
        nsa_ref[k] = sta_ref[k + 1]
    nsa_ref[ka - 2] = ga
    for k in range(kb - 2):
        nsb_ref[k] = stb_ref[k + 1]
    nsb_ref[kb - 2] = cb


def _conv_sample(state_a, state_b, ga, cb, bg, row_block, wa, ba, wb):
    _, n, d = state_a.shape
    ka, kb = wa.shape[0], wb.shape[0]
    row_spec = pl.BlockSpec((n, CONV_TC), lambda c: (row_block, c))
    out_spec = pl.BlockSpec((n, CONV_TC), lambda c: (0, c))
    state_spec = lambda rows: pl.BlockSpec((rows, n, CONV_TC), lambda c: (0, 0, c))
    return pl.pallas_call(
        functools.partial(_conv_sample_kernel, ka, kb),
        grid=(d // CONV_TC,),
        in_specs=[state_spec(ka - 1), state_spec(kb - 1),
                  row_spec, row_spec, row_spec,
                  pl.BlockSpec((ka, CONV_TC), lambda c: (0, c)),
                  pl.BlockSpec((1, CONV_TC), lambda c: (0, c)),
                  pl.BlockSpec((kb, CONV_TC), lambda c: (0, c))],
        out_specs=[out_spec, out_spec, state_spec(ka - 1), state_spec(kb - 1)],
        out_shape=[jax.ShapeDtypeStruct((n, d), F32), jax.ShapeDtypeStruct((n, d), BF16),
                   jax.ShapeDtypeStruct(state_a.shape, F32), jax.ShapeDtypeStruct(state_b.shape, F32)],
        compiler_params=_cparams(1),
        name="conv_sample",
    )(state_a, state_b, ga, cb, bg, wa, ba, wb)


ROUTE_ID, ROUTE_RANK, ROUTE_W = 0, TOP_K, 2 * TOP_K


def _route_tile(lg, carry):
    tm = lg.shape[0]
    lane = lax.broadcasted_iota(jnp.int32, (tm, LANES), 1)
    neg_inf = jnp.float32(-jnp.inf)

    def first_max(v):
        m = jnp.max(v, axis=-1, keepdims=True)
        return m, jnp.min(jnp.where(v == m, lane, LANES), axis=-1, keepdims=True)

    g_mask = lane < N_GROUPS
    g_max, g_sel = first_max(jnp.where(g_mask, lg, neg_inf))
    g_w = 1.0 / jnp.sum(jnp.where(g_mask, jnp.exp(lg - g_max), 0.0), axis=-1, keepdims=True)
    lane0 = N_GROUPS + g_sel * EXPERTS_PER_GROUP
    e_lg = jnp.where(jnp.logical_and(lane >= lane0, lane < lane0 + EXPERTS_PER_GROUP), lg, neg_inf)
    m1, l1 = first_max(e_lg)
    m2, l2 = first_max(jnp.where(lane == l1, neg_inf, e_lg))
    r = jnp.exp(m2 - m1)
    c1 = g_w / (1.0 + r)
    c2 = g_w * r / (1.0 + r)

    a1 = lane == l1
    a2 = lane == l2
    hit = jnp.where(jnp.logical_or(a1, a2), 1.0, 0.0)
    row = lax.broadcasted_iota(jnp.int32, (tm, tm), 0)
    col = lax.broadcasted_iota(jnp.int32, (tm, tm), 1)
    before = jnp.where(col < row, 1.0, 0.0).astype(BF16)
    seen = jnp.dot(before, hit.astype(BF16), preferred_element_type=F32) + carry
    rank1 = jnp.sum(jnp.where(a1, seen, 0.0), axis=-1, keepdims=True)
    rank2 = jnp.sum(jnp.where(a2, seen, 0.0), axis=-1, keepdims=True)
    carry = carry + jnp.sum(hit, axis=0, keepdims=True)

    rec = jnp.zeros((tm, LANES), F32)
    fields = [(l1 - N_GROUPS).astype(F32), (l2 - N_GROUPS).astype(F32), rank1, rank2, c1, c2]
    for n, v in enumerate(fields):
        rec = jnp.where(lane == n, v, rec)
    return rec, carry


def _mixer_kernel(alpha, n_tiles, ua_ref, pb_ref, sga_ref, sgb_ref, x_ref, wa_hbm, wb_hbm, wo_hbm,
                  lnag_ref, lnab_ref, ln1g_ref, ln1b_ref, wrf_ref, br_ref, *refs):
    x1_ref, rt_ref, ri_ref, cnt_ref = refs[-11:-7]
    carry_ref, wa_ref, wb_ref, wo_ref, wr_ref, stage_ref, sem_ref = refs[-7:]
    i = pl.program_id(0)

    @pl.when(i == 0)
    def _():
        carry_ref[...] = jnp.zeros_like(carry_ref)
        n_stage, rows = stage_ref.shape[0], stage_ref.shape[1]
        jobs = [(src, dst, r0) for src, dst in ((wa_hbm, wa_ref), (wb_hbm, wb_ref), (wo_hbm, wo_ref))
                for r0 in range(0, src.shape[0], rows)]

        def chunk_copy(n):
            src, _, r0 = jobs[n]
            return pltpu.make_async_copy(src.at[pl.ds(r0, rows)], stage_ref.at[n % n_stage], sem_ref.at[n % n_stage])

        for n in range(n_stage - 1):
            chunk_copy(n).start()
        for n, (_, dst, r0) in enumerate(jobs):
            if n + n_stage - 1 < len(jobs):
                chunk_copy(n + n_stage - 1).start()
            chunk_copy(n).wait()
            dst[r0:r0 + rows, :] = stage_ref[n % n_stage].astype(BF16)
        wr = wrf_ref[...]
        wr_hi = wr.astype(BF16)
        wr_ref[:, 0:LANES] = wr_hi
        wr_ref[:, LANES:2 * LANES] = (wr - wr_hi.astype(F32)).astype(BF16)

    def tile(rows, ua_t, pb_t, sga_t, sgb_t, x_t):
        yb = jnp.dot(pb_t[...], wb_ref[...], preferred_element_type=F32)
        un = _layer_norm(ua_t[...], lnag_ref[...], lnab_ref[...])
        act = (un * _sigmoid(un)).astype(BF16)
        ya = jnp.dot(act, wa_ref[...], preferred_element_type=F32)
        m = (sga_t[...].astype(F32) * ya + sgb_t[...].astype(F32) * yb).astype(BF16)
        mixed = jnp.dot(m, wo_ref[...], preferred_element_type=F32)
        x1 = _layer_norm(alpha * x_t[...] + mixed, ln1g_ref[...], ln1b_ref[...])
        x1_ref[0:rows, :] = x1
        hi = x1.astype(BF16)
        lo = (x1 - hi.astype(F32)).astype(BF16)
        a = jnp.dot(hi, wr_ref[...], preferred_element_type=F32)
        b = jnp.dot(lo, wr_ref[...], preferred_element_type=F32)
        lg = a[:, :LANES] + a[:, LANES:] + b[:, :LANES] + br_ref[...]
        rec, carry = _route_tile(lg, carry_ref[...])
        carry_ref[...] = carry
        rt_ref[0:rows, :] = rec
        ri_ref[:, 0:rows] = rec.T[0:2 * TOP_K, :].astype(jnp.int32)

    @pl.when(i < n_tiles)
    def _():
        tile(x_ref.shape[0], ua_ref, pb_ref, sga_ref, sgb_ref, x_ref)

    if len(refs) == 16:
        tail_refs = refs[:5]

        @pl.when(i == n_tiles)
        def _():
            tile(tail_refs[0].shape[0], *tail_refs)

    cnt_ref[...] = carry_ref[...]


def _mixer(tm, n_tiles, alpha, ua, pb, sga, sgb, x, wa, wb, wo, lnag, lnab, ln1g, ln1b, wr, br, tail=None):
    d = x.shape[1]
    last = n_tiles - 1
    in_spec = pl.BlockSpec((tm, d), lambda i: (jnp.minimum(i, last), 0))
    vec_spec = pl.BlockSpec((1, d), lambda i: (0, 0))
    lane_spec = pl.BlockSpec((1, LANES), lambda i: (0, 0))
    w_spec = pl.BlockSpec(memory_space=pl.ANY)
    in_specs = [in_spec, in_spec, in_spec, in_spec, in_spec, w_spec, w_spec, w_spec,
                vec_spec, vec_spec, vec_spec, vec_spec,
                pl.BlockSpec((d, LANES), lambda i: (0, 0)), lane_spec]
    args = [ua, pb, sga, sgb, x, wa, wb, wo, lnag, lnab, ln1g, ln1b, wr, br]
    n_rows, n_steps, n_tail = n_tiles * tm, n_tiles, 0
    if tail is not None:
        n_tail = tail[0].shape[0]
        assert n_tail <= tm and all(a.shape[0] >= n_tail for a in tail)
        in_specs += [pl.BlockSpec((n_tail, d), lambda i: (0, 0), pipeline_mode=pl.Buffered(1))] * len(tail)
        args += list(tail)
        n_rows, n_steps = n_rows + n_tail, n_steps + 1
    return pl.pallas_call(
        functools.partial(_mixer_kernel, alpha, n_tiles),
        grid=(n_steps,),
        in_specs=in_specs,
        out_specs=[pl.BlockSpec((tm, d), lambda i: (i, 0)), pl.BlockSpec((tm, LANES), lambda i: (i, 0)),
                   pl.BlockSpec((2 * TOP_K, tm), lambda i: (0, i)), lane_spec],
        out_shape=[jax.ShapeDtypeStruct((n_rows, d), F32), jax.ShapeDtypeStruct((n_rows, LANES), F32),
                   jax.ShapeDtypeStruct((2 * TOP_K, n_rows), jnp.int32), jax.ShapeDtypeStruct((1, LANES), F32)],
        scratch_shapes=[pltpu.VMEM((1, LANES), F32), pltpu.VMEM((d, d), BF16), pltpu.VMEM((d, d), BF16),
                        pltpu.VMEM((d, d), BF16), pltpu.VMEM((d, 2 * LANES), BF16),
                        pltpu.VMEM((MIX_WSTAGE, MIX_WROWS, d), F32), pltpu.SemaphoreType.DMA((MIX_WSTAGE,))],
        compiler_params=_cparams(1, 3 * d * d * 2 + MIX_WSTAGE * MIX_WROWS * d * 4 + (2 * 18 + 12) * tm * d
                                 + 18 * n_tail * d + (2 << 20)),
        name="mixer",
    )(*args)


def _plan_kernel(n_tok, n_blocks, *refs):
    id_refs = refs[0:TOP_K]
    rank_refs = refs[TOP_K:2 * TOP_K]
    cnt_ref, be_ref, tok_ref, dst_ref, nu_ref, start_ref = refs[2 * TOP_K:]
    shift = MOE_BM.bit_length() - 1

    def per_expert(e, blk0):
        cnt = cnt_ref[0, N_GROUPS + e]
        nb = lax.shift_right_logical(cnt + (MOE_BM - 1), shift)
        start_ref[e] = blk0 * MOE_BM

        def fill(j, carry):
            be_ref[blk0 + j] = e
            return carry

        lax.fori_loop(0, nb, fill, 0)

        def pad(s, carry):
            tok_ref[s] = 0
            return carry

        lax.fori_loop(blk0 * MOE_BM + cnt, (blk0 + nb) * MOE_BM, pad, 0)
        return blk0 + nb

    n_used = lax.fori_loop(0, N_EXPERTS, per_expert, 0)
    nu_ref[0] = n_used

    def rest(b, carry):
        be_ref[b] = N_EXPERTS - 1

        def pad(s, c):
            tok_ref[b * MOE_BM + s] = 0
            return c

        lax.fori_loop(0, MOE_BM, pad, 0, unroll=8)
        return carry

    lax.fori_loop(n_used, n_blocks, rest, 0)

    def place(t, carry):
        for k in range(TOP_K):
            slot = start_ref[id_refs[k][t]] + rank_refs[k][t]
            dst_ref[k * n_tok + t] = slot
            tok_ref[slot] = t
        return carry

    lax.fori_loop(0, n_tok, place, 0, unroll=8)


def _plan(ids, ranks, cnt):
    n_tok = ids[0].shape[0]
    n_blocks = -(-n_tok * TOP_K // MOE_BM) + N_EXPERTS
    smem = pl.BlockSpec(memory_space=pltpu.SMEM)
    return pl.pallas_call(
        functools.partial(_plan_kernel, n_tok, n_blocks),
        in_specs=[smem] * (2 * TOP_K + 1),
        out_specs=[smem, smem, smem, smem],
        out_shape=[jax.ShapeDtypeStruct((n_blocks,), jnp.int32), jax.ShapeDtypeStruct((n_blocks * MOE_BM,), jnp.int32),
                   jax.ShapeDtypeStruct((TOP_K * n_tok,), jnp.int32), jax.ShapeDtypeStruct((1,), jnp.int32)],
        scratch_shapes=[pltpu.SMEM((N_EXPERTS,), jnp.int32)],
        name="plan",
    )(*ids, *ranks, cnt)


def _moe_kernel(be_ref, tok_ref, nused_ref, cnt_ref, x_hbm, wg_hbm, wu_hbm, wd_hbm, ys_ref,
                xbuf_ref, gsem_ref, wgf_ref, wuf_ref, wdf_ref, wsem_ref, wgu_ref, wdb_ref, ord_ref):
    b = pl.program_id(0)
    n_used = nused_ref[0]
    de2 = wgu_ref.shape[1]
    de = de2 // 2
    shift = MOE_BM.bit_length() - 1

    def weight_copies(e, slot):
        return [pltpu.make_async_copy(wg_hbm.at[e], wgf_ref.at[slot], wsem_ref.at[slot]),
                pltpu.make_async_copy(wu_hbm.at[e], wuf_ref.at[slot], wsem_ref.at[slot]),
                pltpu.make_async_copy(wd_hbm.at[e], wdf_ref.at[slot], wsem_ref.at[slot])]

    def row_copy(blk, slot, r):
        return pltpu.make_async_copy(x_hbm.at[pl.ds(tok_ref[blk * MOE_BM + r], 1)],
                                     xbuf_ref.at[slot, pl.ds(r, 1)], gsem_ref.at[slot])

    n_wslots = wgf_ref.shape[0]
    last_blk = be_ref.shape[0] - 1

    def blocks_of(e):
        return lax.shift_right_logical(cnt_ref[0, N_GROUPS + e] + (MOE_BM - 1), shift)

    @pl.when(jnp.logical_and(b == 0, n_used > 0))
    def _():
        ord_ref[0] = 0
        blk = jnp.int32(0)
        for ahead in range(n_wslots - 1):
            e_ahead = be_ref[jnp.minimum(blk, last_blk)]

            @pl.when(blk < n_used)
            def _():
                for cp in weight_copies(e_ahead, ahead):
                    cp.start(priority=WEIGHT_DMA_PRIORITY)

            blk = blk + blocks_of(e_ahead)
        for ahead in range(MOE_AHEAD):
            for r in range(MOE_BM):
                row_copy(jnp.minimum(ahead, n_used - 1), ahead, r).start()

    @pl.when(b < n_used)
    def _():
        e = be_ref[b]

        @pl.when(jnp.logical_or(b == 0, e != be_ref[jnp.maximum(b - 1, 0)]))
        def _():
            order = ord_ref[0]
            wslot = order % n_wslots
            for cp in weight_copies(e, wslot):
                cp.wait()
            blk = b
            for _ in range(n_wslots - 1):
                blk = blk + blocks_of(be_ref[jnp.minimum(blk, last_blk)])
                blk = jnp.minimum(blk, n_used)

            @pl.when(blk < n_used)
            def _():
                for cp in weight_copies(be_ref[jnp.minimum(blk, last_blk)], (order + n_wslots - 1) % n_wslots):
                    cp.start(priority=WEIGHT_DMA_PRIORITY)

            wgu_ref[:, 0:de] = wgf_ref[wslot].astype(BF16)
            wgu_ref[:, de:de2] = wuf_ref[wslot].astype(BF16)
            wdb_ref[...] = wdf_ref[wslot].astype(BF16)
            ord_ref[0] = order + 1

        n_buf = MOE_AHEAD + 1
        slot = b % n_buf
        for r in range(MOE_BM):
            row_copy(b, slot, r).wait()
        nxt_blk = jnp.minimum(b + MOE_AHEAD, n_used - 1)
        nxt_slot = (b + MOE_AHEAD) % n_buf
        xb = xbuf_ref[slot].astype(BF16)
        n_chunks = de2 // MOE_NC
        per = MOE_BM // n_chunks
        gu = []
        for c in range(n_chunks):
            for r in range(c * per, (c + 1) * per):
                row_copy(nxt_blk, nxt_slot, r).start()
            gu.append(jnp.dot(xb, wgu_ref[:, c * MOE_NC:(c + 1) * MOE_NC], preferred_element_type=F32))
        half = n_chunks // 2
        y = None
        for c in range(half):
            g = gu[c]
            h = (g * _sigmoid(g) * gu[half + c]).astype(BF16)
            part = jnp.dot(h, wdb_ref[c * MOE_NC:(c + 1) * MOE_NC, :], preferred_element_type=F32)
            y = part if y is None else y + part
        ys_ref[...] = y

    @pl.when(b == n_used - 1)
    def _():
        for ahead in range(1, MOE_AHEAD + 1):
            for r in range(MOE_BM):
                row_copy(b, (b + ahead) % (MOE_AHEAD + 1), r).wait()

    @pl.when(b >= n_used)
    def _():
        ys_ref[...] = jnp.zeros_like(ys_ref)


def _moe(block_expert, slot_tok, n_used, cnt, x1, wg, wu, wd):
    n_blocks = block_expert.shape[0]
    n_slots = slot_tok.shape[0]
    _, d, de = wg.shape
    any_spec = pl.BlockSpec(memory_space=pl.ANY)
    grid_spec = pltpu.PrefetchScalarGridSpec(
        num_scalar_prefetch=4,
        grid=(n_blocks,),
        in_specs=[any_spec, any_spec, any_spec, any_spec],
        out_specs=pl.BlockSpec((MOE_BM, d), lambda b, *_: (b, 0)),
        scratch_shapes=[pltpu.VMEM((MOE_AHEAD + 1, MOE_BM, d), F32), pltpu.SemaphoreType.DMA((MOE_AHEAD + 1,)),
                        pltpu.VMEM((MOE_WSLOTS, d, de), F32), pltpu.VMEM((MOE_WSLOTS, d, de), F32),
                        pltpu.VMEM((MOE_WSLOTS, de, d), F32), pltpu.SemaphoreType.DMA((MOE_WSLOTS,)),
                        pltpu.VMEM((d, 2 * de), BF16), pltpu.VMEM((de, d), BF16), pltpu.SMEM((1,), jnp.int32)],
    )
    return pl.pallas_call(
        _moe_kernel,
        grid_spec=grid_spec,
        out_shape=jax.ShapeDtypeStruct((n_slots, d), F32),
        compiler_params=_cparams(1),
        name="moe_ffn",
    )(block_expert, slot_tok, n_used, cnt, x1, wg, wu, wd)


def _combine_kernel(alpha, tile_off, n_tok, dst_ref, ys_hbm, x1_ref, rt_ref, g_ref, b_ref, out_ref,
                    buf_ref, sem_ref):
    i = pl.program_id(0)
    n = pl.num_programs(0)
    tm = x1_ref.shape[0]

    def row_copy(tile, slot, r, k):
        src = dst_ref[k * n_tok + (tile + tile_off) * tm + r]
        return pltpu.make_async_copy(ys_hbm.at[pl.ds(src, 1)], buf_ref.at[slot, k, pl.ds(r, 1)], sem_ref.at[slot])

    n_buf = CMB_AHEAD + 1

    @pl.when(i == 0)
    def _():
        for ahead in range(CMB_AHEAD):
            for r in range(tm):
                for k in range(TOP_K):
                    row_copy(jnp.minimum(ahead, n - 1), ahead, r, k).start()

    slot = i % n_buf
    for r in range(tm):
        for k in range(TOP_K):
            row_copy(i, slot, r, k).wait()
    nxt = jnp.minimum(i + CMB_AHEAD, n - 1)
    nxt_slot = (i + CMB_AHEAD) % n_buf
    for r0 in range(0, tm, CMB_RC):
        for r in range(r0, r0 + CMB_RC):
            for k in range(TOP_K):
                row_copy(nxt, nxt_slot, r, k).start()
        rt = rt_ref[r0:r0 + CMB_RC, :]
        f = (rt[:, ROUTE_W:ROUTE_W + 1] * buf_ref[slot, 0, r0:r0 + CMB_RC, :]
             + rt[:, ROUTE_W + 1:ROUTE_W + 2] * buf_ref[slot, 1, r0:r0 + CMB_RC, :])
        out_ref[r0:r0 + CMB_RC, :] = _layer_norm(alpha * x1_ref[r0:r0 + CMB_RC, :] + f, g_ref[...], b_ref[...])

    @pl.when(i == n - 1)
    def _():
        for ahead in range(1, n_buf):
            for r in range(tm):
                for k in range(TOP_K):
                    row_copy(i, (i + ahead) % n_buf, r, k).wait()


def _combine(tm, n_tiles, tile_off, alpha, dst, ys, x1, rt, g, b):
    n_tok, d = x1.shape
    grid_spec = pltpu.PrefetchScalarGridSpec(
        num_scalar_prefetch=1,
        grid=(n_tiles,),
        in_specs=[pl.BlockSpec(memory_space=pl.ANY),
                  pl.BlockSpec((tm, d), lambda i, *_: (i + tile_off, 0)),
                  pl.BlockSpec((tm, LANES), lambda i, *_: (i + tile_off, 0)),
                  pl.BlockSpec((1, d), lambda i, *_: (0, 0)),
                  pl.BlockSpec((1, d), lambda i, *_: (0, 0))],
        out_specs=pl.BlockSpec((tm, d), lambda i, *_: (i, 0)),
        scratch_shapes=[pltpu.VMEM((CMB_AHEAD + 1, TOP_K, tm, d), F32), pltpu.SemaphoreType.DMA((CMB_AHEAD + 1,))],
    )
    return pl.pallas_call(
        functools.partial(_combine_kernel, alpha, tile_off, n_tok),
        grid_spec=grid_spec,
        out_shape=jax.ShapeDtypeStruct((n_tiles * tm, d), F32),
        compiler_params=_cparams(1),
        name="combine",
    )(dst, ys, x1, rt, g, b)


def kernel(x_prompt, x_sample, state_conv_a, state_conv_b, meta_tokens, w_in, b_in, conv_a_w, conv_a_b, ln_a_g, ln_a_b, w_a_out, conv_b_w, w_b_out, w_o, ln1_g, ln1_b, w_router_group, b_router_group, w_router_expert, b_router_expert, w_exp_gate, w_exp_up, w_exp_down, ln2_g, ln2_b):
    depth = w_in.shape[0]
    assert depth == 1, "single-layer step only"
    n_seq, seq, d = x_prompt.shape
    n_s = x_sample.shape[0]
    n_meta = meta_tokens.shape[0]
    ka, kb = conv_a_w.shape[1], conv_b_w.shape[1]
    assert x_sample.shape[1] == 1 and seq % IN_TM == 0 and IN_TM >= ka - 1
    assert n_meta <= HIST_A and ka - 1 <= HIST_A and kb - 1 <= HIST_B and kb - 1 <= n_meta
    assert n_s + n_meta <= IN_TM and n_s % CMB_TM == 0
    alpha = (2.0 * depth) ** 0.25
    rp = n_seq * seq

    xp = x_prompt.reshape(rp, d)
    xs = jnp.concatenate([x_sample.reshape(n_s, d), meta_tokens,
                          jnp.zeros((IN_TM - n_s - n_meta, d), F32)], axis=0).astype(BF16)
    (ua_p, pb_p, sga_p, sgb_p, new_a_p, new_b_p, ga_x, cb_x, bg_x, sga_x, sgb_x) = _inproj_pipe(
        xp, xs, w_in[0], b_in, conv_a_w[0], conv_a_b, conv_b_w[0], n_seq, seq, n_s, n_meta)
    ua_s, pb_s, new_a_s, new_b_s = _conv_sample(
        jnp.transpose(state_conv_a[0], (1, 0, 2)), jnp.transpose(state_conv_b[0], (1, 0, 2)),
        ga_x, cb_x, bg_x, 0, conv_a_w[0], conv_a_b, conv_b_w[0])
    new_a_s = jnp.transpose(new_a_s, (1, 0, 2))[None]
    new_b_s = jnp.transpose(new_b_s, (1, 0, 2))[None]

    wr_f = jnp.concatenate([w_router_group[0], w_router_expert[0].transpose(1, 0, 2).reshape(d, N_EXPERTS),
                            jnp.zeros((d, LANES - N_GROUPS - N_EXPERTS), F32)], axis=1)
    br = jnp.concatenate([b_router_group[0], b_router_expert[0].reshape(-1),
                          jnp.zeros((LANES - N_GROUPS - N_EXPERTS,), F32)])[None, :]
    x1, rt, ri, cnt = _mixer(MIX_TM, rp // MIX_TM, alpha, ua_p, pb_p, sga_p, sgb_p, xp, w_a_out[0], w_b_out[0], w_o[0],
                             ln_a_g, ln_a_b, ln1_g, ln1_b, wr_f, br,
                             tail=(ua_s, pb_s, sga_x, sgb_x, x_sample.reshape(n_s, d)))

    cnt_i = cnt.astype(jnp.int32)
    ids = [ri[ROUTE_ID + k] for k in range(TOP_K)]
    ranks = [ri[ROUTE_RANK + k] for k in range(TOP_K)]
    block_expert, slot_tok, dest, n_used = _plan(ids, ranks, cnt_i)
    ys = _moe(block_expert, slot_tok, n_used, cnt_i, x1, w_exp_gate[0], w_exp_up[0], w_exp_down[0])
    y_p = _combine(CMB_TM, rp // CMB_TM, 0, alpha, dest, ys, x1, rt, ln2_g, ln2_b)
    y_s = _combine(CMB_TM, n_s // CMB_TM, rp // CMB_TM, alpha, dest, ys, x1, rt, ln2_g, ln2_b)

    return (y_p.reshape(n_seq, seq, d), y_s.reshape(n_s, 1, d), new_a_p[None], new_b_p[None], new_a_s, new_b_s)
```

```python
import functools

import jax
import jax.numpy as jnp
from jax import lax
from jax.experimental import pallas as pl
from jax.experimental.pallas import tpu as pltpu

F32 = jnp.float32
BF16 = jnp.bfloat16

LN_EPS = 1e-5
N_GROUPS = 4
EXPERTS_PER_GROUP = 8
N_EXPERTS = N_GROUPS * EXPERTS_PER_GROUP
TOP_K = 2
N_PROJ_BLOCKS = 7

VMEM_LIMIT_BYTES = 56 * 1024 * 1024
LANES = 128

IN_TM = 512
IN_TN = 256
IN_HM = 512
W_UNIT_ROWS = 1024
CONV_TC = 256
CONV_RC = 32
HIST_A = 32
HIST_B = 8
MIX_TM = 256
MIX_WROWS = 64
MIX_WSTAGE = 4
MOE_BM = 128
MOE_NC = 256
MOE_AHEAD = 4
MOE_WSLOTS = 2
WEIGHT_DMA_PRIORITY = 1
CMB_TM = 256
CMB_RC = 32
CMB_AHEAD = 3


def _cparams(n_axes, vmem_bytes=None):
    limit = VMEM_LIMIT_BYTES if vmem_bytes is None else min(int(vmem_bytes), VMEM_LIMIT_BYTES)
    return pltpu.CompilerParams(dimension_semantics=("arbitrary",) * n_axes, vmem_limit_bytes=limit)


def _sigmoid(x):
    return 1.0 / (1.0 + jnp.exp(-x))


def _layer_norm(x, g, b):
    mu = jnp.mean(x, axis=-1, keepdims=True)
    xc = x - mu
    var = jnp.mean(xc * xc, axis=-1, keepdims=True)
    return xc * lax.rsqrt(var + LN_EPS) * g + b


def _residues(n_taps, hist):
    return sorted({(hist - (n_taps - 1) + k) % 8 for k in range(n_taps)})


def _inproj_pipe_kernel(ka, kb, tiles_per_seq, n_tiles, n_s, n_meta, xp_ref, xs_ref, wnext_ref, w_hbm, *refs):
    b_refs = refs[0:7]
    wa_ref, ba_ref, wb_ref = refs[7:10]
    ua_ref, pb_ref, sga_ref, sgb_ref, na_ref, nb_ref = refs[10:16]
    gas_ref, cbs_ref, bgs_ref, sgas_ref, sgbs_ref = refs[16:21]
    wbf_ref, stage_ref, sem_ref, xbf_ref = refs[21:25]
    bufs = (refs[25:28], refs[28:31])
    ha_ref, hb_ref = refs[31:33]
    j = pl.program_id(0)
    i = pl.program_id(1)
    nj = pl.num_programs(0)
    tm, d = xbf_ref.shape
    tn = ha_ref.shape[-1]
    unit_rows = stage_ref.shape[1]
    units_per_block = d // unit_rows
    n_units = N_PROJ_BLOCKS * units_per_block
    t = i - 1

    @pl.when(jnp.logical_and(j == 0, i == 0))
    def _():
        def unit_copy(u):
            k, h = divmod(u, units_per_block)
            return pltpu.make_async_copy(
                w_hbm.at[pl.ds(h * unit_rows, unit_rows), pl.ds(k * d, tn)], stage_ref.at[u % 2], sem_ref.at[u % 2])

        unit_copy(0).start()
        for u in range(n_units):
            if u + 1 < n_units:
                unit_copy(u + 1).start()
            unit_copy(u).wait()
            k, h = divmod(u, units_per_block)
            wbf_ref[0, k, h * unit_rows:(h + 1) * unit_rows, :] = stage_ref[u % 2].astype(BF16)
        for buf in bufs:
            for ref in buf:
                ref[...] = jnp.zeros_like(ref)

    @pl.when(jnp.logical_and(i < n_units, j + 1 < nj))
    def _():
        k = i // units_per_block
        h = i % units_per_block
        row0 = pl.multiple_of(h * unit_rows, unit_rows)
        wbf_ref[(j + 1) % 2, k, pl.ds(row0, unit_rows), :] = wnext_ref[...].astype(BF16)

    @pl.when(i == 0)
    def _():
        xbf_ref[...] = xs_ref[...]

    @pl.when(jnp.logical_and(i > 0, i < n_tiles))
    def _():
        xbf_ref[...] = xp_ref[...].astype(BF16)

    res_a, res_b = _residues(ka, HIST_A), _residues(kb, HIST_B)
    meta_end = n_s + n_meta

    def raw_rows(ref, res, hist):
        return ref[res.index(0), hist:hist + tm, :]

    def last_rows(ref, res, hist, n_taps):
        rho = (hist - (n_taps - 1)) % 8
        start = hist + tm - (n_taps - 1) - rho
        return ref[res.index(rho), start:start + n_taps - 1, :]

    def set_history(ref, res, hist, src, row0):
        for q, rho in enumerate(res):
            if hist - rho > 0:
                ref[q, 0:hist - rho, :] = src[q, row0:row0 + hist - rho, :]

    @pl.when(i == 1)
    def _():
        sa_ref, sb_ref, bg_ref = bufs[0]
        gas_ref[...] = raw_rows(sa_ref, res_a, HIST_A)
        cbs_ref[...] = raw_rows(sb_ref, res_b, HIST_B)
        bgs_ref[...] = bg_ref[...]
        sgas_ref[...] = sga_ref[...]
        sgbs_ref[...] = sgb_ref[...]
        for h_ref, src, res, hist in ((ha_ref, sa_ref, res_a, HIST_A), (hb_ref, sb_ref, res_b, HIST_B)):
            for q, rho in enumerate(res):
                h_ref[q] = src[q, meta_end:meta_end + hist, :]
                n_zero = hist - n_meta - rho
                if n_zero > 0:
                    h_ref[q, 0:n_zero, :] = jnp.zeros((n_zero, tn), F32)

    seq_pos = (t - 1) % tiles_per_seq
    for par in range(2):
        sa_ref, sb_ref, _ = bufs[par]
        sa_prev, sb_prev, _ = bufs[1 - par]
        is_t = jnp.logical_and(t >= 1, t % 2 == par)

        @pl.when(jnp.logical_and(is_t, seq_pos == 0))
        def _():
            set_history(sa_ref, res_a, HIST_A, ha_ref, 0)
            set_history(sb_ref, res_b, HIST_B, hb_ref, 0)

        @pl.when(jnp.logical_and(is_t, seq_pos != 0))
        def _():
            set_history(sa_ref, res_a, HIST_A, sa_prev, tm)
            set_history(sb_ref, res_b, HIST_B, sb_prev, tm)

    slot = j % 2
    chunks = [(c0, r0) for c0 in range(0, tn, LANES) for r0 in range(0, tm, CONV_RC)]

    def conv_taps(ref, res, w_ref, n_taps, hist, r0, c0):
        acc = None
        for k in range(n_taps):
            o = hist - (n_taps - 1) + k
            term = w_ref[k:k + 1, c0:c0 + LANES] * ref[res.index(o % 8), pl.ds(r0 + o - o % 8, CONV_RC), c0:c0 + LANES]
            acc = term if acc is None else acc + term
        return acc

    def conv_items(buf):
        sa_ref, sb_ref, bg_ref = buf

        def conv_a(c0, r0):
            acc = conv_taps(sa_ref, res_a, wa_ref, ka, HIST_A, r0, c0)
            ua_ref[r0:r0 + CONV_RC, c0:c0 + LANES] = acc + ba_ref[:, c0:c0 + LANES]

        def conv_b(c0, r0):
            accb = conv_taps(sb_ref, res_b, wb_ref, kb, HIST_B, r0, c0)
            pb_ref[r0:r0 + CONV_RC, c0:c0 + LANES] = (bg_ref[r0:r0 + CONV_RC, c0:c0 + LANES] * accb).astype(BF16)

        return [[functools.partial(conv_a, c0, r0), functools.partial(conv_b, c0, r0)] for c0, r0 in chunks]

    def store_shifted(ref, res, hist, m0, value):
        for q, rho in enumerate(res):
            ref[q, pl.ds(hist + m0 - rho, value.shape[0]), :] = value

    def run(todo, n):
        for _ in range(min(n, len(todo))):
            for item in todo.pop(0):
                item()

    def project(buf, todo):
        sa_ref, sb_ref, bg_ref = buf

        def proj(k, m0):
            return (jnp.dot(xbf_ref[m0:m0 + IN_HM, :], wbf_ref[slot, k], preferred_element_type=F32)
                    + b_refs[k][...])

        run(todo, 1)
        for m0 in range(0, tm, IN_HM):
            p0 = proj(0, m0)
            run(todo, 1)
            p1 = proj(1, m0)
            run(todo, 1)
            store_shifted(sa_ref, res_a, HIST_A, m0, p0 * _sigmoid(p1))
            p3 = proj(3, m0)
            run(todo, 1)
            p4 = proj(4, m0)
            run(todo, 1)
            store_shifted(sb_ref, res_b, HIST_B, m0, p3 * p4)
            bg_ref[m0:m0 + IN_HM, :] = proj(2, m0)
            run(todo, 1)
            sga_ref[m0:m0 + IN_HM, :] = _sigmoid(proj(5, m0)).astype(BF16)
            run(todo, 1)
            sgb_ref[m0:m0 + IN_HM, :] = _sigmoid(proj(6, m0)).astype(BF16)
            run(todo, 1)
        run(todo, len(todo))

    for par in range(2):
        @pl.when(jnp.logical_and(i < n_tiles, i % 2 == par))
        def _():
            project(bufs[par], conv_items(bufs[1 - par]))

    @pl.when(i == n_tiles)
    def _():
        todo = conv_items(bufs[(n_tiles - 1) % 2])
        run(todo, len(todo))

    for par in range(2):
        sa_ref, sb_ref, _ = bufs[par]

        @pl.when(jnp.logical_and(jnp.logical_and(t >= 1, t % 2 == par), seq_pos == tiles_per_seq - 1))
        def _():
            na_ref[...] = last_rows(sa_ref, res_a, HIST_A, ka)
            nb_ref[...] = last_rows(sb_ref, res_b, HIST_B, kb)


def _inproj_pipe(xp, xs_bf, w, b, wa, ba, wb, n_seq, seq, n_s, n_meta):
    rp, d = xp.shape
    ka, kb = wa.shape[0], wb.shape[0]
    tiles_per_seq = seq // IN_TM
    n_prompt_tiles = rp // IN_TM
    n_tiles = n_prompt_tiles + 1
    nj = d // IN_TN
    units_per_block = d // W_UNIT_ROWS
    n_units = N_PROJ_BLOCKS * units_per_block
    last = n_prompt_tiles - 1

    def wnext_map(j, i):
        u = jnp.where(j == nj - 1, n_units - 1, jnp.minimum(i, n_units - 1))
        col = jnp.minimum(j + 1, nj - 1)
        return (u % units_per_block, (u // units_per_block) * nj + col)

    def projected(i):
        return jnp.clip(i - 1, 0, last)

    def convolved(i):
        return jnp.clip(i - 2, 0, last)

    b_specs = [pl.BlockSpec((1, IN_TN), functools.partial(lambda j, i, k: (0, k * nj + j), k=k))
               for k in range(N_PROJ_BLOCKS)]
    ch_spec = lambda rows: pl.BlockSpec((rows, IN_TN), lambda j, i: (0, j))
    proj_spec = pl.BlockSpec((IN_TM, IN_TN), lambda j, i: (projected(i), j))
    conv_spec = pl.BlockSpec((IN_TM, IN_TN), lambda j, i: (convolved(i), j))
    state_spec = lambda rows: pl.BlockSpec((None, rows, IN_TN), lambda j, i: (convolved(i) // tiles_per_seq, 0, j))
    small_spec = pl.BlockSpec((IN_TM, IN_TN), lambda j, i: (0, j))
    sds = jax.ShapeDtypeStruct
    n_res_a, n_res_b = len(_residues(ka, HIST_A)), len(_residues(kb, HIST_B))
    assert (n_s + n_meta) % 8 == 0 and n_s + n_meta + max(HIST_A, HIST_B) <= IN_TM
    tile_bufs = [pltpu.VMEM((n_res_a, HIST_A + IN_TM, IN_TN), F32), pltpu.VMEM((n_res_b, HIST_B + IN_TM, IN_TN), F32),
                 pltpu.VMEM((IN_TM, IN_TN), F32)]
    return pl.pallas_call(
        functools.partial(_inproj_pipe_kernel, ka, kb, tiles_per_seq, n_tiles, n_s, n_meta),
        grid=(nj, n_tiles + 1),
        in_specs=[pl.BlockSpec((IN_TM, d), lambda j, i: (projected(i), 0)),
                  pl.BlockSpec((IN_TM, d), lambda j, i: (0, 0)),
                  pl.BlockSpec((W_UNIT_ROWS, IN_TN), wnext_map),
                  pl.BlockSpec(memory_space=pl.ANY)] + b_specs + [ch_spec(ka), ch_spec(1), ch_spec(kb)],
        out_specs=[conv_spec, conv_spec, proj_spec, proj_spec, state_spec(ka - 1), state_spec(kb - 1),
                   small_spec, small_spec, small_spec, small_spec, small_spec],
        out_shape=[sds((rp, d), F32), sds((rp, d), BF16), sds((rp, d), BF16), sds((rp, d), BF16),
                   sds((n_seq, ka - 1, d), F32), sds((n_seq, kb - 1, d), F32),
                   sds((IN_TM, d), F32), sds((IN_TM, d), F32), sds((IN_TM, d), F32),
                   sds((IN_TM, d), BF16), sds((IN_TM, d), BF16)],
        scratch_shapes=[pltpu.VMEM((2, N_PROJ_BLOCKS, d, IN_TN), BF16),
                        pltpu.VMEM((2, W_UNIT_ROWS, IN_TN), F32), pltpu.SemaphoreType.DMA((2,)),
                        pltpu.VMEM((IN_TM, d), BF16)] + tile_bufs + tile_bufs + [
                        pltpu.VMEM((n_res_a, HIST_A, IN_TN), F32), pltpu.VMEM((n_res_b, HIST_B, IN_TN), F32)],
        compiler_params=_cparams(2),
        name="inproj_conv",
    )(xp, xs_bf, w, w, *([b] * N_PROJ_BLOCKS), wa, ba, wb)


def _conv_sample_kernel(ka, kb, sta_ref, stb_ref, ga_ref, cb_ref, bg_ref, wa_ref, ba_ref, wb_ref,
                        ua_ref, pb_ref, nsa_ref, nsb_ref):
    ga = ga_ref[...]
    cb = cb_ref[...]
    acc = wa_ref[ka - 1:ka, :] * ga
    for k in range(ka - 1):
        acc = acc + wa_ref[k:k + 1, :] * sta_ref[k]
    ua_ref[...] = acc + ba_ref[...]
    accb = wb_ref[kb - 1:kb, :] * cb
    for k in range(kb - 1):
        accb = accb + wb_ref[k:k + 1, :] * stb_ref[k]
    pb_ref[...] = (bg_ref[...] * accb).astype(BF16)
    for k in range(ka - 2):
        nsa_ref[k] = sta_ref[k + 1]
    nsa_ref[ka - 2] = ga
    for k in range(kb - 2):
        nsb_ref[k] = stb_ref[k + 1]
    nsb_ref[kb - 2] = cb


def _conv_sample(state_a, state_b, ga, cb, bg, row_block, wa, ba, wb):
    _, n, d = state_a.shape
    ka, kb = wa.shape[0], wb.shape[0]
    row_spec = pl.BlockSpec((n, CONV_TC), lambda c: (row_block, c))
    out_spec = pl.BlockSpec((n, CONV_TC), lambda c: (0, c))
    state_spec = lambda rows: pl.BlockSpec((rows, n, CONV_TC), lambda c: (0, 0, c))
    return pl.pallas_call(
        functools.partial(_conv_sample_kernel, ka, kb),
        grid=(d // CONV_TC,),
        in_specs=[state_spec(ka - 1), state_spec(kb - 1),
                  row_spec, row_spec, row_spec,
                  pl.BlockSpec((ka, CONV_TC), lambda c: (0, c)),
                  pl.BlockSpec((1, CONV_TC), lambda c: (0, c)),
                  pl.BlockSpec((kb, CONV_TC), lambda c: (0, c))],
        out_specs=[out_spec, out_spec, state_spec(ka - 1), state_spec(kb - 1)],
        out_shape=[jax.ShapeDtypeStruct((n, d), F32), jax.ShapeDtypeStruct((n, d), BF16),
                   jax.ShapeDtypeStruct(state_a.shape, F32), jax.ShapeDtypeStruct(state_b.shape, F32)],
        compiler_params=_cparams(1),
        name="conv_sample",
    )(state_a, state_b, ga, cb, bg, wa, ba, wb)


ROUTE_ID, ROUTE_RANK, ROUTE_W = 0, TOP_K, 2 * TOP_K


def _route_tile(lg, carry):
    tm = lg.shape[0]
    lane = lax.broadcasted_iota(jnp.int32, (tm, LANES), 1)
    neg_inf = jnp.float32(-jnp.inf)

    def first_max(v):
        m = jnp.max(v, axis=-1, keepdims=True)
        return m, jnp.min(jnp.where(v == m, lane, LANES), axis=-1, keepdims=True)

    g_mask = lane < N_GROUPS
    g_max, g_sel = first_max(jnp.where(g_mask, lg, neg_inf))
    g_w = 1.0 / jnp.sum(jnp.where(g_mask, jnp.exp(lg - g_max), 0.0), axis=-1, keepdims=True)
    lane0 = N_GROUPS + g_sel * EXPERTS_PER_GROUP
    e_lg = jnp.where(jnp.logical_and(lane >= lane0, lane < lane0 + EXPERTS_PER_GROUP), lg, neg_inf)
    m1, l1 = first_max(e_lg)
    m2, l2 = first_max(jnp.where(lane == l1, neg_inf, e_lg))
    r = jnp.exp(m2 - m1)
    c1 = g_w / (1.0 + r)
    c2 = g_w * r / (1.0 + r)

    a1 = lane == l1
    a2 = lane == l2
    hit = jnp.where(jnp.logical_or(a1, a2), 1.0, 0.0)
    row = lax.broadcasted_iota(jnp.int32, (tm, tm), 0)
    col = lax.broadcasted_iota(jnp.int32, (tm, tm), 1)
    before = jnp.where(col < row, 1.0, 0.0).astype(BF16)
    seen = jnp.dot(before, hit.astype(BF16), preferred_element_type=F32) + carry
    rank1 = jnp.sum(jnp.where(a1, seen, 0.0), axis=-1, keepdims=True)
    rank2 = jnp.sum(jnp.where(a2, seen, 0.0), axis=-1, keepdims=True)
    carry = carry + jnp.sum(hit, axis=0, keepdims=True)

    rec = jnp.zeros((tm, LANES), F32)
    fields = [(l1 - N_GROUPS).astype(F32), (l2 - N_GROUPS).astype(F32), rank1, rank2, c1, c2]
    for n, v in enumerate(fields):
        rec = jnp.where(lane == n, v, rec)
    return rec, carry


def _mixer_kernel(alpha, n_tiles, ua_ref, pb_ref, sga_ref, sgb_ref, x_ref, wa_hbm, wb_hbm, wo_hbm,
                  lnag_ref, lnab_ref, ln1g_ref, ln1b_ref, wrf_ref, br_ref, *refs):
    x1_ref, rt_ref, ri_ref, cnt_ref = refs[-11:-7]
    carry_ref, wa_ref, wb_ref, wo_ref, wr_ref, stage_ref, sem_ref = refs[-7:]
    i = pl.program_id(0)

    @pl.when(i == 0)
    def _():
        carry_ref[...] = jnp.zeros_like(carry_ref)
        n_stage, rows = stage_ref.shape[0], stage_ref.shape[1]
        jobs = [(src, dst, r0) for src, dst in ((wa_hbm, wa_ref), (wb_hbm, wb_ref), (wo_hbm, wo_ref))
                for r0 in range(0, src.shape[0], rows)]

        def chunk_copy(n):
            src, _, r0 = jobs[n]
            return pltpu.make_async_copy(src.at[pl.ds(r0, rows)], stage_ref.at[n % n_stage], sem_ref.at[n % n_stage])

        for n in range(n_stage - 1):
            chunk_copy(n).start()
        for n, (_, dst, r0) in enumerate(jobs):
            if n + n_stage - 1 < len(jobs):
                chunk_copy(n + n_stage - 1).start()
            chunk_copy(n).wait()
            dst[r0:r0 + rows, :] = stage_ref[n % n_stage].astype(BF16)
        wr = wrf_ref[...]
        wr_hi = wr.astype(BF16)
        wr_ref[:, 0:LANES] = wr_hi
        wr_ref[:, LANES:2 * LANES] = (wr - wr_hi.astype(F32)).astype(BF16)

    def tile(rows, ua_t, pb_t, sga_t, sgb_t, x_t):
        yb = jnp.dot(pb_t[...], wb_ref[...], preferred_element_type=F32)
        un = _layer_norm(ua_t[...], lnag_ref[...], lnab_ref[...])
        act = (un * _sigmoid(un)).astype(BF16)
        ya = jnp.dot(act, wa_ref[...], preferred_element_type=F32)
        m = (sga_t[...].astype(F32) * ya + sgb_t[...].astype(F32) * yb).astype(BF16)
        mixed = jnp.dot(m, wo_ref[...], preferred_element_type=F32)
        x1 = _layer_norm(alpha * x_t[...] + mixed, ln1g_ref[...], ln1b_ref[...])
        x1_ref[0:rows, :] = x1
        hi = x1.astype(BF16)
        lo = (x1 - hi.astype(F32)).astype(BF16)
        a = jnp.dot(hi, wr_ref[...], preferred_element_type=F32)
        b = jnp.dot(lo, wr_ref[...], preferred_element_type=F32)
        lg = a[:, :LANES] + a[:, LANES:] + b[:, :LANES] + br_ref[...]
        rec, carry = _route_tile(lg, carry_ref[...])
        carry_ref[...] = carry
        rt_ref[0:rows, :] = rec
        ri_ref[:, 0:rows] = rec.T[0:2 * TOP_K, :].astype(jnp.int32)

    @pl.when(i < n_tiles)
    def _():
        tile(x_ref.shape[0], ua_ref, pb_ref, sga_ref, sgb_ref, x_ref)

    if len(refs) == 16:
        tail_refs = refs[:5]

        @pl.when(i == n_tiles)
        def _():
            tile(tail_refs[0].shape[0], *tail_refs)

    cnt_ref[...] = carry_ref[...]


def _mixer(tm, n_tiles, alpha, ua, pb, sga, sgb, x, wa, wb, wo, lnag, lnab, ln1g, ln1b, wr, br, tail=None):
    d = x.shape[1]
    last = n_tiles - 1
    in_spec = pl.BlockSpec((tm, d), lambda i: (jnp.minimum(i, last), 0))
    vec_spec = pl.BlockSpec((1, d), lambda i: (0, 0))
    lane_spec = pl.BlockSpec((1, LANES), lambda i: (0, 0))
    w_spec = pl.BlockSpec(memory_space=pl.ANY)
    in_specs = [in_spec, in_spec, in_spec, in_spec, in_spec, w_spec, w_spec, w_spec,
                vec_spec, vec_spec, vec_spec, vec_spec,
                pl.BlockSpec((d, LANES), lambda i: (0, 0)), lane_spec]
    args = [ua, pb, sga, sgb, x, wa, wb, wo, lnag, lnab, ln1g, ln1b, wr, br]
    n_rows, n_steps, n_tail = n_tiles * tm, n_tiles, 0
    if tail is not None:
        n_tail = tail[0].shape[0]
        assert n_tail <= tm and all(a.shape[0] >= n_tail for a in tail)
        in_specs += [pl.BlockSpec((n_tail, d), lambda i: (0, 0), pipeline_mode=pl.Buffered(1))] * len(tail)
        args += list(tail)
        n_rows, n_steps = n_rows + n_tail, n_steps + 1
    return pl.pallas_call(
        functools.partial(_mixer_kernel, alpha, n_tiles),
        grid=(n_steps,),
        in_specs=in_specs,
        out_specs=[pl.BlockSpec((tm, d), lambda i: (i, 0)), pl.BlockSpec((tm, LANES), lambda i: (i, 0)),
                   pl.BlockSpec((2 * TOP_K, tm), lambda i: (0, i)), lane_spec],
        out_shape=[jax.ShapeDtypeStruct((n_rows, d), F32), jax.ShapeDtypeStruct((n_rows, LANES), F32),
                   jax.ShapeDtypeStruct((2 * TOP_K, n_rows), jnp.int32), jax.ShapeDtypeStruct((1, LANES), F32)],
        scratch_shapes=[pltpu.VMEM((1, LANES), F32), pltpu.VMEM((d, d), BF16), pltpu.VMEM((d, d), BF16),
                        pltpu.VMEM((d, d), BF16), pltpu.VMEM((d, 2 * LANES), BF16),
                        pltpu.VMEM((MIX_WSTAGE, MIX_WROWS, d), F32), pltpu.SemaphoreType.DMA((MIX_WSTAGE,))],
        compiler_params=_cparams(1, 3 * d * d * 2 + MIX_WSTAGE * MIX_WROWS * d * 4 + (2 * 18 + 12) * tm * d
                                 + 18 * n_tail * d + (2 << 20)),
        name="mixer",
    )(*args)


def _plan_kernel(n_tok, n_blocks, *refs):
    id_refs = refs[0:TOP_K]
    rank_refs = refs[TOP_K:2 * TOP_K]
    cnt_ref, be_ref, tok_ref, dst_ref, nu_ref, start_ref = refs[2 * TOP_K:]
    shift = MOE_BM.bit_length() - 1

    def per_expert(e, blk0):
        cnt = cnt_ref[0, N_GROUPS + e]
        nb = lax.shift_right_logical(cnt + (MOE_BM - 1), shift)
        start_ref[e] = blk0 * MOE_BM

        def fill(j, carry):
            be_ref[blk0 + j] = e
            return carry

        lax.fori_loop(0, nb, fill, 0)

        def pad(s, carry):
            tok_ref[s] = 0
            return carry

        lax.fori_loop(blk0 * MOE_BM + cnt, (blk0 + nb) * MOE_BM, pad, 0)
        return blk0 + nb

    n_used = lax.fori_loop(0, N_EXPERTS, per_expert, 0)
    nu_ref[0] = n_used

    def rest(b, carry):
        be_ref[b] = N_EXPERTS - 1

        def pad(s, c):
            tok_ref[b * MOE_BM + s] = 0
            return c

        lax.fori_loop(0, MOE_BM, pad, 0, unroll=8)
        return carry

    lax.fori_loop(n_used, n_blocks, rest, 0)

    def place(t, carry):
        for k in range(TOP_K):
            slot = start_ref[id_refs[k][t]] + rank_refs[k][t]
            dst_ref[k * n_tok + t] = slot
            tok_ref[slot] = t
        return carry

    lax.fori_loop(0, n_tok, place, 0, unroll=8)


def _plan(ids, ranks, cnt):
    n_tok = ids[0].shape[0]
    n_blocks = -(-n_tok * TOP_K // MOE_BM) + N_EXPERTS
    smem = pl.BlockSpec(memory_space=pltpu.SMEM)
    return pl.pallas_call(
        functools.partial(_plan_kernel, n_tok, n_blocks),
        in_specs=[smem] * (2 * TOP_K + 1),
        out_specs=[smem, smem, smem, smem],
        out_shape=[jax.ShapeDtypeStruct((n_blocks,), jnp.int32), jax.ShapeDtypeStruct((n_blocks * MOE_BM,), jnp.int32),
                   jax.ShapeDtypeStruct((TOP_K * n_tok,), jnp.int32), jax.ShapeDtypeStruct((1,), jnp.int32)],
        scratch_shapes=[pltpu.SMEM((N_EXPERTS,), jnp.int32)],
        name="plan",
    )(*ids, *ranks, cnt)


def _moe_kernel(be_ref, tok_ref, nused_ref, cnt_ref, x_hbm, wg_hbm, wu_hbm, wd_hbm, ys_ref,
                xbuf_ref, gsem_ref, wgf_ref, wuf_ref, wdf_ref, wsem_ref, wgu_ref, wdb_ref, ord_ref):
    b = pl.program_id(0)
    n_used = nused_ref[0]
    de2 = wgu_ref.shape[1]
    de = de2 // 2
    shift = MOE_BM.bit_length() - 1

    def weight_copies(e, slot):
        return [pltpu.make_async_copy(wg_hbm.at[e], wgf_ref.at[slot], wsem_ref.at[slot]),
                pltpu.make_async_copy(wu_hbm.at[e], wuf_ref.at[slot], wsem_ref.at[slot]),
                pltpu.make_async_copy(wd_hbm.at[e], wdf_ref.at[slot], wsem_ref.at[slot])]

    def row_copy(blk, slot, r):
        return pltpu.make_async_copy(x_hbm.at[pl.ds(tok_ref[blk * MOE_BM + r], 1)],
                                     xbuf_ref.at[slot, pl.ds(r, 1)], gsem_ref.at[slot])

    n_wslots = wgf_ref.shape[0]
    last_blk = be_ref.shape[0] - 1

    def blocks_of(e):
        return lax.shift_right_logical(cnt_ref[0, N_GROUPS + e] + (MOE_BM - 1), shift)

    @pl.when(jnp.logical_and(b == 0, n_used > 0))
    def _():
        ord_ref[0] = 0
        blk = jnp.int32(0)
        for ahead in range(n_wslots - 1):
            e_ahead = be_ref[jnp.minimum(blk, last_blk)]

            @pl.when(blk < n_used)
            def _():
                for cp in weight_copies(e_ahead, ahead):
                    cp.start(priority=WEIGHT_DMA_PRIORITY)

            blk = blk + blocks_of(e_ahead)
        for ahead in range(MOE_AHEAD):
            for r in range(MOE_BM):
                row_copy(jnp.minimum(ahead, n_used - 1), ahead, r).start()

    @pl.when(b < n_used)
    def _():
        e = be_ref[b]

        @pl.when(jnp.logical_or(b == 0, e != be_ref[jnp.maximum(b - 1, 0)]))
        def _():
            order = ord_ref[0]
            wslot = order % n_wslots
            for cp in weight_copies(e, wslot):
                cp.wait()
            blk = b
            for _ in range(n_wslots - 1):
                blk = blk + blocks_of(be_ref[jnp.minimum(blk, last_blk)])
                blk = jnp.minimum(blk, n_used)

            @pl.when(blk < n_used)
            def _():
                for cp in weight_copies(be_ref[jnp.minimum(blk, last_blk)], (order + n_wslots - 1) % n_wslots):
                    cp.start(priority=WEIGHT_DMA_PRIORITY)

            wgu_ref[:, 0:de] = wgf_ref[wslot].astype(BF16)
            wgu_ref[:, de:de2] = wuf_ref[wslot].astype(BF16)
            wdb_ref[...] = wdf_ref[wslot].astype(BF16)
            ord_ref[0] = order + 1

        n_buf = MOE_AHEAD + 1
        slot = b % n_buf
        for r in range(MOE_BM):
            row_copy(b, slot, r).wait()
        nxt_blk = jnp.minimum(b + MOE_AHEAD, n_used - 1)
        nxt_slot = (b + MOE_AHEAD) % n_buf
        xb = xbuf_ref[slot].astype(BF16)
        n_chunks = de2 // MOE_NC
        per = MOE_BM // n_chunks
        gu = []
        for c in range(n_chunks):
            for r in range(c * per, (c + 1) * per):
                row_copy(nxt_blk, nxt_slot, r).start()
            gu.append(jnp.dot(xb, wgu_ref[:, c * MOE_NC:(c + 1) * MOE_NC], preferred_element_type=F32))
        half = n_chunks // 2
        y = None
        for c in range(half):
            g = gu[c]
            h = (g * _sigmoid(g) * gu[half + c]).astype(BF16)
            part = jnp.dot(h, wdb_ref[c * MOE_NC:(c + 1) * MOE_NC, :], preferred_element_type=F32)
            y = part if y is None else y + part
        ys_ref[...] = y

    @pl.when(b == n_used - 1)
    def _():
        for ahead in range(1, MOE_AHEAD + 1):
            for r in range(MOE_BM):
                row_copy(b, (b + ahead) % (MOE_AHEAD + 1), r).wait()

    @pl.when(b >= n_used)
    def _():
        ys_ref[...] = jnp.zeros_like(ys_ref)


def _moe(block_expert, slot_tok, n_used, cnt, x1, wg, wu, wd):
    n_blocks = block_expert.shape[0]
    n_slots = slot_tok.shape[0]
    _, d, de = wg.shape
    any_spec = pl.BlockSpec(memory_space=pl.ANY)
    grid_spec = pltpu.PrefetchScalarGridSpec(
        num_scalar_prefetch=4,
        grid=(n_blocks,),
        in_specs=[any_spec, any_spec, any_spec, any_spec],
        out_specs=pl.BlockSpec((MOE_BM, d), lambda b, *_: (b, 0)),
        scratch_shapes=[pltpu.VMEM((MOE_AHEAD + 1, MOE_BM, d), F32), pltpu.SemaphoreType.DMA((MOE_AHEAD + 1,)),
                        pltpu.VMEM((MOE_WSLOTS, d, de), F32), pltpu.VMEM((MOE_WSLOTS, d, de), F32),
                        pltpu.VMEM((MOE_WSLOTS, de, d), F32), pltpu.SemaphoreType.DMA((MOE_WSLOTS,)),
                        pltpu.VMEM((d, 2 * de), BF16), pltpu.VMEM((de, d), BF16), pltpu.SMEM((1,), jnp.int32)],
    )
    return pl.pallas_call(
        _moe_kernel,
        grid_spec=grid_spec,
        out_shape=jax.ShapeDtypeStruct((n_slots, d), F32),
        compiler_params=_cparams(1),
        name="moe_ffn",
    )(block_expert, slot_tok, n_used, cnt, x1, wg, wu, wd)


def _combine_kernel(alpha, tile_off, n_tok, dst_ref, ys_hbm, x1_ref, rt_ref, g_ref, b_ref, out_ref,
                    buf_ref, sem_ref):
    i = pl.program_id(0)
    n = pl.num_programs(0)
    tm = x1_ref.shape[0]

    def row_copy(tile, slot, r, k):
        src = dst_ref[k * n_tok + (tile + tile_off) * tm + r]
        return pltpu.make_async_copy(ys_hbm.at[pl.ds(src, 1)], buf_ref.at[slot, k, pl.ds(r, 1)], sem_ref.at[slot])

    n_buf = CMB_AHEAD + 1

    @pl.when(i == 0)
    def _():
        for ahead in range(CMB_AHEAD):
            for r in range(tm):
                for k in range(TOP_K):
                    row_copy(jnp.minimum(ahead, n - 1), ahead, r, k).start()

    slot = i % n_buf
    for r in range(tm):
        for k in range(TOP_K):
            row_copy(i, slot, r, k).wait()
    nxt = jnp.minimum(i + CMB_AHEAD, n - 1)
    nxt_slot = (i + CMB_AHEAD) % n_buf
    for r0 in range(0, tm, CMB_RC):
        for r in range(r0, r0 + CMB_RC):
            for k in range(TOP_K):
                row_copy(nxt, nxt_slot, r, k).start()
        rt = rt_ref[r0:r0 + CMB_RC, :]
        f = (rt[:, ROUTE_W:ROUTE_W + 1] * buf_ref[slot, 0, r0:r0 + CMB_RC, :]
             + rt[:, ROUTE_W + 1:ROUTE_W + 2] * buf_ref[slot, 1, r0:r0 + CMB_RC, :])
        out_ref[r0:r0 + CMB_RC, :] = _layer_norm(alpha * x1_ref[r0:r0 + CMB_RC, :] + f, g_ref[...], b_ref[...])

    @pl.when(i == n - 1)
    def _():
        for ahead in range(1, n_buf):
            for r in range(tm):
                for k in range(TOP_K):
                    row_copy(i, (i + ahead) % n_buf, r, k).wait()


def _combine(tm, n_tiles, tile_off, alpha, dst, ys, x1, rt, g, b):
    n_tok, d = x1.shape
    grid_spec = pltpu.PrefetchScalarGridSpec(
        num_scalar_prefetch=1,
        grid=(n_tiles,),
        in_specs=[pl.BlockSpec(memory_space=pl.ANY),
                  pl.BlockSpec((tm, d), lambda i, *_: (i + tile_off, 0)),
                  pl.BlockSpec((tm, LANES), lambda i, *_: (i + tile_off, 0)),
                  pl.BlockSpec((1, d), lambda i, *_: (0, 0)),
                  pl.BlockSpec((1, d), lambda i, *_: (0, 0))],
        out_specs=pl.BlockSpec((tm, d), lambda i, *_: (i, 0)),
        scratch_shapes=[pltpu.VMEM((CMB_AHEAD + 1, TOP_K, tm, d), F32), pltpu.SemaphoreType.DMA((CMB_AHEAD + 1,))],
    )
    return pl.pallas_call(
        functools.partial(_combine_kernel, alpha, tile_off, n_tok),
        grid_spec=grid_spec,
        out_shape=jax.ShapeDtypeStruct((n_tiles * tm, d), F32),
        compiler_params=_cparams(1),
        name="combine",
    )(dst, ys, x1, rt, g, b)


def kernel(x_prompt, x_sample, state_conv_a, state_conv_b, meta_tokens, w_in, b_in, conv_a_w, conv_a_b, ln_a_g, ln_a_b, w_a_out, conv_b_w, w_b_out, w_o, ln1_g, ln1_b, w_router_group, b_router_group, w_router_expert, b_router_expert, w_exp_gate, w_exp_up, w_exp_down, ln2_g, ln2_b):
    depth = w_in.shape[0]
    assert depth == 1, "single-layer step only"
    n_seq, seq, d = x_prompt.shape
    n_s = x_sample.shape[0]
    n_meta = meta_tokens.shape[0]
    ka, kb = conv_a_w.shape[1], conv_b_w.shape[1]
    assert x_sample.shape[1] == 1 and seq % IN_TM == 0 and IN_TM >= ka - 1
    assert n_meta <= HIST_A and ka - 1 <= HIST_A and kb - 1 <= HIST_B and kb - 1 <= n_meta
    rp = n_seq * seq
    assert n_s + n_meta <= IN_TM and rp % CMB_TM == 0 and rp % n_s == 0 and n_s % CMB_RC == 0
    alpha = (2.0 * depth) ** 0.25

    xp = x_prompt.reshape(rp, d)
    xs = jnp.concatenate([x_sample.reshape(n_s, d), meta_tokens,
                          jnp.zeros((IN_TM - n_s - n_meta, d), F32)], axis=0).astype(BF16)
    (ua_p, pb_p, sga_p, sgb_p, new_a_p, new_b_p, ga_x, cb_x, bg_x, sga_x, sgb_x) = _inproj_pipe(
        xp, xs, w_in[0], b_in, conv_a_w[0], conv_a_b, conv_b_w[0], n_seq, seq, n_s, n_meta)
    ua_s, pb_s, new_a_s, new_b_s = _conv_sample(
        jnp.transpose(state_conv_a[0], (1, 0, 2)), jnp.transpose(state_conv_b[0], (1, 0, 2)),
        ga_x, cb_x, bg_x, 0, conv_a_w[0], conv_a_b, conv_b_w[0])
    new_a_s = jnp.transpose(new_a_s, (1, 0, 2))[None]
    new_b_s = jnp.transpose(new_b_s, (1, 0, 2))[None]

    wr_f = jnp.concatenate([w_router_group[0], w_router_expert[0].transpose(1, 0, 2).reshape(d, N_EXPERTS),
                            jnp.zeros((d, LANES - N_GROUPS - N_EXPERTS), F32)], axis=1)
    br = jnp.concatenate([b_router_group[0], b_router_expert[0].reshape(-1),
                          jnp.zeros((LANES - N_GROUPS - N_EXPERTS,), F32)])[None, :]
    x1, rt, ri, cnt = _mixer(MIX_TM, rp // MIX_TM, alpha, ua_p, pb_p, sga_p, sgb_p, xp, w_a_out[0], w_b_out[0], w_o[0],
                             ln_a_g, ln_a_b, ln1_g, ln1_b, wr_f, br,
                             tail=(ua_s, pb_s, sga_x, sgb_x, x_sample.reshape(n_s, d)))

    cnt_i = cnt.astype(jnp.int32)
    ids = [ri[ROUTE_ID + k] for k in range(TOP_K)]
    ranks = [ri[ROUTE_RANK + k] for k in range(TOP_K)]
    block_expert, slot_tok, dest, n_used = _plan(ids, ranks, cnt_i)
    ys = _moe(block_expert, slot_tok, n_used, cnt_i, x1, w_exp_gate[0], w_exp_up[0], w_exp_down[0])
    y_p = _combine(CMB_TM, rp // CMB_TM, 0, alpha, dest, ys, x1, rt, ln2_g, ln2_b)
    y_s = _combine(n_s, 1, rp // n_s, alpha, dest, ys, x1, rt, ln2_g, ln2_b)

    return (y_p.reshape(n_seq, seq, d), y_s.reshape(n_s, 1, d), new_a_p[None], new_b_p[None], new_a_s, new_b_s)
```
